```python
import math
import jax
import jax.numpy as jnp
from jax import lax
import numpy as np

D_MODEL = 4096
BATCH = 2
SEQ = 8192
DEPTH = 4

CHUNK = 64
N_MIXERS = 2
N_RWKV = (DEPTH + N_MIXERS - 1) // N_MIXERS
N_S5 = DEPTH // N_MIXERS
N_VRES = max(N_RWKV - 1, 0)
D_FF = 3 * D_MODEL // 2
RMS_EPS = 1e-6
RWKV_HEAD = 64
RWKV_HEADS = D_MODEL // RWKV_HEAD
W_LORA = 128
A_LORA = 128
V_LORA = 96
G_LORA = 96
GN_EPS = 64e-5
S5_GROUP = 16
S5_GROUPS = D_MODEL // S5_GROUP
S5_STATE = 64
DT_MIN = 1e-3
DT_MAX = 1e-1
LAMBDA_RE_MAX = -1e-4

kernel_name = 'hybrid_rwkv7_s5_macaron_trunk'


def rms_norm(x, g):
    xf = x.astype(jnp.float32)
    y = xf * lax.rsqrt(jnp.mean(xf * xf, axis=-1, keepdims=True) + RMS_EPS)
    return (y * g.astype(jnp.float32)).astype(x.dtype)


def swiglu(h, w_in, w_out):
    gate, up = jnp.split(h @ w_in, 2, axis=-1)
    return (jax.nn.silu(gate) * up) @ w_out


def token_shift(h):
    return jnp.pad(h, ((0, 0), (1, 0), (0, 0)))[:, :-1]


def rwkv7_recurrence(r, w, k, v, kk, a):
    def step(S, inp):
        r_t, w_t, k_t, v_t, kk_t, a_t = inp
        s_kk = jnp.einsum('bhvk,bhk->bhv', S, kk_t)
        S = (S * w_t[:, :, None, :]
             - s_kk[..., None] * (kk_t * a_t)[:, :, None, :]
             + v_t[..., None] * k_t[:, :, None, :])
        return S, jnp.einsum('bhvk,bhk->bhv', S, r_t)
    bsz, _, n_heads, n = r.shape
    xs = (jnp.moveaxis(r, 1, 0), jnp.moveaxis(w, 1, 0), jnp.moveaxis(k, 1, 0),
          jnp.moveaxis(v, 1, 0), jnp.moveaxis(kk, 1, 0), jnp.moveaxis(a, 1, 0))
    s0 = jnp.zeros((bsz, n_heads, n, n), jnp.float32)
    _, ys = lax.scan(step, s0, xs)
    return jnp.moveaxis(ys, 0, 1)


def rwkv7_mix(h, v_first, mu, w_rkv, w_o, w0, w_l1, w_l2, a0, a_l1, a_l2,
              g_l1, g_l2, k_k, k_a, r_k, ln_w, ln_b, v_res):
    f32 = jnp.float32
    bsz, t, d = h.shape
    H, N = RWKV_HEADS, RWKV_HEAD
    dx = token_shift(h) - h
    x_rkv = h[None] + dx[None] * mu[:3, None, None, :]
    r, k, v = jnp.einsum('cbtd,cde->cbte', x_rkv, w_rkv)
    xw = h + dx * mu[3]
    xa = h + dx * mu[4]
    xg = h + dx * mu[5]
    log_w = -jax.nn.softplus(-(w0 + jnp.tanh(xw @ w_l1) @ w_l2).astype(f32)) - 0.5
    decay = jnp.exp(-jnp.exp(log_w))
    a = jax.nn.sigmoid((a0 + (xa @ a_l1) @ a_l2).astype(f32))
    g = jax.nn.sigmoid(xg @ g_l1) @ g_l2
    if v_res is None:
        v_first = v
    else:
        v0, v_l1, v_l2 = v_res
        v = v + (v_first - v) * jax.nn.sigmoid(v0 + (x_rkv[2] @ v_l1) @ v_l2)

    def heads(z):
        return z.astype(f32).reshape(bsz, t, H, N)

    rh, vh, ah, wh = heads(r), heads(v), heads(a), heads(decay)
    kkh = heads(k * k_k)
    kkh = kkh / jnp.maximum(jnp.sqrt(jnp.sum(kkh * kkh, axis=-1, keepdims=True)), 1e-12)
    kh = heads(k) * (1.0 + (ah - 1.0) * k_a.astype(f32).reshape(H, N))
    y = rwkv7_recurrence(rh, wh, kh, vh, kkh, ah)
    mean = jnp.mean(y, axis=-1, keepdims=True)
    var = jnp.mean(jnp.square(y - mean), axis=-1, keepdims=True)
    y = ((y - mean) * lax.rsqrt(var + GN_EPS)).reshape(bsz, t, d)
    y = y * ln_w.astype(f32) + ln_b.astype(f32)
    bonus = jnp.sum(rh * kh * r_k.astype(f32), axis=-1, keepdims=True) * vh
    y = y + bonus.reshape(bsz, t, d)
    return (y.astype(h.dtype) * g) @ w_o, v_first


def _complex_combine(e_i, e_j):
    (ai_re, ai_im), (bi_re, bi_im) = e_i
    (aj_re, aj_im), (bj_re, bj_im) = e_j
    return ((aj_re * ai_re - aj_im * ai_im, aj_re * ai_im + aj_im * ai_re),
            (aj_re * bi_re - aj_im * bi_im + bj_re, aj_re * bi_im + aj_im * bi_re + bj_im))


def s5_mix(u, lam_re, lam_im, log_step, b_re, b_im, c_re, c_im, d_skip, w_glu):
    f32 = jnp.float32
    bsz, t, d = u.shape
    G, P, L, C = S5_GROUPS, S5_STATE, CHUNK, S5_GROUP
    lr = jnp.minimum(lam_re.astype(f32), LAMBDA_RE_MAX)
    li = lam_im.astype(f32)
    dt = jnp.exp(log_step.astype(f32))[:, None]
    ldt_re, ldt_im = lr * dt, li * dt
    mag = jnp.exp(ldt_re)
    ab_re, ab_im = mag * jnp.cos(ldt_im), mag * jnp.sin(ldt_im)
    den = lr * lr + li * li
    q_re = ((ab_re - 1.0) * lr + ab_im * li) / den
    q_im = (ab_im * lr - (ab_re - 1.0) * li) / den
    br, bi = b_re.astype(f32), b_im.astype(f32)
    bb_re = q_re[..., None] * br - q_im[..., None] * bi
    bb_im = q_re[..., None] * bi + q_im[..., None] * br
    cr, ci = c_re.astype(f32), c_im.astype(f32)
    dsk = d_skip.astype(f32).reshape(G, C)
    pos = jnp.arange(1, L + 1, dtype=f32)[:, None, None]
    pmag = jnp.exp(pos * ldt_re)
    pw_re, pw_im = pmag * jnp.cos(pos * ldt_im), pmag * jnp.sin(pos * ldt_im)
    a_re = jnp.broadcast_to(ab_re, (bsz, L, G, P))
    a_im = jnp.broadcast_to(ab_im, (bsz, L, G, P))

    def chunk_step(carry, u_c):
        s_re, s_im = carry
        ug = u_c.reshape(bsz, L, G, C)
        bu_re = jnp.einsum('gpc,blgc->blgp', bb_re, ug)
        bu_im = jnp.einsum('gpc,blgc->blgp', bb_im, ug)
        _, (x_re, x_im) = lax.associative_scan(
            _complex_combine, ((a_re, a_im), (bu_re, bu_im)), axis=1)
        x_re = x_re + pw_re * s_re[:, None] - pw_im * s_im[:, None]
        x_im = x_im + pw_re * s_im[:, None] + pw_im * s_re[:, None]
        y = (jnp.einsum('gcp,blgp->blgc', cr, x_re)
             - jnp.einsum('gcp,blgp->blgc', ci, x_im)
             + dsk * ug)
        return (x_re[:, -1], x_im[:, -1]), y.reshape(bsz, L, d)

    u_chunks = jnp.moveaxis(u.astype(f32).reshape(bsz, t // L, L, d), 1, 0)
    s0 = (jnp.zeros((bsz, G, P), f32), jnp.zeros((bsz, G, P), f32))
    _, ys = lax.scan(chunk_step, s0, u_chunks)
    y = jnp.moveaxis(ys, 0, 1).reshape(bsz, t, d)
    val, gate = jnp.split(jax.nn.gelu(y).astype(u.dtype) @ w_glu, 2, axis=-1)
    return val * jax.nn.sigmoid(gate)


def setup_inputs(seed: int = 0) -> dict:
    key = jax.random.key(seed)
    ks = iter(jax.random.split(key, 48))
    f32 = jnp.float32

    def nrm(shape, scale):
        return scale * jax.random.normal(next(ks), shape, f32)

    def uni(shape, lo, hi):
        return jax.random.uniform(next(ks), shape, f32, lo, hi)

    D, F = D_MODEL, D_FF
    NR, NS, NV = N_RWKV, N_S5, N_VRES
    H, N, G, P, C = RWKV_HEADS, RWKV_HEAD, S5_GROUPS, S5_STATE, S5_GROUP
    n_idx = jnp.arange(P, dtype=f32)
    return {
        'x': nrm((BATCH, SEQ, D), 1.0),
        'ffn_norm': 1.0 + nrm((DEPTH, 2, D), 0.02),
        'ffn_w_in': nrm((DEPTH, 2, D, 2 * F), D ** -0.5),
        'ffn_w_out': nrm((DEPTH, 2, F, D), F ** -0.5),
        'mix_norm': 1.0 + nrm((DEPTH, D), 0.02),
        'rwkv_mu': uni((NR, 6, D), 0.0, 1.0),
        'rwkv_w_rkv': nrm((NR, 3, D, D), D ** -0.5),
        'rwkv_w_o': nrm((NR, D, D), D ** -0.5),
        'rwkv_w0': uni((NR, D), -6.0, -1.0),
        'rwkv_w_l1': nrm((NR, D, W_LORA), D ** -0.5),
        'rwkv_w_l2': nrm((NR, W_LORA, D), 0.1 * W_LORA ** -0.5),
        'rwkv_a0': nrm((NR, D), 0.1),
        'rwkv_a_l1': nrm((NR, D, A_LORA), D ** -0.5),
        'rwkv_a_l2': nrm((NR, A_LORA, D), 0.3 * A_LORA ** -0.5),
        'rwkv_v0': 1.0 + nrm((NV, D), 0.1),
        'rwkv_v_l1': nrm((NV, D, V_LORA), D ** -0.5),
        'rwkv_v_l2': nrm((NV, V_LORA, D), 0.3 * V_LORA ** -0.5),
        'rwkv_g_l1': nrm((NR, D, G_LORA), D ** -0.5),
        'rwkv_g_l2': nrm((NR, G_LORA, D), G_LORA ** -0.5),
        'rwkv_k_k': 0.85 + nrm((NR, D), 0.02),
        'rwkv_k_a': 1.0 + nrm((NR, D), 0.02),
        'rwkv_r_k': nrm((NR, H, N), 0.1),
        'rwkv_ln_w': 1.0 + nrm((NR, D), 0.02),
        'rwkv_ln_b': nrm((NR, D), 0.02),
        's5_lam_re': -0.5 + nrm((NS, G, P), 0.01),
        's5_lam_im': math.pi * n_idx + nrm((NS, G, P), 0.01),
        's5_log_step': uni((NS, G), math.log(DT_MIN), math.log(DT_MAX)),
        's5_b_re': nrm((NS, G, P, C), (2 * C) ** -0.5),
        's5_b_im': nrm((NS, G, P, C), (2 * C) ** -0.5),
        's5_c_re': nrm((NS, G, C, P), 2 ** -0.5),
        's5_c_im': nrm((NS, G, C, P), 2 ** -0.5),
        's5_d': nrm((NS, D), 1.0),
        's5_w_glu': nrm((NS, D, 2 * D), D ** -0.5),
        'final_norm': 1.0 + nrm((D,), 0.02),
    }


def reference(x, ffn_norm, ffn_w_in, ffn_w_out, mix_norm,
              rwkv_mu, rwkv_w_rkv, rwkv_w_o, rwkv_w0, rwkv_w_l1, rwkv_w_l2,
              rwkv_a0, rwkv_a_l1, rwkv_a_l2, rwkv_v0, rwkv_v_l1, rwkv_v_l2,
              rwkv_g_l1, rwkv_g_l2, rwkv_k_k, rwkv_k_a, rwkv_r_k, rwkv_ln_w, rwkv_ln_b,
              s5_lam_re, s5_lam_im, s5_log_step, s5_b_re, s5_b_im, s5_c_re, s5_c_im,
              s5_d, s5_w_glu, final_norm):
    v_first = None
    for i in range(DEPTH):
        x = x + 0.5 * swiglu(rms_norm(x, ffn_norm[i, 0]), ffn_w_in[i, 0], ffn_w_out[i, 0])
        h = rms_norm(x, mix_norm[i])
        j = i // N_MIXERS
        if i % N_MIXERS == 0:
            v_res = None if j == 0 else (rwkv_v0[j - 1], rwkv_v_l1[j - 1], rwkv_v_l2[j - 1])
            y, v_first = rwkv7_mix(h, v_first, rwkv_mu[j], rwkv_w_rkv[j], rwkv_w_o[j],
                                   rwkv_w0[j], rwkv_w_l1[j], rwkv_w_l2[j],
                                   rwkv_a0[j], rwkv_a_l1[j], rwkv_a_l2[j],
                                   rwkv_g_l1[j], rwkv_g_l2[j], rwkv_k_k[j], rwkv_k_a[j],
                                   rwkv_r_k[j], rwkv_ln_w[j], rwkv_ln_b[j], v_res)
        else:
            y = s5_mix(h, s5_lam_re[j], s5_lam_im[j], s5_log_step[j], s5_b_re[j], s5_b_im[j],
                       s5_c_re[j], s5_c_im[j], s5_d[j], s5_w_glu[j])
        x = x + y
        x = x + 0.5 * swiglu(rms_norm(x, ffn_norm[i, 1]), ffn_w_in[i, 1], ffn_w_out[i, 1])
    return rms_norm(x, final_norm)
```

```python
import functools
import math

import jax
import jax.numpy as jnp
from jax import lax
from jax.experimental import pallas as pl
from jax.experimental.pallas import tpu as pltpu

F32 = jnp.float32
BF16 = jnp.bfloat16

RMS_EPS = 1e-6
GN_EPS = 64e-5
LAMBDA_RE_MAX = -1e-4
HEAD = 64
CHUNK = 64
S5_GROUP = 16
LANES = 128
VMEM_LIMIT = 56 * 1024 * 1024

_NN = (((1,), (0,)), ((), ()))
_NT = (((1,), (1,)), ((), ()))
_TN = (((0,), (0,)), ((), ()))


def _params(*sem):
    return pltpu.CompilerParams(dimension_semantics=sem, vmem_limit_bytes=VMEM_LIMIT)


def _tile(n, pref, quantum):
    best = None
    t = quantum
    while t <= min(n, pref):
        if n % t == 0:
            best = t
        t += quantum
    return best if best is not None else n


def _rms_body(x_ref, g_ref, o_ref):
    x = x_ref[...]
    ms = jnp.mean(x * x, axis=-1, keepdims=True)
    o_ref[...] = (x * lax.rsqrt(ms + RMS_EPS) * g_ref[...]).astype(o_ref.dtype)


def _rmsnorm(x, g, out_dtype):
    m, d = x.shape
    bm = _tile(m, 256, 8)
    return pl.pallas_call(
        _rms_body,
        grid=(m // bm,),
        in_specs=[pl.BlockSpec((bm, d), lambda i: (i, 0)),
                  pl.BlockSpec((1, d), lambda i: (0, 0))],
        out_specs=pl.BlockSpec((bm, d), lambda i: (i, 0)),
        out_shape=jax.ShapeDtypeStruct((m, d), out_dtype),
        compiler_params=_params("parallel"),
        name="rmsnorm",
    )(x, g.reshape(1, d))


def _rms_mix_body(x_ref, g_ref, mu_ref, o_ref, carry_ref, *, tiles_per_seq):
    i = pl.program_id(0)

    @pl.when(i % tiles_per_seq == 0)
    def _():
        carry_ref[...] = jnp.zeros_like(carry_ref)

    x = x_ref[...]
    ms = jnp.mean(x * x, axis=-1, keepdims=True)
    h = x * lax.rsqrt(ms + RMS_EPS) * g_ref[...]
    bm = h.shape[0]
    row = lax.broadcasted_iota(jnp.int32, h.shape, 0)
    prev = jnp.where(row == 0, carry_ref[...], pltpu.roll(h, 1, axis=0))
    carry_ref[...] = h[bm - 1:bm, :]
    dx = prev - h
    for c in range(o_ref.shape[0]):
        o_ref[c] = (h + dx * mu_ref[c:c + 1, :]).astype(o_ref.dtype)


def _rms_mix(x, g, mu, seq):
    m, d = x.shape
    n_mix = mu.shape[0]
    bm = _tile(seq, 256, 8)
    return pl.pallas_call(
        functools.partial(_rms_mix_body, tiles_per_seq=seq // bm),
        grid=(m // bm,),
        in_specs=[pl.BlockSpec((bm, d), lambda i: (i, 0)),
                  pl.BlockSpec((1, d), lambda i: (0, 0)),
                  pl.BlockSpec((n_mix, d), lambda i: (0, 0))],
        out_specs=pl.BlockSpec((n_mix, bm, d), lambda i: (0, i, 0)),
        out_shape=jax.ShapeDtypeStruct((n_mix, m, d), BF16),
        scratch_shapes=[pltpu.VMEM((1, d), F32)],
        compiler_params=_params("arbitrary"),
        name="rms_token_shift_mix",
    )(x, g.reshape(1, d), mu)


def _dual_body(a_ref, w1_ref, w2_ref, *rest, mode):
    a = a_ref[...]
    p1 = jnp.dot(a, w1_ref[...], preferred_element_type=F32)
    p2 = jnp.dot(a, w2_ref[...], preferred_element_type=F32)
    if mode == "swiglu":
        (o_ref,) = rest
        o_ref[...] = (jax.nn.silu(p1) * p2).astype(o_ref.dtype)
    else:
        res_ref, o_ref = rest
        o_ref[...] = res_ref[...] + p1 * jax.nn.sigmoid(p2)


def _dual_matmul(a, w, mode, res=None):
    m, k = a.shape
    n = w.shape[1] // 2
    bm = _tile(m, 1024, 8)
    bn = _tile(n, 512, LANES)
    nj = n // bn
    in_specs = [pl.BlockSpec((bm, k), lambda i, j: (i, 0)),
                pl.BlockSpec((k, bn), lambda i, j: (0, j)),
                pl.BlockSpec((k, bn), lambda i, j: (0, j + nj))]
    args = [a, w, w]
    if mode == "glu_residual":
        in_specs.append(pl.BlockSpec((bm, bn), lambda i, j: (i, j)))
        args.append(res)
        out_dtype = F32
    else:
        out_dtype = BF16
    return pl.pallas_call(
        functools.partial(_dual_body, mode=mode),
        grid=(m // bm, nj),
        in_specs=in_specs,
        out_specs=pl.BlockSpec((bm, bn), lambda i, j: (i, j)),
        out_shape=jax.ShapeDtypeStruct((m, n), out_dtype),
        compiler_params=_params("parallel", "parallel"),
        name="matmul_" + mode,
    )(*args)


def _resid_body(a_ref, w_ref, res_ref, o_ref, *, scale):
    acc = jnp.dot(a_ref[...], w_ref[...], preferred_element_type=F32)
    o_ref[...] = res_ref[...] + scale * acc


def _resid_matmul(a, w, res, scale):
    m, k = a.shape
    n = w.shape[1]
    bm = _tile(m, 1024, 8)
    bn = _tile(n, 512, LANES)
    return pl.pallas_call(
        functools.partial(_resid_body, scale=scale),
        grid=(m // bm, n // bn),
        in_specs=[pl.BlockSpec((bm, k), lambda i, j: (i, 0)),
                  pl.BlockSpec((k, bn), lambda i, j: (0, j)),
                  pl.BlockSpec((bm, bn), lambda i, j: (i, j))],
        out_specs=pl.BlockSpec((bm, bn), lambda i, j: (i, j)),
        out_shape=jax.ShapeDtypeStruct((m, n), F32),
        compiler_params=_params("parallel", "parallel"),
        name="matmul_residual",
    )(a, w, res)


def _plain_body(a_ref, w_ref, o_ref):
    o_ref[...] = jnp.dot(a_ref[...], w_ref[...], preferred_element_type=F32)


def _batched_matmul(a, w, n_batch):
    _, m, k = a.shape
    n = w.shape[2]
    bm = _tile(m, 1024, 8)
    bn = _tile(n, 1024, LANES)
    return pl.pallas_call(
        _plain_body,
        grid=(n_batch, m // bm, n // bn),
        in_specs=[pl.BlockSpec((None, bm, k), lambda c, i, j: (c, i, 0)),
                  pl.BlockSpec((None, k, bn), lambda c, i, j: (c, 0, j))],
        out_specs=pl.BlockSpec((None, bm, bn), lambda c, i, j: (c, i, j)),
        out_shape=jax.ShapeDtypeStruct((n_batch, m, n), F32),
        compiler_params=_params("parallel", "parallel", "parallel"),
        name="matmul_rkv",
    )(a, w)


_DECAY_SCALE = math.exp(-0.5)


def _lora_body(x_ref, l1_ref, l2_ref, *rest, mode):
    t = jnp.dot(x_ref[...], l1_ref[...], preferred_element_type=F32)
    if mode == "decay":
        t = jnp.tanh(t)
    elif mode == "gate":
        t = jax.nn.sigmoid(t)
    z = jnp.dot(t.astype(BF16), l2_ref[...], preferred_element_type=F32)
    if mode == "decay":
        b_ref, o_ref = rest
        o_ref[...] = -_DECAY_SCALE * jax.nn.sigmoid(z + b_ref[...])
    elif mode == "lr":
        b_ref, o_ref = rest
        o_ref[...] = jax.nn.sigmoid(z + b_ref[...])
    elif mode == "gate":
        (o_ref,) = rest
        o_ref[...] = z
    else:
        b_ref, v_ref, vf_ref, o_ref = rest
        v = v_ref[...]
        o_ref[...] = v + (vf_ref[...] - v) * jax.nn.sigmoid(z + b_ref[...])


def _lora(xmix, c, l1, l2, mode, bias=None, v=None, v_first=None):
    _, m, d = xmix.shape
    rank = l1.shape[1]
    pad = (-rank) % LANES
    if pad:
        l1 = jnp.pad(l1, ((0, 0), (0, pad)))
        l2 = jnp.pad(l2, ((0, pad), (0, 0)))
    rp = rank + pad
    bm = _tile(m, 256, 8)
    row = pl.BlockSpec((bm, d), lambda i: (i, 0))
    vec = pl.BlockSpec((1, d), lambda i: (0, 0))
    in_specs = [pl.BlockSpec((None, bm, d), lambda i: (c, i, 0)),
                pl.BlockSpec((d, rp), lambda i: (0, 0)),
                pl.BlockSpec((rp, d), lambda i: (0, 0))]
    args = [xmix, l1.astype(BF16), l2.astype(BF16)]
    if mode != "gate":
        in_specs.append(vec)
        args.append(bias.reshape(1, d))
    if mode == "value_residual":
        in_specs += [row, row]
        args += [v, v_first]
    return pl.pallas_call(
        functools.partial(_lora_body, mode=mode),
        grid=(m // bm,),
        in_specs=in_specs,
        out_specs=row,
        out_shape=jax.ShapeDtypeStruct((m, d), F32),
        compiler_params=_params("parallel"),
        name="lora_" + mode,
    )(*args)


def _dot_f32(a, b, dims=_NN):
    return lax.dot_general(a, b, dims, precision=lax.Precision.HIGHEST,
                           preferred_element_type=F32)


def _rwkv_body(r_ref, k_ref, v_ref, lw_ref, a_ref, g_ref,
               kk_ref, ka_ref, rk_ref, lnw_ref, lnb_ref, o_ref, s_ref, *, n_chunks, heads):
    L, N = CHUNK, HEAD

    @pl.when(pl.program_id(2) == 0)
    def _():
        s_ref[...] = jnp.zeros_like(s_ref)

    row = lax.broadcasted_iota(jnp.int32, (L, L), 0)
    col = lax.broadcasted_iota(jnp.int32, (L, L), 1)
    tri = (col <= row).astype(F32)
    row2 = lax.broadcasted_iota(jnp.int32, (2 * L, 2 * L), 0)
    col2 = lax.broadcasted_iota(jnp.int32, (2 * L, 2 * L), 1)
    mask2 = (col2 % L) < (row2 % L) + (row2 >= L).astype(jnp.int32)
    n_double = max(1, (L - 1).bit_length())

    def chunk(c, carry):
        t0 = pl.multiple_of(c * L, L)
        outs = []
        for h in range(heads):
            ls = slice(h * N, (h + 1) * N)
            r = r_ref[pl.ds(t0, L), ls]
            k = k_ref[pl.ds(t0, L), ls]
            v = v_ref[pl.ds(t0, L), ls]
            lw = lw_ref[pl.ds(t0, L), ls]
            a = a_ref[pl.ds(t0, L), ls]
            g = g_ref[pl.ds(t0, L), ls]
            kkr = k * kk_ref[:, ls]
            nrm = jnp.sqrt(jnp.sum(kkr * kkr, axis=-1, keepdims=True))
            kk = kkr / jnp.maximum(nrm, 1e-12)
            kh = k * (1.0 + (a - 1.0) * ka_ref[:, ls])
            ba = kk * a
            lp = _dot_f32(tri, lw)
            lp_end = lp[L - 1:L, :]
            e_neg = jnp.exp(-lp)
            e_end = jnp.exp(lp_end - lp)
            at = -kk * jnp.exp(lp - lw)
            rt = r * jnp.exp(lp)
            ar = jnp.concatenate([at, rt], axis=0)
            bk = jnp.concatenate([ba * e_neg, kh * e_neg], axis=0)
            gm = jnp.where(mask2, _dot_f32(ar, bk, _NT), 0.0)
            s0 = s_ref[h]
            ars = _dot_f32(ar, s0, _NT)
            u = ars[:L] + _dot_f32(gm[:L, L:], v)
            nk = gm[:L, :L]
            for it in range(n_double):
                u = u + _dot_f32(nk, u)
                if it + 1 < n_double:
                    nk = _dot_f32(nk, nk)
            y = ars[L:] + _dot_f32(gm[L:, :L], u) + _dot_f32(gm[L:, L:], v)
            s_ref[h] = (s0 * jnp.exp(lp_end) + _dot_f32(u, ba * e_end, _TN)
                        + _dot_f32(v, kh * e_end, _TN))
            mean = jnp.mean(y, axis=-1, keepdims=True)
            yc = y - mean
            var = jnp.mean(yc * yc, axis=-1, keepdims=True)
            yn = yc * lax.rsqrt(var + GN_EPS) * lnw_ref[:, ls] + lnb_ref[:, ls]
            bonus = jnp.sum(r * kh * rk_ref[:, ls], axis=-1, keepdims=True) * v
            outs.append((yn + bonus) * g)
        o_ref[pl.ds(t0, L), :] = jnp.concatenate(outs, axis=-1).astype(o_ref.dtype)
        return carry

    lax.fori_loop(0, n_chunks, chunk, 0)


def _rwkv_recurrence(r, k, v, lw, a, g, k_k, k_a, r_k, ln_w, ln_b):
    b, t, d = r.shape
    heads = LANES // HEAD
    tb = _tile(t, 512, CHUNK)
    tok = pl.BlockSpec((None, tb, LANES), lambda bi, hi, ti: (bi, ti, hi))
    par = pl.BlockSpec((1, LANES), lambda bi, hi, ti: (0, hi))
    return pl.pallas_call(
        functools.partial(_rwkv_body, n_chunks=tb // CHUNK, heads=heads),
        grid=(b, d // LANES, t // tb),
        in_specs=[tok] * 6 + [par] * 5,
        out_specs=tok,
        out_shape=jax.ShapeDtypeStruct((b, t, d), BF16),
        scratch_shapes=[pltpu.VMEM((heads, HEAD, HEAD), F32)],
        compiler_params=_params("parallel", "parallel", "arbitrary"),
        name="rwkv7_recurrence",
    )(r, k, v, lw, a, g, *(p.reshape(1, d) for p in (k_k, k_a, r_k, ln_w, ln_b)))


def _rwkv_layer(x, seq, v_first, norm_g, mu, w_rkv, w_o, w0, w_l1, w_l2, a0, a_l1, a_l2,
                g_l1, g_l2, k_k, k_a, r_k, ln_w, ln_b, v_res):
    m, d = x.shape
    bsz = m // seq
    xmix = _rms_mix(x, norm_g, mu, seq)
    rkv = _batched_matmul(xmix, w_rkv.astype(BF16), 3)
    r, k, v = rkv[0], rkv[1], rkv[2]
    lw = _lora(xmix, 3, w_l1, w_l2, "decay", bias=w0)
    a = _lora(xmix, 4, a_l1, a_l2, "lr", bias=a0)
    g = _lora(xmix, 5, g_l1, g_l2, "gate")
    if v_res is None:
        v_first = v
    else:
        v0, v_l1, v_l2 = v_res
        v = _lora(xmix, 2, v_l1, v_l2, "value_residual", bias=v0, v=v, v_first=v_first)
    sh = (bsz, seq, d)
    yg = _rwkv_recurrence(r.reshape(sh), k.reshape(sh), v.reshape(sh), lw.reshape(sh),
                          a.reshape(sh), g.reshape(sh), k_k, k_a, r_k, ln_w, ln_b)
    return _resid_matmul(yg.reshape(m, d), w_o.astype(BF16), x, 1.0), v_first


def _s5_body(u_ref, wt_ref, tt_ref, vt_ref, c1_ref, c2_ref, o_ref, *, n_chunks, state):
    u = u_ref[...]
    s = jnp.dot(u, wt_ref[...], preferred_element_type=F32)
    chunk_idx = lax.broadcasted_iota(jnp.int32, s.shape, 0) % n_chunks
    step = 1
    it = 0
    while step < n_chunks:
        sp = jnp.where(chunk_idx >= step, pltpu.roll(s, step, axis=0), 0.0)
        s = s + c1_ref[it:it + 1, :] * sp + c2_ref[it:it + 1, :] * pltpu.roll(sp, state, axis=1)
        step *= 2
        it += 1
    s_start = jnp.where(chunk_idx >= 1, pltpu.roll(s, 1, axis=0), 0.0)
    y = (jnp.dot(u, tt_ref[...], preferred_element_type=F32)
         + jnp.dot(s_start.astype(BF16), vt_ref[...], preferred_element_type=F32))
    o_ref[...] = jax.nn.gelu(y).astype(o_ref.dtype)


def _s5_operators(lam_re, lam_im, log_step, b_re, b_im, c_re, c_im, d_skip, n_chunks):
    g, p = lam_re.shape
    c = S5_GROUP
    L = CHUNK
    lr = jnp.minimum(lam_re.astype(F32), LAMBDA_RE_MAX)
    li = lam_im.astype(F32)
    dt = jnp.exp(log_step.astype(F32))[:, None]
    ldt_re, ldt_im = lr * dt, li * dt
    mag = jnp.exp(ldt_re)
    ab_re, ab_im = mag * jnp.cos(ldt_im), mag * jnp.sin(ldt_im)
    den = lr * lr + li * li
    q_re = ((ab_re - 1.0) * lr + ab_im * li) / den
    q_im = (ab_im * lr - (ab_re - 1.0) * li) / den
    br, bi = b_re.astype(F32), b_im.astype(F32)
    bb_re = q_re[..., None] * br - q_im[..., None] * bi
    bb_im = q_re[..., None] * bi + q_im[..., None] * br
    cr, ci = c_re.astype(F32), c_im.astype(F32)
    dsk = d_skip.astype(F32).reshape(g, c)

    def power(n):
        m_ = jnp.exp(n * ldt_re)
        return m_ * jnp.cos(n * ldt_im), m_ * jnp.sin(n * ldt_im)

    lag = jnp.arange(0, L + 1, dtype=F32)[:, None, None]
    pw_re, pw_im = power(lag)
    ca_re = cr[None] * pw_re[:, :, None, :] - ci[None] * pw_im[:, :, None, :]
    ca_im = cr[None] * pw_im[:, :, None, :] + ci[None] * pw_re[:, :, None, :]
    hi = lax.Precision.HIGHEST
    taps = (jnp.einsum("lgdp,gpc->lgdc", ca_re[:L], bb_re, precision=hi)
            - jnp.einsum("lgdp,gpc->lgdc", ca_im[:L], bb_im, precision=hi))
    taps = taps.at[0].add(dsk[:, :, None] * jnp.eye(c, dtype=F32)[None])
    idx = jnp.arange(L)
    diff = idx[None, :] - idx[:, None]
    toep = jnp.where((diff >= 0)[:, :, None, None, None], taps[jnp.clip(diff, 0, L - 1)], 0.0)
    tt = toep.transpose(2, 0, 4, 1, 3).reshape(g, L * c, L * c)
    vt = jnp.concatenate([ca_re[1:].transpose(1, 3, 0, 2), -ca_im[1:].transpose(1, 3, 0, 2)],
                         axis=1).reshape(g, 2 * p, L * c)
    rev_re, rev_im = pw_re[L - 1 - idx], pw_im[L - 1 - idx]
    w_re = (rev_re[:, :, None, :] * bb_re.transpose(0, 2, 1)[None]
            - rev_im[:, :, None, :] * bb_im.transpose(0, 2, 1)[None])
    w_im = (rev_re[:, :, None, :] * bb_im.transpose(0, 2, 1)[None]
            + rev_im[:, :, None, :] * bb_re.transpose(0, 2, 1)[None])
    wt = jnp.concatenate([w_re, w_im], axis=-1).transpose(1, 0, 2, 3).reshape(g, L * c, 2 * p)
    n_steps = max(1, (n_chunks - 1).bit_length())
    hop = (L * 2.0 ** jnp.arange(n_steps, dtype=F32))[:, None, None]
    hop_re, hop_im = power(hop)
    c1 = jnp.concatenate([hop_re, hop_re], axis=-1).transpose(1, 0, 2)
    c2 = jnp.concatenate([-hop_im, hop_im], axis=-1).transpose(1, 0, 2)
    return wt.astype(BF16), tt.astype(BF16), vt.astype(BF16), c1, c2


def _s5_core(h, seq, lam_re, lam_im, log_step, b_re, b_im, c_re, c_im, d_skip):
    m, d = h.shape
    g, p = lam_re.shape
    c, L = S5_GROUP, CHUNK
    n_chunks = seq // L
    rows = m // L
    wt, tt, vt, c1, c2 = _s5_operators(lam_re, lam_im, log_step, b_re, b_im, c_re, c_im,
                                       d_skip, n_chunks)
    n_steps = c1.shape[1]
    u = h.reshape(rows, L, g, c).transpose(2, 0, 1, 3).reshape(g, rows, L * c)
    grp = lambda *shape: pl.BlockSpec((None,) + shape, lambda gi: (gi, 0, 0))
    y = pl.pallas_call(
        functools.partial(_s5_body, n_chunks=n_chunks, state=p),
        grid=(g,),
        in_specs=[grp(rows, L * c), grp(L * c, 2 * p), grp(L * c, L * c), grp(2 * p, L * c),
                  grp(n_steps, 2 * p), grp(n_steps, 2 * p)],
        out_specs=grp(rows, L * c),
        out_shape=jax.ShapeDtypeStruct((g, rows, L * c), BF16),
        compiler_params=_params("parallel"),
        name="s5_chunk_scan",
    )(u, wt, tt, vt, c1, c2)
    return y.reshape(g, rows, L, c).transpose(1, 2, 0, 3).reshape(m, d)


def _ffn(x, norm_g, w_in, w_out):
    h = _rmsnorm(x, norm_g, BF16)
    act = _dual_matmul(h, w_in.astype(BF16), "swiglu")
    return _resid_matmul(act, w_out.astype(BF16), x, 0.5)


def kernel(x, ffn_norm, ffn_w_in, ffn_w_out, mix_norm, rwkv_mu, rwkv_w_rkv, rwkv_w_o, rwkv_w0, rwkv_w_l1, rwkv_w_l2, rwkv_a0, rwkv_a_l1, rwkv_a_l2, rwkv_v0, rwkv_v_l1, rwkv_v_l2, rwkv_g_l1, rwkv_g_l2, rwkv_k_k, rwkv_k_a, rwkv_r_k, rwkv_ln_w, rwkv_ln_b, s5_lam_re, s5_lam_im, s5_log_step, s5_b_re, s5_b_im, s5_c_re, s5_c_im, s5_d, s5_w_glu, final_norm):
    bsz, seq, d = x.shape
    depth = ffn_norm.shape[0]
    n_mixers = 2
    x = x.reshape(bsz * seq, d)
    v_first = None
    for i in range(depth):
        x = _ffn(x, ffn_norm[i, 0], ffn_w_in[i, 0], ffn_w_out[i, 0])
        j = i // n_mixers
        if i % n_mixers == 0:
            v_res = None if j == 0 else (rwkv_v0[j - 1], rwkv_v_l1[j - 1], rwkv_v_l2[j - 1])
            x, v_first = _rwkv_layer(
                x, seq, v_first, mix_norm[i], rwkv_mu[j], rwkv_w_rkv[j], rwkv_w_o[j],
                rwkv_w0[j], rwkv_w_l1[j], rwkv_w_l2[j], rwkv_a0[j], rwkv_a_l1[j], rwkv_a_l2[j],
                rwkv_g_l1[j], rwkv_g_l2[j], rwkv_k_k[j], rwkv_k_a[j], rwkv_r_k[j],
                rwkv_ln_w[j], rwkv_ln_b[j], v_res)
        else:
            h = _rmsnorm(x, mix_norm[i], BF16)
            y = _s5_core(h, seq, s5_lam_re[j], s5_lam_im[j], s5_log_step[j], s5_b_re[j],
                         s5_b_im[j], s5_c_re[j], s5_c_im[j], s5_d[j])
            x = _dual_matmul(y, s5_w_glu[j].astype(BF16), "glu_residual", res=x)
        x = _ffn(x, ffn_norm[i, 1], ffn_w_in[i, 1], ffn_w_out[i, 1])
    return _rmsnorm(x, final_norm, F32).reshape(bsz, seq, d)
```

```python
import functools
import math

import jax
import jax.numpy as jnp
from jax import lax
from jax.experimental import pallas as pl
from jax.experimental.pallas import tpu as pltpu

F32 = jnp.float32
BF16 = jnp.bfloat16

RMS_EPS = 1e-6
GN_EPS = 64e-5
LAMBDA_RE_MAX = -1e-4
HEAD = 64
CHUNK = 64
S5_GROUP = 16
LANES = 128
VMEM_LIMIT = 56 * 1024 * 1024

_NN = (((1,), (0,)), ((), ()))
_NT = (((1,), (1,)), ((), ()))
_TN = (((0,), (0,)), ((), ()))


def _params(*sem):
    return pltpu.CompilerParams(dimension_semantics=sem, vmem_limit_bytes=VMEM_LIMIT)


def _tile(n, pref, quantum):
    best = None
    t = quantum
    while t <= min(n, pref):
        if n % t == 0:
            best = t
        t += quantum
    return best if best is not None else n


def _rms_body(x_ref, g_ref, o_ref):
    x = x_ref[...]
    ms = jnp.mean(x * x, axis=-1, keepdims=True)
    o_ref[...] = (x * lax.rsqrt(ms + RMS_EPS) * g_ref[...]).astype(o_ref.dtype)


def _rmsnorm(x, g, out_dtype):
    m, d = x.shape
    bm = _tile(m, 256, 8)
    return pl.pallas_call(
        _rms_body,
        grid=(m // bm,),
        in_specs=[pl.BlockSpec((bm, d), lambda i: (i, 0)),
                  pl.BlockSpec((1, d), lambda i: (0, 0))],
        out_specs=pl.BlockSpec((bm, d), lambda i: (i, 0)),
        out_shape=jax.ShapeDtypeStruct((m, d), out_dtype),
        compiler_params=_params("parallel"),
        name="rmsnorm",
    )(x, g.reshape(1, d))


def _rms_mix_body(x_ref, g_ref, mu_ref, o_ref, carry_ref, *, tiles_per_seq):
    i = pl.program_id(0)

    @pl.when(i % tiles_per_seq == 0)
    def _():
        carry_ref[...] = jnp.zeros_like(carry_ref)

    x = x_ref[...]
    ms = jnp.mean(x * x, axis=-1, keepdims=True)
    h = x * lax.rsqrt(ms + RMS_EPS) * g_ref[...]
    bm = h.shape[0]
    row = lax.broadcasted_iota(jnp.int32, h.shape, 0)
    prev = jnp.where(row == 0, carry_ref[...], pltpu.roll(h, 1, axis=0))
    carry_ref[...] = h[bm - 1:bm, :]
    dx = prev - h
    for c in range(o_ref.shape[0]):
        o_ref[c] = (h + dx * mu_ref[c:c + 1, :]).astype(o_ref.dtype)


def _rms_mix(x, g, mu, seq):
    m, d = x.shape
    n_mix = mu.shape[0]
    bm = _tile(seq, 256, 8)
    return pl.pallas_call(
        functools.partial(_rms_mix_body, tiles_per_seq=seq // bm),
        grid=(m // bm,),
        in_specs=[pl.BlockSpec((bm, d), lambda i: (i, 0)),
                  pl.BlockSpec((1, d), lambda i: (0, 0)),
                  pl.BlockSpec((n_mix, d), lambda i: (0, 0))],
        out_specs=pl.BlockSpec((n_mix, bm, d), lambda i: (0, i, 0)),
        out_shape=jax.ShapeDtypeStruct((n_mix, m, d), BF16),
        scratch_shapes=[pltpu.VMEM((1, d), F32)],
        compiler_params=_params("arbitrary"),
        name="rms_token_shift_mix",
    )(x, g.reshape(1, d), mu)


def _dual_body(a_ref, w1_ref, w2_ref, *rest, mode):
    a = a_ref[...]
    p1 = jnp.dot(a, w1_ref[...], preferred_element_type=F32)
    p2 = jnp.dot(a, w2_ref[...], preferred_element_type=F32)
    if mode == "swiglu":
        (o_ref,) = rest
        o_ref[...] = (jax.nn.silu(p1) * p2).astype(o_ref.dtype)
    else:
        res_ref, o_ref = rest
        o_ref[...] = res_ref[...] + p1 * jax.nn.sigmoid(p2)


def _dual_matmul(a, w, mode, res=None):
    m, k = a.shape
    n = w.shape[1] // 2
    bm = _tile(m, 1024, 8)
    bn = _tile(n, 512, LANES)
    nj = n // bn
    in_specs = [pl.BlockSpec((bm, k), lambda i, j: (i, 0)),
                pl.BlockSpec((k, bn), lambda i, j: (0, j)),
                pl.BlockSpec((k, bn), lambda i, j: (0, j + nj))]
    args = [a, w, w]
    if mode == "glu_residual":
        in_specs.append(pl.BlockSpec((bm, bn), lambda i, j: (i, j)))
        args.append(res)
        out_dtype = F32
    else:
        out_dtype = BF16
    return pl.pallas_call(
        functools.partial(_dual_body, mode=mode),
        grid=(m // bm, nj),
        in_specs=in_specs,
        out_specs=pl.BlockSpec((bm, bn), lambda i, j: (i, j)),
        out_shape=jax.ShapeDtypeStruct((m, n), out_dtype),
        compiler_params=_params("parallel", "parallel"),
        name="matmul_" + mode,
    )(*args)


def _resid_body(a_ref, w_ref, res_ref, o_ref, *, scale):
    acc = jnp.dot(a_ref[...], w_ref[...], preferred_element_type=F32)
    o_ref[...] = res_ref[...] + scale * acc


def _resid_matmul(a, w, res, scale):
    m, k = a.shape
    n = w.shape[1]
    bm = _tile(m, 1024, 8)
    bn = _tile(n, 512, LANES)
    return pl.pallas_call(
        functools.partial(_resid_body, scale=scale),
        grid=(m // bm, n // bn),
        in_specs=[pl.BlockSpec((bm, k), lambda i, j: (i, 0)),
                  pl.BlockSpec((k, bn), lambda i, j: (0, j)),
                  pl.BlockSpec((bm, bn), lambda i, j: (i, j))],
        out_specs=pl.BlockSpec((bm, bn), lambda i, j: (i, j)),
        out_shape=jax.ShapeDtypeStruct((m, n), F32),
        compiler_params=_params("parallel", "parallel"),
        name="matmul_residual",
    )(a, w, res)


def _plain_body(a_ref, w_ref, o_ref):
    o_ref[...] = jnp.dot(a_ref[...], w_ref[...], preferred_element_type=F32)


def _batched_matmul(a, w, n_batch):
    _, m, k = a.shape
    n = w.shape[2]
    bm = _tile(m, 1024, 8)
    bn = _tile(n, 1024, LANES)
    return pl.pallas_call(
        _plain_body,
        grid=(n_batch, m // bm, n // bn),
        in_specs=[pl.BlockSpec((None, bm, k), lambda c, i, j: (c, i, 0)),
                  pl.BlockSpec((None, k, bn), lambda c, i, j: (c, 0, j))],
        out_specs=pl.BlockSpec((None, bm, bn), lambda c, i, j: (c, i, j)),
        out_shape=jax.ShapeDtypeStruct((n_batch, m, n), F32),
        compiler_params=_params("parallel", "parallel", "parallel"),
        name="matmul_rkv",
    )(a, w)


_DECAY_SCALE = math.exp(-0.5)


def _lora_body(x_ref, l1_ref, l2_ref, *rest, mode):
    t = jnp.dot(x_ref[...], l1_ref[...], preferred_element_type=F32)
    if mode == "decay":
        t = jnp.tanh(t)
    elif mode == "gate":
        t = jax.nn.sigmoid(t)
    z = jnp.dot(t.astype(BF16), l2_ref[...], preferred_element_type=F32)
    if mode == "decay":
        b_ref, o_ref = rest
        o_ref[...] = -_DECAY_SCALE * jax.nn.sigmoid(z + b_ref[...])
    elif mode == "lr":
        b_ref, o_ref = rest
        o_ref[...] = jax.nn.sigmoid(z + b_ref[...])
    elif mode == "gate":
        (o_ref,) = rest
        o_ref[...] = z
    else:
        b_ref, v_ref, vf_ref, o_ref = rest
        v = v_ref[...]
        o_ref[...] = v + (vf_ref[...] - v) * jax.nn.sigmoid(z + b_ref[...])


def _lora(xmix, c, l1, l2, mode, bias=None, v=None, v_first=None):
    _, m, d = xmix.shape
    rank = l1.shape[1]
    pad = (-rank) % LANES
    if pad:
        l1 = jnp.pad(l1, ((0, 0), (0, pad)))
        l2 = jnp.pad(l2, ((0, pad), (0, 0)))
    rp = rank + pad
    bm = _tile(m, 256, 8)
    row = pl.BlockSpec((bm, d), lambda i: (i, 0))
    vec = pl.BlockSpec((1, d), lambda i: (0, 0))
    in_specs = [pl.BlockSpec((None, bm, d), lambda i: (c, i, 0)),
                pl.BlockSpec((d, rp), lambda i: (0, 0)),
                pl.BlockSpec((rp, d), lambda i: (0, 0))]
    args = [xmix, l1.astype(BF16), l2.astype(BF16)]
    if mode != "gate":
        in_specs.append(vec)
        args.append(bias.reshape(1, d))
    if mode == "value_residual":
        stacked_v = pl.BlockSpec((None, bm, d), lambda i: (2, i, 0))
        in_specs += [stacked_v, stacked_v]
        args += [v, v_first]
    return pl.pallas_call(
        functools.partial(_lora_body, mode=mode),
        grid=(m // bm,),
        in_specs=in_specs,
        out_specs=row,
        out_shape=jax.ShapeDtypeStruct((m, d), F32),
        compiler_params=_params("parallel"),
        name="lora_" + mode,
    )(*args)


RWKV_TIME_BLOCK = 512


def _bdot(a, b, dims=_NN):
    return lax.dot_general(a.astype(BF16), b.astype(BF16), dims, preferred_element_type=F32)


def _split_dot(a, b, split, passes):
    acc = None
    rem = (a, b)[split]
    for _ in range(passes):
        part = rem.astype(BF16)
        term = (lax.dot_general(part, b, _NN, preferred_element_type=F32) if split == 0 else
                lax.dot_general(a, part, _NN, preferred_element_type=F32))
        acc = term if acc is None else acc + term
        rem = rem - part.astype(F32)
    return acc


def _rwkv_body(r_ref, k_ref, v_ref, lw_ref, a_ref, g_ref,
               kk_ref, ka_ref, rk_ref, lnw_ref, lnb_ref, o_ref, s_ref, *, n_chunks):
    L, N, W = CHUNK, HEAD, LANES

    @pl.when(pl.program_id(2) == 0)
    def _():
        s_ref[...] = jnp.zeros_like(s_ref)

    def iota(shape, dim):
        return lax.broadcasted_iota(jnp.int32, shape, dim)

    tri = (iota((L, L), 1) <= iota((L, L), 0)).astype(BF16)
    same_head = (iota((W, W), 0) // N) == (iota((W, W), 1) // N)
    seg = same_head.astype(BF16)
    g_row, g_col = iota((2 * L, 2 * W), 0), iota((2 * L, 2 * W), 1)
    mask_g = (g_col % N) < (g_row % L) + (g_row >= L).astype(jnp.int32)
    n_double = max(1, (L - 1).bit_length())

    def stack(x):
        head0 = (iota(x.shape, 1) % W) < N
        zero = jnp.zeros_like(x)
        return jnp.concatenate([jnp.where(head0, x, zero), jnp.where(head0, zero, x)], axis=0)

    def seg_sum(x, passes):
        return _split_dot(x, seg, 0, passes)

    kk_p, ka_p, rk_p = kk_ref[...], ka_ref[...], rk_ref[...]
    lnw_p, lnb_p = lnw_ref[...], lnb_ref[...]
    state = {"ss": s_ref[...]}

    def chunk_stages(c):
        rows = slice(c * L, (c + 1) * L)
        r, k, v = r_ref[rows, :], k_ref[rows, :], v_ref[rows, :]
        lw, a = lw_ref[rows, :], a_ref[rows, :]
        kkr = k * kk_p
        kh = k * (1.0 + (a - 1.0) * ka_p)
        ssq = seg_sum(kkr * kkr, 2)
        lp = _split_dot(tri, lw, 1, 3)
        bonus_w = seg_sum(r * kh * rk_p, 1)
        yield
        kk = kkr / jnp.maximum(jnp.sqrt(ssq), 1e-12)
        ba = kk * a
        lp_end = lp[L - 1:L, :]
        e_neg = jnp.exp(-lp)
        e_end = jnp.exp(lp_end - lp)
        at = -kk * jnp.exp(lp - lw)
        rt = r * jnp.exp(lp)
        bh = (ba * e_end).astype(BF16)
        khh = (kh * e_end).astype(BF16)
        v16 = v.astype(BF16)
        ar = jnp.concatenate([at, rt], axis=0)
        bk = jnp.concatenate([stack((ba * e_neg).astype(BF16)),
                              stack((kh * e_neg).astype(BF16))], axis=0)
        gm = _bdot(ar, bk, _NT)
        yield
        gm = jnp.where(mask_g, gm, 0.0)
        nk, gak = gm[:L, :W], gm[:L, W:]
        grb, grk = gm[L:, :W], gm[L:, W:]
        sv = stack(v16)
        aakv = _bdot(gak, sv)
        yield
        w = jnp.concatenate([at, aakv], axis=1)
        for it in range(n_double):
            nk16, w16 = nk.astype(BF16), w.astype(BF16)
            if it + 1 < n_double:
                p = _bdot(nk16, jnp.concatenate([stack(nk16), stack(w16)], axis=1))
                yield
                nk, w = p[:, :W], w + p[:, W:]
            else:
                p = _bdot(nk16, stack(w16))
                yield
                w = w + p
        p2 = _bdot(grb, stack(w.astype(BF16)))
        p3 = _bdot(grk, sv)
        tn = _bdot(w.T, bh)
        vk = _bdot(v.T, khh)
        yield
        rp = rt + p2[:, :W]
        yv = p2[:, W:] + p3
        zero = jnp.zeros((W, W), F32)
        m_bd = jnp.where(same_head, tn[:W], zero)
        sv_bd = jnp.where(same_head, tn[W:] + vk, zero)
        ss = state["ss"]
        ss16 = ss.astype(BF16)
        ys = _bdot(rp, ss16, _NT)
        state["ss"] = ss * jnp.exp(lp_end) + _bdot(ss16, m_bd) + sv_bd
        yield
        y = yv + ys
        mean = seg_sum(y, 1) * (1.0 / N)
        yield
        yc = y - mean
        var = seg_sum(yc * yc, 1) * (1.0 / N)
        yield
        yn = yc * lax.rsqrt(var + GN_EPS) * lnw_p + lnb_p
        o_ref[rows, :] = ((yn + bonus_w * v) * g_ref[rows, :]).astype(o_ref.dtype)

    pending = {c: chunk_stages(c) for c in range(n_chunks)}
    slot = 0
    while pending:
        for c in sorted(pending):
            if c <= slot and next(pending[c], True):
                del pending[c]
        slot += 1
    s_ref[...] = state["ss"]


def _rwkv_recurrence(rkv, v_mixed, lw, a, g, k_k, k_a, r_k, ln_w, ln_b):
    _, b, t, d = rkv.shape
    tb = _tile(t, RWKV_TIME_BLOCK, CHUNK)
    tok = pl.BlockSpec((None, tb, LANES), lambda bi, hi, ti: (bi, ti, hi))
    par = pl.BlockSpec((1, LANES), lambda bi, hi, ti: (0, hi))

    def stacked(c):
        return pl.BlockSpec((None, None, tb, LANES), lambda bi, hi, ti: (c, bi, ti, hi))

    v_spec, v_arg = (stacked(2), rkv) if v_mixed is None else (tok, v_mixed)
    return pl.pallas_call(
        functools.partial(_rwkv_body, n_chunks=tb // CHUNK),
        grid=(b, d // LANES, t // tb),
        in_specs=[stacked(0), stacked(1), v_spec, tok, tok, tok] + [par] * 5,
        out_specs=tok,
        out_shape=jax.ShapeDtypeStruct((b, t, d), BF16),
        scratch_shapes=[pltpu.VMEM((LANES, LANES), F32)],
        compiler_params=_params("parallel", "parallel", "arbitrary"),
        name="rwkv7_recurrence",
    )(rkv, rkv, v_arg, lw, a, g, *(p.reshape(1, d) for p in (k_k, k_a, r_k, ln_w, ln_b)))


def _rwkv_layer(x, seq, v_first, norm_g, mu, w_rkv, w_o, w0, w_l1, w_l2, a0, a_l1, a_l2,
                g_l1, g_l2, k_k, k_a, r_k, ln_w, ln_b, v_res):
    m, d = x.shape
    bsz = m // seq
    xmix = _rms_mix(x, norm_g, mu, seq)
    rkv = _batched_matmul(xmix, w_rkv.astype(BF16), 3)
    lw = _lora(xmix, 3, w_l1, w_l2, "decay", bias=w0)
    a = _lora(xmix, 4, a_l1, a_l2, "lr", bias=a0)
    g = _lora(xmix, 5, g_l1, g_l2, "gate")
    sh = (bsz, seq, d)
    if v_res is None:
        v_first = rkv
        v_mixed = None
    else:
        v0, v_l1, v_l2 = v_res
        v_mixed = _lora(xmix, 2, v_l1, v_l2, "value_residual", bias=v0, v=rkv,
                        v_first=v_first).reshape(sh)
    yg = _rwkv_recurrence(rkv.reshape((3,) + sh), v_mixed, lw.reshape(sh), a.reshape(sh),
                          g.reshape(sh), k_k, k_a, r_k, ln_w, ln_b)
    return _resid_matmul(yg.reshape(m, d), w_o.astype(BF16), x, 1.0), v_first


def _s5_body(u_ref, wt_ref, bbt_ref, ca_ref, dsk_ref, vt_ref, c1_ref, c2_ref, o_ref, tt_ref,
             *, n_chunks, state):
    L, C = CHUNK, S5_GROUP
    LC = L * C
    taps = lax.dot_general(bbt_ref[...], ca_ref[...], _NN, precision=lax.Precision.HIGHEST,
                           preferred_element_type=F32)
    on_diag = (lax.broadcasted_iota(jnp.int32, taps.shape, 1)
               == lax.broadcasted_iota(jnp.int32, taps.shape, 0))
    taps = taps + jnp.where(on_diag, dsk_ref[...], 0.0)
    padded = jnp.concatenate([jnp.zeros_like(taps), taps], axis=1)
    lags_per_tile = LANES // C
    for sub in range(lags_per_tile):
        shifted = padded if sub == 0 else pltpu.roll(padded, sub * C, axis=1)
        shifted = shifted.astype(BF16)
        for q in range(L // lags_per_tile):
            lag = q * lags_per_tile + sub
            tt_ref[lag * C:(lag + 1) * C, :] = shifted[:, LC - q * LANES:2 * LC - q * LANES]
    u = u_ref[...]
    s = jnp.dot(u, wt_ref[...], preferred_element_type=F32)
    chunk_idx = lax.broadcasted_iota(jnp.int32, s.shape, 0) % n_chunks
    step = 1
    it = 0
    while step < n_chunks:
        sp = jnp.where(chunk_idx >= step, pltpu.roll(s, step, axis=0), 0.0)
        s = s + c1_ref[it:it + 1, :] * sp + c2_ref[it:it + 1, :] * pltpu.roll(sp, state, axis=1)
        step *= 2
        it += 1
    s_start = jnp.where(chunk_idx >= 1, pltpu.roll(s, 1, axis=0), 0.0)
    y = (jnp.dot(u, tt_ref[...], preferred_element_type=F32)
         + jnp.dot(s_start.astype(BF16), vt_ref[...], preferred_element_type=F32))
    o_ref[...] = jax.nn.gelu(y).astype(o_ref.dtype)


def _s5_operators(lam_re, lam_im, log_step, b_re, b_im, c_re, c_im, d_skip, n_chunks):
    g, p = lam_re.shape
    c = S5_GROUP
    L = CHUNK
    lr = jnp.minimum(lam_re.astype(F32), LAMBDA_RE_MAX)
    li = lam_im.astype(F32)
    dt = jnp.exp(log_step.astype(F32))[:, None]
    ldt_re, ldt_im = lr * dt, li * dt
    mag = jnp.exp(ldt_re)
    ab_re, ab_im = mag * jnp.cos(ldt_im), mag * jnp.sin(ldt_im)
    den = lr * lr + li * li
    q_re = ((ab_re - 1.0) * lr + ab_im * li) / den
    q_im = (ab_im * lr - (ab_re - 1.0) * li) / den
    br, bi = b_re.astype(F32), b_im.astype(F32)
    bb_re = q_re[..., None] * br - q_im[..., None] * bi
    bb_im = q_re[..., None] * bi + q_im[..., None] * br
    cr, ci = c_re.astype(F32), c_im.astype(F32)
    dsk = d_skip.astype(F32).reshape(g, c)

    def power(n):
        m_ = jnp.exp(n * ldt_re)
        return m_ * jnp.cos(n * ldt_im), m_ * jnp.sin(n * ldt_im)

    lag = jnp.arange(0, L + 1, dtype=F32)[:, None, None]
    pw_re, pw_im = power(lag)
    pwt_re, pwt_im = pw_re.transpose(1, 2, 0), pw_im.transpose(1, 2, 0)
    crt, cit = cr.transpose(0, 2, 1), ci.transpose(0, 2, 1)
    ca = jnp.concatenate([
        crt[:, :, None, :] * pwt_re[..., None] - cit[:, :, None, :] * pwt_im[..., None],
        -(crt[:, :, None, :] * pwt_im[..., None] + cit[:, :, None, :] * pwt_re[..., None])],
        axis=1)
    ca_lo = ca[:, :, :L].reshape(g, 2 * p, L * c)
    vt = ca[:, :, 1:].reshape(g, 2 * p, L * c).astype(BF16)
    bbt_re, bbt_im = bb_re.transpose(0, 2, 1), bb_im.transpose(0, 2, 1)
    bbt = jnp.concatenate([bbt_re, bbt_im], axis=-1)
    idx = jnp.arange(L)
    rev_re = pw_re[L - 1 - idx].transpose(1, 0, 2)[:, :, None, :]
    rev_im = pw_im[L - 1 - idx].transpose(1, 0, 2)[:, :, None, :]
    wt = jnp.concatenate([rev_re * bbt_re[:, None] - rev_im * bbt_im[:, None],
                          rev_re * bbt_im[:, None] + rev_im * bbt_re[:, None]],
                         axis=-1).reshape(g, L * c, 2 * p).astype(BF16)
    n_steps = max(1, (n_chunks - 1).bit_length())
    hop = (L * 2.0 ** jnp.arange(n_steps, dtype=F32))[:, None, None]
    hop_re, hop_im = power(hop)
    c1 = jnp.concatenate([hop_re, hop_re], axis=-1).transpose(1, 0, 2)
    c2 = jnp.concatenate([-hop_im, hop_im], axis=-1).transpose(1, 0, 2)
    return wt, bbt, ca_lo, dsk[:, :, None], vt, c1, c2


def _s5_core(h, seq, lam_re, lam_im, log_step, b_re, b_im, c_re, c_im, d_skip):
    m, d = h.shape
    g, p = lam_re.shape
    c, L = S5_GROUP, CHUNK
    n_chunks = seq // L
    rows = m // L
    wt, bbt, ca_lo, dsk, vt, c1, c2 = _s5_operators(lam_re, lam_im, log_step, b_re, b_im, c_re,
                                                    c_im, d_skip, n_chunks)
    n_steps = c1.shape[1]
    u = h.reshape(rows, L, g, c).transpose(2, 0, 1, 3).reshape(g, rows, L * c)
    grp = lambda *shape: pl.BlockSpec((None,) + shape, lambda gi: (gi, 0, 0))
    y = pl.pallas_call(
        functools.partial(_s5_body, n_chunks=n_chunks, state=p),
        grid=(g,),
        in_specs=[grp(rows, L * c), grp(L * c, 2 * p), grp(c, 2 * p), grp(2 * p, L * c),
                  grp(c, 1), grp(2 * p, L * c), grp(n_steps, 2 * p), grp(n_steps, 2 * p)],
        out_specs=grp(rows, L * c),
        out_shape=jax.ShapeDtypeStruct((g, rows, L * c), BF16),
        scratch_shapes=[pltpu.VMEM((L * c, L * c), BF16)],
        compiler_params=_params("parallel"),
        name="s5_chunk_scan",
    )(u, wt, bbt, ca_lo, dsk, vt, c1, c2)
    return y.reshape(g, rows, L, c).transpose(1, 2, 0, 3).reshape(m, d)


def _ffn(x, norm_g, w_in, w_out):
    h = _rmsnorm(x, norm_g, BF16)
    act = _dual_matmul(h, w_in.astype(BF16), "swiglu")
    return _resid_matmul(act, w_out.astype(BF16), x, 0.5)


def kernel(x, ffn_norm, ffn_w_in, ffn_w_out, mix_norm, rwkv_mu, rwkv_w_rkv, rwkv_w_o, rwkv_w0, rwkv_w_l1, rwkv_w_l2, rwkv_a0, rwkv_a_l1, rwkv_a_l2, rwkv_v0, rwkv_v_l1, rwkv_v_l2, rwkv_g_l1, rwkv_g_l2, rwkv_k_k, rwkv_k_a, rwkv_r_k, rwkv_ln_w, rwkv_ln_b, s5_lam_re, s5_lam_im, s5_log_step, s5_b_re, s5_b_im, s5_c_re, s5_c_im, s5_d, s5_w_glu, final_norm):
    bsz, seq, d = x.shape
    depth = ffn_norm.shape[0]
    n_mixers = 2
    x = x.reshape(bsz * seq, d)
    v_first = None
    for i in range(depth):
        x = _ffn(x, ffn_norm[i, 0], ffn_w_in[i, 0], ffn_w_out[i, 0])
        j = i // n_mixers
        if i % n_mixers == 0:
            v_res = None if j == 0 else (rwkv_v0[j - 1], rwkv_v_l1[j - 1], rwkv_v_l2[j - 1])
            x, v_first = _rwkv_layer(
                x, seq, v_first, mix_norm[i], rwkv_mu[j], rwkv_w_rkv[j], rwkv_w_o[j],
                rwkv_w0[j], rwkv_w_l1[j], rwkv_w_l2[j], rwkv_a0[j], rwkv_a_l1[j], rwkv_a_l2[j],
                rwkv_g_l1[j], rwkv_g_l2[j], rwkv_k_k[j], rwkv_k_a[j], rwkv_r_k[j],
                rwkv_ln_w[j], rwkv_ln_b[j], v_res)
        else:
            h = _rmsnorm(x, mix_norm[i], BF16)
            y = _s5_core(h, seq, s5_lam_re[j], s5_lam_im[j], s5_log_step[j], s5_b_re[j],
                         s5_b_im[j], s5_c_re[j], s5_c_im[j], s5_d[j])
            x = _dual_matmul(y, s5_w_glu[j].astype(BF16), "glu_residual", res=x)
        x = _ffn(x, ffn_norm[i, 1], ffn_w_in[i, 1], ffn_w_out[i, 1])
    return _rmsnorm(x, final_norm, F32).reshape(bsz, seq, d)
```

```python
import functools
import math

import jax
import jax.numpy as jnp
from jax import lax
from jax.experimental import pallas as pl
from jax.experimental.pallas import tpu as pltpu

F32 = jnp.float32
BF16 = jnp.bfloat16

RMS_EPS = 1e-6
GN_EPS = 64e-5
LAMBDA_RE_MAX = -1e-4
HEAD = 64
CHUNK = 64
S5_GROUP = 16
LANES = 128
VMEM_LIMIT = 56 * 1024 * 1024

_NN = (((1,), (0,)), ((), ()))
_NT = (((1,), (1,)), ((), ()))
_TN = (((0,), (0,)), ((), ()))


def _params(*sem):
    return pltpu.CompilerParams(dimension_semantics=sem, vmem_limit_bytes=VMEM_LIMIT)


def _tile(n, pref, quantum):
    best = None
    t = quantum
    while t <= min(n, pref):
        if n % t == 0:
            best = t
        t += quantum
    return best if best is not None else n


def _rms_body(x_ref, g_ref, o_ref):
    x = x_ref[...]
    ms = jnp.mean(x * x, axis=-1, keepdims=True)
    o_ref[...] = (x * lax.rsqrt(ms + RMS_EPS) * g_ref[...]).astype(o_ref.dtype)


def _rmsnorm(x, g, out_dtype):
    m, d = x.shape
    bm = _tile(m, 256, 8)
    return pl.pallas_call(
        _rms_body,
        grid=(m // bm,),
        in_specs=[pl.BlockSpec((bm, d), lambda i: (i, 0)),
                  pl.BlockSpec((1, d), lambda i: (0, 0))],
        out_specs=pl.BlockSpec((bm, d), lambda i: (i, 0)),
        out_shape=jax.ShapeDtypeStruct((m, d), out_dtype),
        compiler_params=_params("parallel"),
        name="rmsnorm",
    )(x, g.reshape(1, d))


def _rms_mix_body(x_ref, g_ref, mu_ref, o_ref, carry_ref, *, tiles_per_seq):
    i = pl.program_id(0)

    @pl.when(i % tiles_per_seq == 0)
    def _():
        carry_ref[...] = jnp.zeros_like(carry_ref)

    x = x_ref[...]
    ms = jnp.mean(x * x, axis=-1, keepdims=True)
    h = x * lax.rsqrt(ms + RMS_EPS) * g_ref[...]
    bm = h.shape[0]
    row = lax.broadcasted_iota(jnp.int32, h.shape, 0)
    prev = jnp.where(row == 0, carry_ref[...], pltpu.roll(h, 1, axis=0))
    carry_ref[...] = h[bm - 1:bm, :]
    dx = prev - h
    for c in range(o_ref.shape[0]):
        o_ref[c] = (h + dx * mu_ref[c:c + 1, :]).astype(o_ref.dtype)


def _rms_mix(x, g, mu, seq):
    m, d = x.shape
    n_mix = mu.shape[0]
    bm = _tile(seq, 256, 8)
    return pl.pallas_call(
        functools.partial(_rms_mix_body, tiles_per_seq=seq // bm),
        grid=(m // bm,),
        in_specs=[pl.BlockSpec((bm, d), lambda i: (i, 0)),
                  pl.BlockSpec((1, d), lambda i: (0, 0)),
                  pl.BlockSpec((n_mix, d), lambda i: (0, 0))],
        out_specs=pl.BlockSpec((n_mix, bm, d), lambda i: (0, i, 0)),
        out_shape=jax.ShapeDtypeStruct((n_mix, m, d), BF16),
        scratch_shapes=[pltpu.VMEM((1, d), F32)],
        compiler_params=_params("arbitrary"),
        name="rms_token_shift_mix",
    )(x, g.reshape(1, d), mu)


def _w_spec(widx, k, bn, col):
    return pl.BlockSpec((None,) * len(widx) + (k, bn), lambda *gi: widx + (0, col(*gi)))


def _emit_next_norm_input(x_new, g_ref, xg_ref, ssq_ref):
    xg_ref[...] = (x_new * g_ref[...]).astype(xg_ref.dtype)
    ssq_ref[...] = jnp.sum(x_new * x_new, axis=-1, keepdims=True)


def _next_norm_specs(m, n, bm, bn):
    g_spec = pl.BlockSpec((1, bn), lambda i, j: (0, j))
    out_specs = [pl.BlockSpec((bm, bn), lambda i, j: (i, j)),
                 pl.BlockSpec((None, bm, 1), lambda i, j: (j, i, 0))]
    out_shapes = [jax.ShapeDtypeStruct((m, n), BF16),
                  jax.ShapeDtypeStruct((n // bn, m, 1), F32)]
    return g_spec, out_specs, out_shapes


def _dual_body(*refs, mode, normed, emit_next, k_dim):
    refs = list(refs)
    a_ref, w1_ref, w2_ref = refs[:3]
    del refs[:3]
    a = a_ref[...]
    p1 = jnp.dot(a, w1_ref[...], preferred_element_type=F32)
    p2 = jnp.dot(a, w2_ref[...], preferred_element_type=F32)
    if not normed:
        ssq_ref = refs.pop(0)
        rstd = lax.rsqrt(jnp.sum(ssq_ref[...], axis=0) * (1.0 / k_dim) + RMS_EPS)
        p1, p2 = p1 * rstd, p2 * rstd
    if mode == "swiglu":
        (o_ref,) = refs
        o_ref[...] = (jax.nn.silu(p1) * p2).astype(o_ref.dtype)
        return
    res_ref = refs.pop(0)
    x_new = res_ref[...] + p1 * jax.nn.sigmoid(p2)
    if emit_next:
        g_ref, o_ref, xg_ref, ssq_out_ref = refs
        _emit_next_norm_input(x_new, g_ref, xg_ref, ssq_out_ref)
    else:
        (o_ref,) = refs
    o_ref[...] = x_new


def _dual_matmul(a, w, widx, mode, ssq=None, res=None, next_g=None):
    m, k = a.shape
    n = w.shape[-1] // 2
    bm = _tile(m, 1024, 8)
    bn = _tile(n, 512, LANES)
    nj = n // bn
    in_specs = [pl.BlockSpec((bm, k), lambda i, j: (i, 0)),
                _w_spec(widx, k, bn, lambda i, j: j),
                _w_spec(widx, k, bn, lambda i, j: j + nj)]
    args = [a, w, w]
    if ssq is not None:
        in_specs.append(pl.BlockSpec((ssq.shape[0], bm, 1), lambda i, j: (0, i, 0)))
        args.append(ssq)
    tile = pl.BlockSpec((bm, bn), lambda i, j: (i, j))
    out_specs, out_shapes = [tile], [jax.ShapeDtypeStruct((m, n), BF16 if mode == "swiglu" else F32)]
    if mode == "glu_residual":
        in_specs.append(tile)
        args.append(res)
        if next_g is not None:
            g_spec, extra_specs, extra_shapes = _next_norm_specs(m, n, bm, bn)
            in_specs.append(g_spec)
            args.append(next_g.reshape(1, n))
            out_specs += extra_specs
            out_shapes += extra_shapes
    out = pl.pallas_call(
        functools.partial(_dual_body, mode=mode, normed=ssq is None,
                          emit_next=next_g is not None, k_dim=k),
        grid=(m // bm, nj),
        in_specs=in_specs,
        out_specs=out_specs,
        out_shape=out_shapes,
        compiler_params=_params("parallel", "parallel"),
        name="matmul_" + mode,
    )(*args)
    return out[0] if len(out) == 1 else tuple(out)


def _resid_body(a_ref, w_ref, res_ref, *rest, scale, emit_next):
    acc = jnp.dot(a_ref[...], w_ref[...], preferred_element_type=F32)
    x_new = res_ref[...] + scale * acc
    if emit_next:
        g_ref, o_ref, xg_ref, ssq_ref = rest
        _emit_next_norm_input(x_new, g_ref, xg_ref, ssq_ref)
    else:
        (o_ref,) = rest
    o_ref[...] = x_new


def _resid_matmul(a, w, widx, res, scale, next_g=None):
    m, k = a.shape
    n = w.shape[-1]
    bm = _tile(m, 1024, 8)
    bn = _tile(n, 512, LANES)
    tile = pl.BlockSpec((bm, bn), lambda i, j: (i, j))
    in_specs = [pl.BlockSpec((bm, k), lambda i, j: (i, 0)),
                _w_spec(widx, k, bn, lambda i, j: j), tile]
    args = [a, w, res]
    out_specs, out_shapes = [tile], [jax.ShapeDtypeStruct((m, n), F32)]
    if next_g is not None:
        g_spec, extra_specs, extra_shapes = _next_norm_specs(m, n, bm, bn)
        in_specs.append(g_spec)
        args.append(next_g.reshape(1, n))
        out_specs += extra_specs
        out_shapes += extra_shapes
    out = pl.pallas_call(
        functools.partial(_resid_body, scale=scale, emit_next=next_g is not None),
        grid=(m // bm, n // bn),
        in_specs=in_specs,
        out_specs=out_specs,
        out_shape=out_shapes,
        compiler_params=_params("parallel", "parallel"),
        name="matmul_residual",
    )(*args)
    return out[0] if len(out) == 1 else tuple(out)


def _plain_body(a_ref, w_ref, o_ref):
    o_ref[...] = jnp.dot(a_ref[...], w_ref[...], preferred_element_type=F32)


def _batched_matmul(a, w, widx, n_batch):
    _, m, k = a.shape
    n = w.shape[-1]
    bm = _tile(m, 1024, 8)
    bn = _tile(n, 1024, LANES)
    return pl.pallas_call(
        _plain_body,
        grid=(n_batch, m // bm, n // bn),
        in_specs=[pl.BlockSpec((None, bm, k), lambda c, i, j: (c, i, 0)),
                  pl.BlockSpec((None,) * (len(widx) + 1) + (k, bn),
                               lambda c, i, j: widx + (c, 0, j))],
        out_specs=pl.BlockSpec((None, bm, bn), lambda c, i, j: (c, i, j)),
        out_shape=jax.ShapeDtypeStruct((n_batch, m, n), F32),
        compiler_params=_params("parallel", "parallel", "parallel"),
        name="matmul_rkv",
    )(a, w)


_DECAY_SCALE = math.exp(-0.5)


def _lora_body(x_ref, l1_ref, l2_ref, *rest, mode):
    t = jnp.dot(x_ref[...], l1_ref[...], preferred_element_type=F32)
    if mode == "decay":
        t = jnp.tanh(t)
    elif mode == "gate":
        t = jax.nn.sigmoid(t)
    z = jnp.dot(t.astype(BF16), l2_ref[...], preferred_element_type=F32)
    if mode == "decay":
        b_ref, o_ref = rest
        o_ref[...] = -_DECAY_SCALE * jax.nn.sigmoid(z + b_ref[...])
    elif mode == "lr":
        b_ref, o_ref = rest
        o_ref[...] = jax.nn.sigmoid(z + b_ref[...])
    elif mode == "gate":
        (o_ref,) = rest
        o_ref[...] = z
    else:
        b_ref, v_ref, vf_ref, o_ref = rest
        v = v_ref[...]
        o_ref[...] = v + (vf_ref[...] - v) * jax.nn.sigmoid(z + b_ref[...])


def _lora(xmix, c, l1, l2, mode, bias=None, v=None, v_first=None):
    _, m, d = xmix.shape
    rank = l1.shape[1]
    pad = (-rank) % LANES
    if pad:
        l1 = jnp.pad(l1, ((0, 0), (0, pad)))
        l2 = jnp.pad(l2, ((0, pad), (0, 0)))
    rp = rank + pad
    bm = _tile(m, 256, 8)
    row = pl.BlockSpec((bm, d), lambda i: (i, 0))
    vec = pl.BlockSpec((1, d), lambda i: (0, 0))
    in_specs = [pl.BlockSpec((None, bm, d), lambda i: (c, i, 0)),
                pl.BlockSpec((d, rp), lambda i: (0, 0)),
                pl.BlockSpec((rp, d), lambda i: (0, 0))]
    args = [xmix, l1.astype(BF16), l2.astype(BF16)]
    if mode != "gate":
        in_specs.append(vec)
        args.append(bias.reshape(1, d))
    if mode == "value_residual":
        stacked_v = pl.BlockSpec((None, bm, d), lambda i: (2, i, 0))
        in_specs += [stacked_v, stacked_v]
        args += [v, v_first]
    return pl.pallas_call(
        functools.partial(_lora_body, mode=mode),
        grid=(m // bm,),
        in_specs=in_specs,
        out_specs=row,
        out_shape=jax.ShapeDtypeStruct((m, d), F32),
        compiler_params=_params("parallel"),
        name="lora_" + mode,
    )(*args)


RWKV_TIME_BLOCK = 1024


def _bdot(a, b, dims=_NN):
    return lax.dot_general(a.astype(BF16), b.astype(BF16), dims, preferred_element_type=F32)


def _split_dot(a, b, split, passes):
    acc = None
    rem = (a, b)[split]
    for _ in range(passes):
        part = rem.astype(BF16)
        term = (lax.dot_general(part, b, _NN, preferred_element_type=F32) if split == 0 else
                lax.dot_general(a, part, _NN, preferred_element_type=F32))
        acc = term if acc is None else acc + term
        rem = rem - part.astype(F32)
    return acc


def _rwkv_body(r_ref, k_ref, v_ref, lw_ref, a_ref, g_ref,
               kk_ref, ka_ref, rk_ref, lnw_ref, lnb_ref, o_ref, s_ref, *, n_chunks):
    L, N, W = CHUNK, HEAD, LANES

    @pl.when(pl.program_id(2) == 0)
    def _():
        s_ref[...] = jnp.zeros_like(s_ref)

    def iota(shape, dim):
        return lax.broadcasted_iota(jnp.int32, shape, dim)

    tri = (iota((L, L), 1) <= iota((L, L), 0)).astype(BF16)
    same_head = (iota((W, W), 0) // N) == (iota((W, W), 1) // N)
    seg = same_head.astype(BF16)
    g_row, g_col = iota((2 * L, 2 * W), 0), iota((2 * L, 2 * W), 1)
    mask_g = (g_col % N) < (g_row % L) + (g_row >= L).astype(jnp.int32)
    n_double = max(1, (L - 1).bit_length())

    def stack(x):
        head0 = (iota(x.shape, 1) % W) < N
        zero = jnp.zeros_like(x)
        return jnp.concatenate([jnp.where(head0, x, zero), jnp.where(head0, zero, x)], axis=0)

    def seg_sum(x, passes):
        return _split_dot(x, seg, 0, passes)

    kk_p, ka_p, rk_p = kk_ref[...], ka_ref[...], rk_ref[...]
    lnw_p, lnb_p = lnw_ref[...], lnb_ref[...]
    state = {"ss": s_ref[...]}

    def chunk_stages(c):
        rows = slice(c * L, (c + 1) * L)
        r, k, v = r_ref[rows, :], k_ref[rows, :], v_ref[rows, :]
        lw, a = lw_ref[rows, :], a_ref[rows, :]
        kkr = k * kk_p
        kh = k * (1.0 + (a - 1.0) * ka_p)
        ssq = seg_sum(kkr * kkr, 2)
        lp = _split_dot(tri, lw, 1, 3)
        bonus_w = seg_sum(r * kh * rk_p, 1)
        yield
        kk = kkr / jnp.maximum(jnp.sqrt(ssq), 1e-12)
        ba = kk * a
        lp_end = lp[L - 1:L, :]
        e_neg = jnp.exp(-lp)
        e_end = jnp.exp(lp_end - lp)
        at = -kk * jnp.exp(lp - lw)
        rt = r * jnp.exp(lp)
        bh = (ba * e_end).astype(BF16)
        khh = (kh * e_end).astype(BF16)
        v16 = v.astype(BF16)
        ar = jnp.concatenate([at, rt], axis=0)
        bk = jnp.concatenate([stack((ba * e_neg).astype(BF16)),
                              stack((kh * e_neg).astype(BF16))], axis=0)
        gm = _bdot(ar, bk, _NT)
        yield
        gm = jnp.where(mask_g, gm, 0.0)
        nk, gak = gm[:L, :W], gm[:L, W:]
        grb, grk = gm[L:, :W], gm[L:, W:]
        sv = stack(v16)
        aakv = _bdot(gak, sv)
        yield
        w = jnp.concatenate([at, aakv], axis=1)
        for it in range(n_double):
            nk16, w16 = nk.astype(BF16), w.astype(BF16)
            if it + 1 < n_double:
                p = _bdot(nk16, jnp.concatenate([stack(nk16), stack(w16)], axis=1))
                yield
                nk, w = p[:, :W], w + p[:, W:]
            else:
                p = _bdot(nk16, stack(w16))
                yield
                w = w + p
        p2 = _bdot(grb, stack(w.astype(BF16)))
        p3 = _bdot(grk, sv)
        tn = _bdot(w.T, bh)
        vk = _bdot(v.T, khh)
        yield
        rp = rt + p2[:, :W]
        yv = p2[:, W:] + p3
        zero = jnp.zeros((W, W), F32)
        m_bd = jnp.where(same_head, tn[:W], zero)
        sv_bd = jnp.where(same_head, tn[W:] + vk, zero)
        ss = state["ss"]
        ss16 = ss.astype(BF16)
        ys = _bdot(rp, ss16, _NT)
        state["ss"] = ss * jnp.exp(lp_end) + _bdot(ss16, m_bd) + sv_bd
        yield
        y = yv + ys
        mean = seg_sum(y, 1) * (1.0 / N)
        yield
        yc = y - mean
        var = seg_sum(yc * yc, 1) * (1.0 / N)
        yield
        yn = yc * lax.rsqrt(var + GN_EPS) * lnw_p + lnb_p
        o_ref[rows, :] = ((yn + bonus_w * v) * g_ref[rows, :]).astype(o_ref.dtype)

    pending = {c: chunk_stages(c) for c in range(n_chunks)}
    slot = 0
    while pending:
        for c in sorted(pending):
            if c <= slot and next(pending[c], True):
                del pending[c]
        slot += 1
    s_ref[...] = state["ss"]


def _rwkv_recurrence(rkv, v_mixed, lw, a, g, k_k, k_a, r_k, ln_w, ln_b):
    _, b, t, d = rkv.shape
    tb = _tile(t, RWKV_TIME_BLOCK, CHUNK)
    tok = pl.BlockSpec((None, tb, LANES), lambda bi, hi, ti: (bi, ti, hi))
    par = pl.BlockSpec((1, LANES), lambda bi, hi, ti: (0, hi))

    def stacked(c):
        return pl.BlockSpec((None, None, tb, LANES), lambda bi, hi, ti: (c, bi, ti, hi))

    v_spec, v_arg = (stacked(2), rkv) if v_mixed is None else (tok, v_mixed)
    return pl.pallas_call(
        functools.partial(_rwkv_body, n_chunks=tb // CHUNK),
        grid=(b, d // LANES, t // tb),
        in_specs=[stacked(0), stacked(1), v_spec, tok, tok, tok] + [par] * 5,
        out_specs=tok,
        out_shape=jax.ShapeDtypeStruct((b, t, d), BF16),
        scratch_shapes=[pltpu.VMEM((LANES, LANES), F32)],
        compiler_params=_params("parallel", "parallel", "arbitrary"),
        name="rwkv7_recurrence",
    )(rkv, rkv, v_arg, lw, a, g, *(p.reshape(1, d) for p in (k_k, k_a, r_k, ln_w, ln_b)))


def _rwkv_layer(x, seq, v_first, norm_g, mu, w_rkv, w_o, widx, w0, w_l1, w_l2, a0, a_l1, a_l2,
                g_l1, g_l2, k_k, k_a, r_k, ln_w, ln_b, v_res, next_g):
    m, d = x.shape
    bsz = m // seq
    xmix = _rms_mix(x, norm_g, mu, seq)
    rkv = _batched_matmul(xmix, w_rkv, widx, 3)
    lw = _lora(xmix, 3, w_l1, w_l2, "decay", bias=w0)
    a = _lora(xmix, 4, a_l1, a_l2, "lr", bias=a0)
    g = _lora(xmix, 5, g_l1, g_l2, "gate")
    sh = (bsz, seq, d)
    if v_res is None:
        v_first = rkv
        v_mixed = None
    else:
        v0, v_l1, v_l2 = v_res
        v_mixed = _lora(xmix, 2, v_l1, v_l2, "value_residual", bias=v0, v=rkv,
                        v_first=v_first).reshape(sh)
    yg = _rwkv_recurrence(rkv.reshape((3,) + sh), v_mixed, lw.reshape(sh), a.reshape(sh),
                          g.reshape(sh), k_k, k_a, r_k, ln_w, ln_b)
    return _resid_matmul(yg.reshape(m, d), w_o, widx, x, 1.0, next_g=next_g), v_first


def _s5_body(u_ref, wt_ref, bbt_ref, ca_ref, dsk_ref, vt_ref, c1_ref, c2_ref, o_ref, tt_ref,
             *, n_chunks, state):
    L, C = CHUNK, S5_GROUP
    LC = L * C
    taps = lax.dot_general(bbt_ref[...], ca_ref[...], _NN, precision=lax.Precision.HIGHEST,
                           preferred_element_type=F32)
    on_diag = (lax.broadcasted_iota(jnp.int32, taps.shape, 1)
               == lax.broadcasted_iota(jnp.int32, taps.shape, 0))
    taps = taps + jnp.where(on_diag, dsk_ref[...], 0.0)
    padded = jnp.concatenate([jnp.zeros_like(taps), taps], axis=1)
    lags_per_tile = LANES // C
    for sub in range(lags_per_tile):
        shifted = padded if sub == 0 else pltpu.roll(padded, sub * C, axis=1)
        shifted = shifted.astype(BF16)
        for q in range(L // lags_per_tile):
            lag = q * lags_per_tile + sub
            tt_ref[lag * C:(lag + 1) * C, :] = shifted[:, LC - q * LANES:2 * LC - q * LANES]
    u = u_ref[...]
    s = jnp.dot(u, wt_ref[...], preferred_element_type=F32)
    chunk_idx = lax.broadcasted_iota(jnp.int32, s.shape, 0) % n_chunks
    step = 1
    it = 0
    while step < n_chunks:
        sp = jnp.where(chunk_idx >= step, pltpu.roll(s, step, axis=0), 0.0)
        s = s + c1_ref[it:it + 1, :] * sp + c2_ref[it:it + 1, :] * pltpu.roll(sp, state, axis=1)
        step *= 2
        it += 1
    s_start = jnp.where(chunk_idx >= 1, pltpu.roll(s, 1, axis=0), 0.0)
    y = (jnp.dot(u, tt_ref[...], preferred_element_type=F32)
         + jnp.dot(s_start.astype(BF16), vt_ref[...], preferred_element_type=F32))
    o_ref[...] = jax.nn.gelu(y).astype(o_ref.dtype)


def _s5_operators(lam_re, lam_im, log_step, b_re, b_im, c_re, c_im, d_skip, n_chunks):
    g, p = lam_re.shape
    c = S5_GROUP
    L = CHUNK
    lr = jnp.minimum(lam_re.astype(F32), LAMBDA_RE_MAX)
    li = lam_im.astype(F32)
    dt = jnp.exp(log_step.astype(F32))[:, None]
    ldt_re, ldt_im = lr * dt, li * dt
    mag = jnp.exp(ldt_re)
    ab_re, ab_im = mag * jnp.cos(ldt_im), mag * jnp.sin(ldt_im)
    den = lr * lr + li * li
    q_re = ((ab_re - 1.0) * lr + ab_im * li) / den
    q_im = (ab_im * lr - (ab_re - 1.0) * li) / den
    br, bi = b_re.astype(F32), b_im.astype(F32)
    bb_re = q_re[..., None] * br - q_im[..., None] * bi
    bb_im = q_re[..., None] * bi + q_im[..., None] * br
    cr, ci = c_re.astype(F32), c_im.astype(F32)
    dsk = d_skip.astype(F32).reshape(g, c)

    def power(n):
        m_ = jnp.exp(n * ldt_re)
        return m_ * jnp.cos(n * ldt_im), m_ * jnp.sin(n * ldt_im)

    lag = jnp.arange(0, L + 1, dtype=F32)[:, None, None]
    pw_re, pw_im = power(lag)
    pwt_re, pwt_im = pw_re.transpose(1, 2, 0), pw_im.transpose(1, 2, 0)
    crt, cit = cr.transpose(0, 2, 1), ci.transpose(0, 2, 1)
    ca = jnp.concatenate([
        crt[:, :, None, :] * pwt_re[..., None] - cit[:, :, None, :] * pwt_im[..., None],
        -(crt[:, :, None, :] * pwt_im[..., None] + cit[:, :, None, :] * pwt_re[..., None])],
        axis=1)
    ca_lo = ca[:, :, :L].reshape(g, 2 * p, L * c)
    vt = ca[:, :, 1:].reshape(g, 2 * p, L * c).astype(BF16)
    bbt_re, bbt_im = bb_re.transpose(0, 2, 1), bb_im.transpose(0, 2, 1)
    bbt = jnp.concatenate([bbt_re, bbt_im], axis=-1)
    idx = jnp.arange(L)
    rev_re = pw_re[L - 1 - idx].transpose(1, 0, 2)[:, :, None, :]
    rev_im = pw_im[L - 1 - idx].transpose(1, 0, 2)[:, :, None, :]
    wt = jnp.concatenate([rev_re * bbt_re[:, None] - rev_im * bbt_im[:, None],
                          rev_re * bbt_im[:, None] + rev_im * bbt_re[:, None]],
                         axis=-1).reshape(g, L * c, 2 * p).astype(BF16)
    n_steps = max(1, (n_chunks - 1).bit_length())
    hop = (L * 2.0 ** jnp.arange(n_steps, dtype=F32))[:, None, None]
    hop_re, hop_im = power(hop)
    c1 = jnp.concatenate([hop_re, hop_re], axis=-1).transpose(1, 0, 2)
    c2 = jnp.concatenate([-hop_im, hop_im], axis=-1).transpose(1, 0, 2)
    return wt, bbt, ca_lo, dsk[:, :, None], vt, c1, c2


def _s5_core(h, seq, lam_re, lam_im, log_step, b_re, b_im, c_re, c_im, d_skip):
    m, d = h.shape
    g, p = lam_re.shape
    c, L = S5_GROUP, CHUNK
    n_chunks = seq // L
    rows = m // L
    wt, bbt, ca_lo, dsk, vt, c1, c2 = _s5_operators(lam_re, lam_im, log_step, b_re, b_im, c_re,
                                                    c_im, d_skip, n_chunks)
    n_steps = c1.shape[1]
    u = h.reshape(rows, L, g, c).transpose(2, 0, 1, 3).reshape(g, rows, L * c)
    grp = lambda *shape: pl.BlockSpec((None,) + shape, lambda gi: (gi, 0, 0))
    y = pl.pallas_call(
        functools.partial(_s5_body, n_chunks=n_chunks, state=p),
        grid=(g,),
        in_specs=[grp(rows, L * c), grp(L * c, 2 * p), grp(c, 2 * p), grp(2 * p, L * c),
                  grp(c, 1), grp(2 * p, L * c), grp(n_steps, 2 * p), grp(n_steps, 2 * p)],
        out_specs=grp(rows, L * c),
        out_shape=jax.ShapeDtypeStruct((g, rows, L * c), BF16),
        scratch_shapes=[pltpu.VMEM((L * c, L * c), BF16)],
        compiler_params=_params("parallel"),
        name="s5_chunk_scan",
    )(u, wt, bbt, ca_lo, dsk, vt, c1, c2)
    return y.reshape(g, rows, L, c).transpose(1, 2, 0, 3).reshape(m, d)


def _ffn(x, pre, w_in, w_out, widx, norm_g, next_g):
    if pre is None:
        act = _dual_matmul(_rmsnorm(x, norm_g, BF16), w_in, widx, "swiglu")
    else:
        act = _dual_matmul(pre[0], w_in, widx, "swiglu", ssq=pre[1])
    return _resid_matmul(act, w_out, widx, x, 0.5, next_g=next_g)


def kernel(x, ffn_norm, ffn_w_in, ffn_w_out, mix_norm, rwkv_mu, rwkv_w_rkv, rwkv_w_o, rwkv_w0, rwkv_w_l1, rwkv_w_l2, rwkv_a0, rwkv_a_l1, rwkv_a_l2, rwkv_v0, rwkv_v_l1, rwkv_v_l2, rwkv_g_l1, rwkv_g_l2, rwkv_k_k, rwkv_k_a, rwkv_r_k, rwkv_ln_w, rwkv_ln_b, s5_lam_re, s5_lam_im, s5_log_step, s5_b_re, s5_b_im, s5_c_re, s5_c_im, s5_d, s5_w_glu, final_norm):
    bsz, seq, d = x.shape
    depth = ffn_norm.shape[0]
    n_mixers = 2
    w_in, w_out = ffn_w_in.astype(BF16), ffn_w_out.astype(BF16)
    w_rkv, w_o, w_glu = rwkv_w_rkv.astype(BF16), rwkv_w_o.astype(BF16), s5_w_glu.astype(BF16)
    x = x.reshape(bsz * seq, d)
    v_first = None
    pre = None
    for i in range(depth):
        x = _ffn(x, pre, w_in, w_out, (i, 0), ffn_norm[i, 0], None)
        j = i // n_mixers
        if i % n_mixers == 0:
            v_res = None if j == 0 else (rwkv_v0[j - 1], rwkv_v_l1[j - 1], rwkv_v_l2[j - 1])
            (x, xg, ssq), v_first = _rwkv_layer(
                x, seq, v_first, mix_norm[i], rwkv_mu[j], w_rkv, w_o, (j,),
                rwkv_w0[j], rwkv_w_l1[j], rwkv_w_l2[j], rwkv_a0[j], rwkv_a_l1[j], rwkv_a_l2[j],
                rwkv_g_l1[j], rwkv_g_l2[j], rwkv_k_k[j], rwkv_k_a[j], rwkv_r_k[j],
                rwkv_ln_w[j], rwkv_ln_b[j], v_res, ffn_norm[i, 1])
        else:
            h = _rmsnorm(x, mix_norm[i], BF16)
            y = _s5_core(h, seq, s5_lam_re[j], s5_lam_im[j], s5_log_step[j], s5_b_re[j],
                         s5_b_im[j], s5_c_re[j], s5_c_im[j], s5_d[j])
            x, xg, ssq = _dual_matmul(y, w_glu, (j,), "glu_residual", res=x,
                                      next_g=ffn_norm[i, 1])
        if i + 1 < depth:
            x, xg, ssq = _ffn(x, (xg, ssq), w_in, w_out, (i, 1), None, ffn_norm[i + 1, 0])
            pre = (xg, ssq)
        else:
            x = _ffn(x, (xg, ssq), w_in, w_out, (i, 1), None, None)
    return _rmsnorm(x, final_norm, F32).reshape(bsz, seq, d)
```

```python
import functools
import math

import jax
import jax.numpy as jnp
from jax import lax
from jax.experimental import pallas as pl
from jax.experimental.pallas import tpu as pltpu

F32 = jnp.float32
BF16 = jnp.bfloat16

RMS_EPS = 1e-6
GN_EPS = 64e-5
LAMBDA_RE_MAX = -1e-4
HEAD = 64
CHUNK = 64
S5_GROUP = 16
LANES = 128
VMEM_LIMIT = 56 * 1024 * 1024

_NN = (((1,), (0,)), ((), ()))
_NT = (((1,), (1,)), ((), ()))
_TN = (((0,), (0,)), ((), ()))


def _params(*sem):
    return pltpu.CompilerParams(dimension_semantics=sem, vmem_limit_bytes=VMEM_LIMIT)


def _tile(n, pref, quantum):
    best = None
    t = quantum
    while t <= min(n, pref):
        if n % t == 0:
            best = t
        t += quantum
    return best if best is not None else n


def _rms_body(x_ref, g_ref, o_ref):
    x = x_ref[...]
    ms = jnp.mean(x * x, axis=-1, keepdims=True)
    o_ref[...] = (x * lax.rsqrt(ms + RMS_EPS) * g_ref[...]).astype(o_ref.dtype)


def _rmsnorm(x, g, out_dtype):
    m, d = x.shape
    bm = _tile(m, 256, 8)
    return pl.pallas_call(
        _rms_body,
        grid=(m // bm,),
        in_specs=[pl.BlockSpec((bm, d), lambda i: (i, 0)),
                  pl.BlockSpec((1, d), lambda i: (0, 0))],
        out_specs=pl.BlockSpec((bm, d), lambda i: (i, 0)),
        out_shape=jax.ShapeDtypeStruct((m, d), out_dtype),
        compiler_params=_params("parallel"),
        name="rmsnorm",
    )(x, g.reshape(1, d))


def _rms_mix_body(x_ref, g_ref, mu_ref, o_ref, carry_ref, *, tiles_per_seq):
    i = pl.program_id(0)

    @pl.when(i % tiles_per_seq == 0)
    def _():
        carry_ref[...] = jnp.zeros_like(carry_ref)

    x = x_ref[...]
    ms = jnp.mean(x * x, axis=-1, keepdims=True)
    h = x * lax.rsqrt(ms + RMS_EPS) * g_ref[...]
    bm = h.shape[0]
    row = lax.broadcasted_iota(jnp.int32, h.shape, 0)
    prev = jnp.where(row == 0, carry_ref[...], pltpu.roll(h, 1, axis=0))
    carry_ref[...] = h[bm - 1:bm, :]
    dx = prev - h
    for c in range(o_ref.shape[0]):
        o_ref[c] = (h + dx * mu_ref[c:c + 1, :]).astype(o_ref.dtype)


def _rms_mix(x, g, mu, seq):
    m, d = x.shape
    n_mix = mu.shape[0]
    bm = _tile(seq, 256, 8)
    return pl.pallas_call(
        functools.partial(_rms_mix_body, tiles_per_seq=seq // bm),
        grid=(m // bm,),
        in_specs=[pl.BlockSpec((bm, d), lambda i: (i, 0)),
                  pl.BlockSpec((1, d), lambda i: (0, 0)),
                  pl.BlockSpec((n_mix, d), lambda i: (0, 0))],
        out_specs=pl.BlockSpec((n_mix, bm, d), lambda i: (0, i, 0)),
        out_shape=jax.ShapeDtypeStruct((n_mix, m, d), BF16),
        scratch_shapes=[pltpu.VMEM((1, d), F32)],
        compiler_params=_params("arbitrary"),
        name="rms_token_shift_mix",
    )(x, g.reshape(1, d), mu)


def _w_spec(widx, k, bn, col):
    return pl.BlockSpec((None,) * len(widx) + (k, bn), lambda *gi: widx + (0, col(*gi)))


def _emit_next_norm_input(x_new, g_ref, xg_ref, ssq_ref):
    xg_ref[...] = (x_new * g_ref[...]).astype(xg_ref.dtype)
    ssq_ref[...] = jnp.sum(x_new * x_new, axis=-1, keepdims=True)


def _next_norm_specs(m, n, bm, bn):
    g_spec = pl.BlockSpec((1, bn), lambda i, j: (0, j))
    out_specs = [pl.BlockSpec((bm, bn), lambda i, j: (i, j)),
                 pl.BlockSpec((None, bm, 1), lambda i, j: (j, i, 0))]
    out_shapes = [jax.ShapeDtypeStruct((m, n), BF16),
                  jax.ShapeDtypeStruct((n // bn, m, 1), F32)]
    return g_spec, out_specs, out_shapes


def _dual_body(*refs, mode, normed, emit_next, k_dim):
    refs = list(refs)
    a_ref, w1_ref, w2_ref = refs[:3]
    del refs[:3]
    a = a_ref[...]
    p1 = jnp.dot(a, w1_ref[...], preferred_element_type=F32)
    p2 = jnp.dot(a, w2_ref[...], preferred_element_type=F32)
    if not normed:
        ssq_ref = refs.pop(0)
        rstd = lax.rsqrt(jnp.sum(ssq_ref[...], axis=0) * (1.0 / k_dim) + RMS_EPS)
        p1, p2 = p1 * rstd, p2 * rstd
    if mode == "swiglu":
        (o_ref,) = refs
        o_ref[...] = (jax.nn.silu(p1) * p2).astype(o_ref.dtype)
        return
    res_ref = refs.pop(0)
    x_new = res_ref[...] + p1 * jax.nn.sigmoid(p2)
    if emit_next:
        g_ref, o_ref, xg_ref, ssq_out_ref = refs
        _emit_next_norm_input(x_new, g_ref, xg_ref, ssq_out_ref)
    else:
        (o_ref,) = refs
    o_ref[...] = x_new


def _dual_matmul(a, w, widx, mode, ssq=None, res=None, next_g=None):
    m, k = a.shape
    n = w.shape[-1] // 2
    bm = _tile(m, 1024, 8)
    bn = _tile(n, 512, LANES)
    nj = n // bn
    in_specs = [pl.BlockSpec((bm, k), lambda i, j: (i, 0)),
                _w_spec(widx, k, bn, lambda i, j: j),
                _w_spec(widx, k, bn, lambda i, j: j + nj)]
    args = [a, w, w]
    if ssq is not None:
        in_specs.append(pl.BlockSpec((ssq.shape[0], bm, 1), lambda i, j: (0, i, 0)))
        args.append(ssq)
    tile = pl.BlockSpec((bm, bn), lambda i, j: (i, j))
    out_specs, out_shapes = [tile], [jax.ShapeDtypeStruct((m, n), BF16 if mode == "swiglu" else F32)]
    if mode == "glu_residual":
        in_specs.append(tile)
        args.append(res)
        if next_g is not None:
            g_spec, extra_specs, extra_shapes = _next_norm_specs(m, n, bm, bn)
            in_specs.append(g_spec)
            args.append(next_g.reshape(1, n))
            out_specs += extra_specs
            out_shapes += extra_shapes
    out = pl.pallas_call(
        functools.partial(_dual_body, mode=mode, normed=ssq is None,
                          emit_next=next_g is not None, k_dim=k),
        grid=(m // bm, nj),
        in_specs=in_specs,
        out_specs=out_specs,
        out_shape=out_shapes,
        compiler_params=_params("parallel", "parallel"),
        name="matmul_" + mode,
    )(*args)
    return out[0] if len(out) == 1 else tuple(out)


def _resid_body(a_ref, w_ref, res_ref, *rest, scale, emit_next):
    acc = jnp.dot(a_ref[...], w_ref[...], preferred_element_type=F32)
    x_new = res_ref[...] + scale * acc
    if emit_next:
        g_ref, o_ref, xg_ref, ssq_ref = rest
        _emit_next_norm_input(x_new, g_ref, xg_ref, ssq_ref)
    else:
        (o_ref,) = rest
    o_ref[...] = x_new


def _resid_matmul(a, w, widx, res, scale, next_g=None):
    m, k = a.shape
    n = w.shape[-1]
    bm = _tile(m, 1024, 8)
    bn = _tile(n, 512, LANES)
    tile = pl.BlockSpec((bm, bn), lambda i, j: (i, j))
    in_specs = [pl.BlockSpec((bm, k), lambda i, j: (i, 0)),
                _w_spec(widx, k, bn, lambda i, j: j), tile]
    args = [a, w, res]
    out_specs, out_shapes = [tile], [jax.ShapeDtypeStruct((m, n), F32)]
    if next_g is not None:
        g_spec, extra_specs, extra_shapes = _next_norm_specs(m, n, bm, bn)
        in_specs.append(g_spec)
        args.append(next_g.reshape(1, n))
        out_specs += extra_specs
        out_shapes += extra_shapes
    out = pl.pallas_call(
        functools.partial(_resid_body, scale=scale, emit_next=next_g is not None),
        grid=(m // bm, n // bn),
        in_specs=in_specs,
        out_specs=out_specs,
        out_shape=out_shapes,
        compiler_params=_params("parallel", "parallel"),
        name="matmul_residual",
    )(*args)
    return out[0] if len(out) == 1 else tuple(out)


def _plain_body(a_ref, w_ref, o_ref):
    o_ref[...] = jnp.dot(a_ref[...], w_ref[...], preferred_element_type=F32)


def _batched_matmul(a, w, widx, n_batch):
    _, m, k = a.shape
    n = w.shape[-1]
    bm = _tile(m, 1024, 8)
    bn = _tile(n, 1024, LANES)
    return pl.pallas_call(
        _plain_body,
        grid=(n_batch, m // bm, n // bn),
        in_specs=[pl.BlockSpec((None, bm, k), lambda c, i, j: (c, i, 0)),
                  pl.BlockSpec((None,) * (len(widx) + 1) + (k, bn),
                               lambda c, i, j: widx + (c, 0, j))],
        out_specs=pl.BlockSpec((None, bm, bn), lambda c, i, j: (c, i, j)),
        out_shape=jax.ShapeDtypeStruct((n_batch, m, n), F32),
        compiler_params=_params("parallel", "parallel", "parallel"),
        name="matmul_rkv",
    )(a, w)


_DECAY_SCALE = math.exp(-0.5)


def _lora_body(x_ref, l1_ref, l2_ref, *rest, mode):
    t = jnp.dot(x_ref[...], l1_ref[...], preferred_element_type=F32)
    if mode == "decay":
        t = jnp.tanh(t)
    elif mode == "gate":
        t = jax.nn.sigmoid(t)
    z = jnp.dot(t.astype(BF16), l2_ref[...], preferred_element_type=F32)
    if mode == "decay":
        b_ref, o_ref = rest
        o_ref[...] = -_DECAY_SCALE * jax.nn.sigmoid(z + b_ref[...])
    elif mode == "lr":
        b_ref, o_ref = rest
        o_ref[...] = jax.nn.sigmoid(z + b_ref[...])
    elif mode == "gate":
        (o_ref,) = rest
        o_ref[...] = z
    else:
        b_ref, v_ref, vf_ref, o_ref = rest
        v = v_ref[...]
        o_ref[...] = v + (vf_ref[...] - v) * jax.nn.sigmoid(z + b_ref[...])


def _lora(xmix, c, l1, l2, mode, bias=None, v=None, v_first=None):
    _, m, d = xmix.shape
    rank = l1.shape[1]
    pad = (-rank) % LANES
    if pad:
        l1 = jnp.pad(l1, ((0, 0), (0, pad)))
        l2 = jnp.pad(l2, ((0, pad), (0, 0)))
    rp = rank + pad
    bm = _tile(m, 256, 8)
    row = pl.BlockSpec((bm, d), lambda i: (i, 0))
    vec = pl.BlockSpec((1, d), lambda i: (0, 0))
    in_specs = [pl.BlockSpec((None, bm, d), lambda i: (c, i, 0)),
                pl.BlockSpec((d, rp), lambda i: (0, 0)),
                pl.BlockSpec((rp, d), lambda i: (0, 0))]
    args = [xmix, l1.astype(BF16), l2.astype(BF16)]
    if mode != "gate":
        in_specs.append(vec)
        args.append(bias.reshape(1, d))
    if mode == "value_residual":
        stacked_v = pl.BlockSpec((None, bm, d), lambda i: (2, i, 0))
        in_specs += [stacked_v, stacked_v]
        args += [v, v_first]
    return pl.pallas_call(
        functools.partial(_lora_body, mode=mode),
        grid=(m // bm,),
        in_specs=in_specs,
        out_specs=row,
        out_shape=jax.ShapeDtypeStruct((m, d), F32),
        compiler_params=_params("parallel"),
        name="lora_" + mode,
    )(*args)


RWKV_TIME_BLOCK = 512
RWKV_PAIRS_PER_STEP = 4


def _bdot(a, b, dims=_NN):
    return lax.dot_general(a.astype(BF16), b.astype(BF16), dims, preferred_element_type=F32)


def _split_dot(a, b, split, passes):
    acc = None
    rem = (a, b)[split]
    for _ in range(passes):
        part = rem.astype(BF16)
        term = (lax.dot_general(part, b, _NN, preferred_element_type=F32) if split == 0 else
                lax.dot_general(a, part, _NN, preferred_element_type=F32))
        acc = term if acc is None else acc + term
        rem = rem - part.astype(F32)
    return acc


def _rwkv_body(r_ref, k_ref, v_ref, lw_ref, a_ref, g_ref,
               kk_ref, ka_ref, rk_ref, lnw_ref, lnb_ref, o_ref, s_ref, *, n_chunks, n_pairs):
    L, N, W = CHUNK, HEAD, LANES

    @pl.when(pl.program_id(2) == 0)
    def _():
        s_ref[...] = jnp.zeros_like(s_ref)

    def iota(shape, dim):
        return lax.broadcasted_iota(jnp.int32, shape, dim)

    tri = (iota((L, L), 1) <= iota((L, L), 0)).astype(BF16)
    same_head = (iota((W, W), 0) // N) == (iota((W, W), 1) // N)
    seg = same_head.astype(BF16)
    g_row, g_col = iota((2 * L, 2 * W), 0), iota((2 * L, 2 * W), 1)
    mask_g = (g_col % N) < (g_row % L) + (g_row >= L).astype(jnp.int32)
    n_double = max(1, (L - 1).bit_length())

    def stack(x):
        head0 = (iota(x.shape, 1) % W) < N
        zero = jnp.zeros_like(x)
        return jnp.concatenate([jnp.where(head0, x, zero), jnp.where(head0, zero, x)], axis=0)

    def seg_sum(x, passes):
        return _split_dot(x, seg, 0, passes)

    eye2 = ((iota((L, W), 1) % N) == iota((L, W), 0)).astype(F32)
    state = {pr: s_ref[pr] for pr in range(n_pairs)}

    def chunk_stages(c, pr):
        rows, lanes = slice(c * L, (c + 1) * L), slice(pr * W, (pr + 1) * W)
        r, k, v = r_ref[rows, lanes], k_ref[rows, lanes], v_ref[rows, lanes]
        lw, a = lw_ref[rows, lanes], a_ref[rows, lanes]
        kk_p, ka_p, rk_p = kk_ref[:, lanes], ka_ref[:, lanes], rk_ref[:, lanes]
        kkr = k * kk_p
        kh = k * (1.0 + (a - 1.0) * ka_p)
        ssq = seg_sum(kkr * kkr, 2)
        lp = _split_dot(tri, lw, 1, 3)
        bonus_w = seg_sum(r * kh * rk_p, 1)
        yield
        kk = kkr / jnp.maximum(jnp.sqrt(ssq), 1e-12)
        ba = kk * a
        lp_end = lp[L - 1:L, :]
        e_neg = jnp.exp(-lp)
        e_end = jnp.exp(lp_end - lp)
        at = -kk * jnp.exp(lp - lw)
        rt = r * jnp.exp(lp)
        v16 = v.astype(BF16)
        ar = jnp.concatenate([at, rt], axis=0)
        bk = jnp.concatenate([stack((ba * e_neg).astype(BF16)),
                              stack((kh * e_neg).astype(BF16))], axis=0)
        gm = _bdot(ar, bk, _NT)
        bk_end_t = jnp.concatenate([ba * e_end, kh * e_end], axis=0).T.astype(BF16)
        p_col = jnp.broadcast_to(jnp.exp(lp_end), (W, W)).T
        yield
        gm = jnp.where(mask_g, gm, 0.0)
        nk, gak = gm[:L, :W], gm[:L, W:]
        grb, grk = gm[L:, :W], gm[L:, W:]
        sv = stack(v16)
        nk16 = nk.astype(BF16)
        aakv = _bdot(gak, sv)
        npow = _bdot(nk16, stack(nk16))
        yield
        t = eye2 + nk
        for it in range(1, n_double):
            np16, t16 = npow.astype(BF16), t.astype(BF16)
            if it + 1 < n_double:
                p = _bdot(np16, jnp.concatenate([stack(np16), stack(t16)], axis=1))
                yield
                npow, t = p[:, :W], t + p[:, W:]
            else:
                p = _bdot(np16, stack(t16))
                yield
                t = t + p
        w16 = _bdot(t, stack(jnp.concatenate([at, aakv], axis=1).astype(BF16))).astype(BF16)
        yield
        p2 = _bdot(grb, stack(w16))
        p3 = _bdot(grk, sv)
        below = jnp.concatenate([jnp.zeros((L, W), BF16), v16], axis=1)
        mt = _bdot(bk_end_t, jnp.concatenate([w16, below], axis=0))
        yield
        rp = rt + p2[:, :W]
        yv = p2[:, W:] + p3
        zero = jnp.zeros((W, W), F32)
        m_bd_t = jnp.where(same_head, mt[:, :W], zero)
        sv_bd_t = jnp.where(same_head, mt[:, W:], zero)
        sst = state[pr]
        sst16 = sst.astype(BF16)
        ys = _bdot(rp, sst16)
        state[pr] = p_col * sst + _bdot(m_bd_t, sst16) + sv_bd_t
        yield
        y = yv + ys
        mean = seg_sum(y, 1) * (1.0 / N)
        yield
        yc = y - mean
        var = seg_sum(yc * yc, 1) * (1.0 / N)
        yield
        yn = yc * lax.rsqrt(var + GN_EPS) * lnw_ref[:, lanes] + lnb_ref[:, lanes]
        o_ref[rows, lanes] = ((yn + bonus_w * v) * g_ref[rows, lanes]).astype(o_ref.dtype)

    pending = {(c, pr): chunk_stages(c, pr) for c in range(n_chunks) for pr in range(n_pairs)}
    slot = 0
    while pending:
        for key in sorted(pending):
            if key[0] <= slot and next(pending[key], True):
                del pending[key]
        slot += 1
    for pr in range(n_pairs):
        s_ref[pr] = state[pr]


def _rwkv_recurrence(rkv, v_mixed, lw, a, g, k_k, k_a, r_k, ln_w, ln_b):
    _, b, t, d = rkv.shape
    tb = _tile(t, RWKV_TIME_BLOCK, CHUNK)
    width = _tile(d, RWKV_PAIRS_PER_STEP * LANES, LANES)
    tok = pl.BlockSpec((None, tb, width), lambda bi, hi, ti: (bi, ti, hi))
    par = pl.BlockSpec((1, width), lambda bi, hi, ti: (0, hi))

    def stacked(c):
        return pl.BlockSpec((None, None, tb, width), lambda bi, hi, ti: (c, bi, ti, hi))

    v_spec, v_arg = (stacked(2), rkv) if v_mixed is None else (tok, v_mixed)
    return pl.pallas_call(
        functools.partial(_rwkv_body, n_chunks=tb // CHUNK, n_pairs=width // LANES),
        grid=(b, d // width, t // tb),
        in_specs=[stacked(0), stacked(1), v_spec, tok, tok, tok] + [par] * 5,
        out_specs=tok,
        out_shape=jax.ShapeDtypeStruct((b, t, d), BF16),
        scratch_shapes=[pltpu.VMEM((width // LANES, LANES, LANES), F32)],
        compiler_params=_params("parallel", "parallel", "arbitrary"),
        name="rwkv7_recurrence",
    )(rkv, rkv, v_arg, lw, a, g, *(p.reshape(1, d) for p in (k_k, k_a, r_k, ln_w, ln_b)))


def _rwkv_layer(x, seq, v_first, norm_g, mu, w_rkv, w_o, widx, w0, w_l1, w_l2, a0, a_l1, a_l2,
                g_l1, g_l2, k_k, k_a, r_k, ln_w, ln_b, v_res, next_g):
    m, d = x.shape
    bsz = m // seq
    xmix = _rms_mix(x, norm_g, mu, seq)
    rkv = _batched_matmul(xmix, w_rkv, widx, 3)
    lw = _lora(xmix, 3, w_l1, w_l2, "decay", bias=w0)
    a = _lora(xmix, 4, a_l1, a_l2, "lr", bias=a0)
    g = _lora(xmix, 5, g_l1, g_l2, "gate")
    sh = (bsz, seq, d)
    if v_res is None:
        v_first = rkv
        v_mixed = None
    else:
        v0, v_l1, v_l2 = v_res
        v_mixed = _lora(xmix, 2, v_l1, v_l2, "value_residual", bias=v0, v=rkv,
                        v_first=v_first).reshape(sh)
    yg = _rwkv_recurrence(rkv.reshape((3,) + sh), v_mixed, lw.reshape(sh), a.reshape(sh),
                          g.reshape(sh), k_k, k_a, r_k, ln_w, ln_b)
    return _resid_matmul(yg.reshape(m, d), w_o, widx, x, 1.0, next_g=next_g), v_first


def _s5_body(u_ref, wt_ref, bbt_ref, ca_ref, dsk_ref, vt_ref, c1_ref, c2_ref, o_ref, tt_ref,
             *, n_chunks, state):
    L, C = CHUNK, S5_GROUP
    LC = L * C
    taps = lax.dot_general(bbt_ref[...], ca_ref[...], _NN, precision=lax.Precision.HIGHEST,
                           preferred_element_type=F32)
    on_diag = (lax.broadcasted_iota(jnp.int32, taps.shape, 1)
               == lax.broadcasted_iota(jnp.int32, taps.shape, 0))
    taps = taps + jnp.where(on_diag, dsk_ref[...], 0.0)
    padded = jnp.concatenate([jnp.zeros_like(taps), taps], axis=1)
    lags_per_tile = LANES // C
    for sub in range(lags_per_tile):
        shifted = padded if sub == 0 else pltpu.roll(padded, sub * C, axis=1)
        shifted = shifted.astype(BF16)
        for q in range(L // lags_per_tile):
            lag = q * lags_per_tile + sub
            tt_ref[lag * C:(lag + 1) * C, :] = shifted[:, LC - q * LANES:2 * LC - q * LANES]
    u = u_ref[...]
    s = jnp.dot(u, wt_ref[...], preferred_element_type=F32)
    chunk_idx = lax.broadcasted_iota(jnp.int32, s.shape, 0) % n_chunks
    step = 1
    it = 0
    while step < n_chunks:
        sp = jnp.where(chunk_idx >= step, pltpu.roll(s, step, axis=0), 0.0)
        s = s + c1_ref[it:it + 1, :] * sp + c2_ref[it:it + 1, :] * pltpu.roll(sp, state, axis=1)
        step *= 2
        it += 1
    s_start = jnp.where(chunk_idx >= 1, pltpu.roll(s, 1, axis=0), 0.0)
    y = (jnp.dot(u, tt_ref[...], preferred_element_type=F32)
         + jnp.dot(s_start.astype(BF16), vt_ref[...], preferred_element_type=F32))
    o_ref[...] = jax.nn.gelu(y).astype(o_ref.dtype)


def _s5_operators(lam_re, lam_im, log_step, b_re, b_im, c_re, c_im, d_skip, n_chunks):
    g, p = lam_re.shape
    c = S5_GROUP
    L = CHUNK
    lr = jnp.minimum(lam_re.astype(F32), LAMBDA_RE_MAX)
    li = lam_im.astype(F32)
    dt = jnp.exp(log_step.astype(F32))[:, None]
    ldt_re, ldt_im = lr * dt, li * dt
    mag = jnp.exp(ldt_re)
    ab_re, ab_im = mag * jnp.cos(ldt_im), mag * jnp.sin(ldt_im)
    den = lr * lr + li * li
    q_re = ((ab_re - 1.0) * lr + ab_im * li) / den
    q_im = (ab_im * lr - (ab_re - 1.0) * li) / den
    br, bi = b_re.astype(F32), b_im.astype(F32)
    bb_re = q_re[..., None] * br - q_im[..., None] * bi
    bb_im = q_re[..., None] * bi + q_im[..., None] * br
    cr, ci = c_re.astype(F32), c_im.astype(F32)
    dsk = d_skip.astype(F32).reshape(g, c)

    def power(n):
        m_ = jnp.exp(n * ldt_re)
        return m_ * jnp.cos(n * ldt_im), m_ * jnp.sin(n * ldt_im)

    lag = jnp.arange(0, L + 1, dtype=F32)[:, None, None]
    pw_re, pw_im = power(lag)
    pwt_re, pwt_im = pw_re.transpose(1, 2, 0), pw_im.transpose(1, 2, 0)
    crt, cit = cr.transpose(0, 2, 1), ci.transpose(0, 2, 1)
    ca = jnp.concatenate([
        crt[:, :, None, :] * pwt_re[..., None] - cit[:, :, None, :] * pwt_im[..., None],
        -(crt[:, :, None, :] * pwt_im[..., None] + cit[:, :, None, :] * pwt_re[..., None])],
        axis=1)
    ca_lo = ca[:, :, :L].reshape(g, 2 * p, L * c)
    vt = ca[:, :, 1:].reshape(g, 2 * p, L * c).astype(BF16)
    bbt_re, bbt_im = bb_re.transpose(0, 2, 1), bb_im.transpose(0, 2, 1)
    bbt = jnp.concatenate([bbt_re, bbt_im], axis=-1)
    idx = jnp.arange(L)
    rev_re = pw_re[L - 1 - idx].transpose(1, 0, 2)[:, :, None, :]
    rev_im = pw_im[L - 1 - idx].transpose(1, 0, 2)[:, :, None, :]
    wt = jnp.concatenate([rev_re * bbt_re[:, None] - rev_im * bbt_im[:, None],
                          rev_re * bbt_im[:, None] + rev_im * bbt_re[:, None]],
                         axis=-1).reshape(g, L * c, 2 * p).astype(BF16)
    n_steps = max(1, (n_chunks - 1).bit_length())
    hop = (L * 2.0 ** jnp.arange(n_steps, dtype=F32))[:, None, None]
    hop_re, hop_im = power(hop)
    c1 = jnp.concatenate([hop_re, hop_re], axis=-1).transpose(1, 0, 2)
    c2 = jnp.concatenate([-hop_im, hop_im], axis=-1).transpose(1, 0, 2)
    return wt, bbt, ca_lo, dsk[:, :, None], vt, c1, c2


def _s5_core(h, seq, lam_re, lam_im, log_step, b_re, b_im, c_re, c_im, d_skip):
    m, d = h.shape
    g, p = lam_re.shape
    c, L = S5_GROUP, CHUNK
    n_chunks = seq // L
    rows = m // L
    wt, bbt, ca_lo, dsk, vt, c1, c2 = _s5_operators(lam_re, lam_im, log_step, b_re, b_im, c_re,
                                                    c_im, d_skip, n_chunks)
    n_steps = c1.shape[1]
    u = h.reshape(rows, L, g, c).transpose(2, 0, 1, 3).reshape(g, rows, L * c)
    grp = lambda *shape: pl.BlockSpec((None,) + shape, lambda gi: (gi, 0, 0))
    y = pl.pallas_call(
        functools.partial(_s5_body, n_chunks=n_chunks, state=p),
        grid=(g,),
        in_specs=[grp(rows, L * c), grp(L * c, 2 * p), grp(c, 2 * p), grp(2 * p, L * c),
                  grp(c, 1), grp(2 * p, L * c), grp(n_steps, 2 * p), grp(n_steps, 2 * p)],
        out_specs=grp(rows, L * c),
        out_shape=jax.ShapeDtypeStruct((g, rows, L * c), BF16),
        scratch_shapes=[pltpu.VMEM((L * c, L * c), BF16)],
        compiler_params=_params("parallel"),
        name="s5_chunk_scan",
    )(u, wt, bbt, ca_lo, dsk, vt, c1, c2)
    return y.reshape(g, rows, L, c).transpose(1, 2, 0, 3).reshape(m, d)


def _ffn(x, pre, w_in, w_out, widx, norm_g, next_g):
    if pre is None:
        act = _dual_matmul(_rmsnorm(x, norm_g, BF16), w_in, widx, "swiglu")
    else:
        act = _dual_matmul(pre[0], w_in, widx, "swiglu", ssq=pre[1])
    return _resid_matmul(act, w_out, widx, x, 0.5, next_g=next_g)


def kernel(x, ffn_norm, ffn_w_in, ffn_w_out, mix_norm, rwkv_mu, rwkv_w_rkv, rwkv_w_o, rwkv_w0, rwkv_w_l1, rwkv_w_l2, rwkv_a0, rwkv_a_l1, rwkv_a_l2, rwkv_v0, rwkv_v_l1, rwkv_v_l2, rwkv_g_l1, rwkv_g_l2, rwkv_k_k, rwkv_k_a, rwkv_r_k, rwkv_ln_w, rwkv_ln_b, s5_lam_re, s5_lam_im, s5_log_step, s5_b_re, s5_b_im, s5_c_re, s5_c_im, s5_d, s5_w_glu, final_norm):
    bsz, seq, d = x.shape
    depth = ffn_norm.shape[0]
    n_mixers = 2
    w_in, w_out = ffn_w_in.astype(BF16), ffn_w_out.astype(BF16)
    w_rkv, w_o, w_glu = rwkv_w_rkv.astype(BF16), rwkv_w_o.astype(BF16), s5_w_glu.astype(BF16)
    x = x.reshape(bsz * seq, d)
    v_first = None
    pre = None
    for i in range(depth):
        x = _ffn(x, pre, w_in, w_out, (i, 0), ffn_norm[i, 0], None)
        j = i // n_mixers
        if i % n_mixers == 0:
            v_res = None if j == 0 else (rwkv_v0[j - 1], rwkv_v_l1[j - 1], rwkv_v_l2[j - 1])
            (x, xg, ssq), v_first = _rwkv_layer(
                x, seq, v_first, mix_norm[i], rwkv_mu[j], w_rkv, w_o, (j,),
                rwkv_w0[j], rwkv_w_l1[j], rwkv_w_l2[j], rwkv_a0[j], rwkv_a_l1[j], rwkv_a_l2[j],
                rwkv_g_l1[j], rwkv_g_l2[j], rwkv_k_k[j], rwkv_k_a[j], rwkv_r_k[j],
                rwkv_ln_w[j], rwkv_ln_b[j], v_res, ffn_norm[i, 1])
        else:
            h = _rmsnorm(x, mix_norm[i], BF16)
            y = _s5_core(h, seq, s5_lam_re[j], s5_lam_im[j], s5_log_step[j], s5_b_re[j],
                         s5_b_im[j], s5_c_re[j], s5_c_im[j], s5_d[j])
            x, xg, ssq = _dual_matmul(y, w_glu, (j,), "glu_residual", res=x,
                                      next_g=ffn_norm[i, 1])
        if i + 1 < depth:
            x, xg, ssq = _ffn(x, (xg, ssq), w_in, w_out, (i, 1), None, ffn_norm[i + 1, 0])
            pre = (xg, ssq)
        else:
            x = _ffn(x, (xg, ssq), w_in, w_out, (i, 1), None, None)
    return _rmsnorm(x, final_norm, F32).reshape(bsz, seq, d)
```

```python
import functools
import math

import jax
import jax.numpy as jnp
from jax import lax
from jax.experimental import pallas as pl
from jax.experimental.pallas import tpu as pltpu

F32 = jnp.float32
BF16 = jnp.bfloat16

RMS_EPS = 1e-6
GN_EPS = 64e-5
LAMBDA_RE_MAX = -1e-4
HEAD = 64
CHUNK = 64
S5_GROUP = 16
LANES = 128
VMEM_LIMIT = 56 * 1024 * 1024

_NN = (((1,), (0,)), ((), ()))
_NT = (((1,), (1,)), ((), ()))
_TN = (((0,), (0,)), ((), ()))


def _params(*sem):
    return pltpu.CompilerParams(dimension_semantics=sem, vmem_limit_bytes=VMEM_LIMIT)


def _tile(n, pref, quantum):
    best = None
    t = quantum
    while t <= min(n, pref):
        if n % t == 0:
            best = t
        t += quantum
    return best if best is not None else n


def _rms_body(x_ref, g_ref, o_ref):
    x = x_ref[...]
    ms = jnp.mean(x * x, axis=-1, keepdims=True)
    o_ref[...] = (x * lax.rsqrt(ms + RMS_EPS) * g_ref[...]).astype(o_ref.dtype)


def _rmsnorm(x, g, out_dtype):
    m, d = x.shape
    bm = _tile(m, 256, 8)
    return pl.pallas_call(
        _rms_body,
        grid=(m // bm,),
        in_specs=[pl.BlockSpec((bm, d), lambda i: (i, 0)),
                  pl.BlockSpec((1, d), lambda i: (0, 0))],
        out_specs=pl.BlockSpec((bm, d), lambda i: (i, 0)),
        out_shape=jax.ShapeDtypeStruct((m, d), out_dtype),
        compiler_params=_params("parallel"),
        name="rmsnorm",
    )(x, g.reshape(1, d))


_DECAY_SCALE = math.exp(-0.5)


def _rwkv_front_body(*refs, tiles_per_seq, has_vres):
    x_ref, g_ref, mu_ref = refs[:3]
    n_lora = 11 if has_vres else 8
    lora = refs[3:3 + n_lora]
    outs = refs[3 + n_lora:-1]
    carry_ref = refs[-1]
    wl1, wl2, w0, al1, al2, a0, gl1, gl2 = lora[:8]
    xmix_ref, lw_ref, a_ref, gate_ref = outs[:4]
    i = pl.program_id(0)

    @pl.when(i % tiles_per_seq == 0)
    def _():
        carry_ref[...] = jnp.zeros_like(carry_ref)

    x = x_ref[...]
    ms = jnp.mean(x * x, axis=-1, keepdims=True)
    h = x * lax.rsqrt(ms + RMS_EPS) * g_ref[...]
    bm = h.shape[0]
    row = lax.broadcasted_iota(jnp.int32, h.shape, 0)
    prev = jnp.where(row == 0, carry_ref[...], pltpu.roll(h, 1, axis=0))
    carry_ref[...] = h[bm - 1:bm, :]
    dx = prev - h

    def mixed(c):
        return (h + dx * mu_ref[c:c + 1, :]).astype(BF16)

    def low_rank(xc, l1_ref, l2_ref, inner):
        t = inner(jnp.dot(xc, l1_ref[...], preferred_element_type=F32))
        return jnp.dot(t.astype(BF16), l2_ref[...], preferred_element_type=F32)

    xmix_ref[0] = mixed(0)
    xmix_ref[1] = mixed(1)
    xv = mixed(2)
    xmix_ref[2] = xv
    lw_ref[...] = -_DECAY_SCALE * jax.nn.sigmoid(low_rank(mixed(3), wl1, wl2, jnp.tanh) + w0[...])
    a_ref[...] = jax.nn.sigmoid(low_rank(mixed(4), al1, al2, lambda t: t) + a0[...])
    gate_ref[...] = low_rank(mixed(5), gl1, gl2, jax.nn.sigmoid)
    if has_vres:
        vl1, vl2, v0 = lora[8:]
        outs[4][...] = jax.nn.sigmoid(low_rank(xv, vl1, vl2, lambda t: t) + v0[...])


def _pad_rank(l1, l2):
    pad = (-l1.shape[1]) % LANES
    return (jnp.pad(l1, ((0, 0), (0, pad))).astype(BF16),
            jnp.pad(l2, ((0, pad), (0, 0))).astype(BF16))


def _rwkv_front(x, g, mu, seq, w_lora, a_lora, g_lora, v_lora):
    m, d = x.shape
    bm = _tile(seq, 128, 8)
    row = pl.BlockSpec((bm, d), lambda i: (i, 0))
    vec = pl.BlockSpec((1, d), lambda i: (0, 0))
    in_specs = [row, vec, pl.BlockSpec(mu.shape, lambda i: (0, 0))]
    args = [x, g.reshape(1, d), mu]

    def add_pair(l1, l2):
        l1, l2 = _pad_rank(l1, l2)
        in_specs.extend([pl.BlockSpec(l1.shape, lambda i: (0, 0)),
                         pl.BlockSpec(l2.shape, lambda i: (0, 0))])
        args.extend([l1, l2])

    def add_bias(b):
        in_specs.append(vec)
        args.append(b.reshape(1, d))

    w0, w_l1, w_l2 = w_lora
    a0, a_l1, a_l2 = a_lora
    add_pair(w_l1, w_l2)
    add_bias(w0)
    add_pair(a_l1, a_l2)
    add_bias(a0)
    add_pair(*g_lora)
    n_tok_out = 3
    if v_lora is not None:
        v0, v_l1, v_l2 = v_lora
        add_pair(v_l1, v_l2)
        add_bias(v0)
        n_tok_out = 4
    return pl.pallas_call(
        functools.partial(_rwkv_front_body, tiles_per_seq=seq // bm, has_vres=v_lora is not None),
        grid=(m // bm,),
        in_specs=in_specs,
        out_specs=[pl.BlockSpec((3, bm, d), lambda i: (0, i, 0))] + [row] * n_tok_out,
        out_shape=[jax.ShapeDtypeStruct((3, m, d), BF16)]
        + [jax.ShapeDtypeStruct((m, d), F32)] * n_tok_out,
        scratch_shapes=[pltpu.VMEM((1, d), F32)],
        compiler_params=_params("arbitrary"),
        name="rwkv_front",
    )(*args)


def _w_spec(widx, k, bn, col):
    return pl.BlockSpec((None,) * len(widx) + (k, bn), lambda *gi: widx + (0, col(*gi)))


def _emit_next_norm_input(x_new, g_ref, xg_ref, ssq_ref):
    xg_ref[...] = (x_new * g_ref[...]).astype(xg_ref.dtype)
    ssq_ref[...] = jnp.sum(x_new * x_new, axis=-1, keepdims=True)


def _next_norm_specs(m, n, bm, bn):
    g_spec = pl.BlockSpec((1, bn), lambda i, j: (0, j))
    out_specs = [pl.BlockSpec((bm, bn), lambda i, j: (i, j)),
                 pl.BlockSpec((None, bm, 1), lambda i, j: (j, i, 0))]
    out_shapes = [jax.ShapeDtypeStruct((m, n), BF16),
                  jax.ShapeDtypeStruct((n // bn, m, 1), F32)]
    return g_spec, out_specs, out_shapes


def _dual_body(*refs, mode, normed, emit_next, k_dim):
    refs = list(refs)
    a_ref, w1_ref, w2_ref = refs[:3]
    del refs[:3]
    a = a_ref[...]
    p1 = jnp.dot(a, w1_ref[...], preferred_element_type=F32)
    p2 = jnp.dot(a, w2_ref[...], preferred_element_type=F32)
    if not normed:
        ssq_ref = refs.pop(0)
        rstd = lax.rsqrt(jnp.sum(ssq_ref[...], axis=0) * (1.0 / k_dim) + RMS_EPS)
        p1, p2 = p1 * rstd, p2 * rstd
    if mode == "swiglu":
        (o_ref,) = refs
        o_ref[...] = (jax.nn.silu(p1) * p2).astype(o_ref.dtype)
        return
    res_ref = refs.pop(0)
    x_new = res_ref[...] + p1 * jax.nn.sigmoid(p2)
    if emit_next:
        g_ref, o_ref, xg_ref, ssq_out_ref = refs
        _emit_next_norm_input(x_new, g_ref, xg_ref, ssq_out_ref)
    else:
        (o_ref,) = refs
    o_ref[...] = x_new


def _dual_matmul(a, w, widx, mode, ssq=None, res=None, next_g=None):
    m, k = a.shape
    n = w.shape[-1] // 2
    bm = _tile(m, 1024, 8)
    bn = _tile(n, 512, LANES)
    nj = n // bn
    in_specs = [pl.BlockSpec((bm, k), lambda i, j: (i, 0)),
                _w_spec(widx, k, bn, lambda i, j: j),
                _w_spec(widx, k, bn, lambda i, j: j + nj)]
    args = [a, w, w]
    if ssq is not None:
        in_specs.append(pl.BlockSpec((ssq.shape[0], bm, 1), lambda i, j: (0, i, 0)))
        args.append(ssq)
    tile = pl.BlockSpec((bm, bn), lambda i, j: (i, j))
    out_specs, out_shapes = [tile], [jax.ShapeDtypeStruct((m, n), BF16 if mode == "swiglu" else F32)]
    if mode == "glu_residual":
        in_specs.append(tile)
        args.append(res)
        if next_g is not None:
            g_spec, extra_specs, extra_shapes = _next_norm_specs(m, n, bm, bn)
            in_specs.append(g_spec)
            args.append(next_g.reshape(1, n))
            out_specs += extra_specs
            out_shapes += extra_shapes
    out = pl.pallas_call(
        functools.partial(_dual_body, mode=mode, normed=ssq is None,
                          emit_next=next_g is not None, k_dim=k),
        grid=(m // bm, nj),
        in_specs=in_specs,
        out_specs=out_specs,
        out_shape=out_shapes,
        compiler_params=_params("parallel", "parallel"),
        name="matmul_" + mode,
    )(*args)
    return out[0] if len(out) == 1 else tuple(out)


def _resid_body(a_ref, w_ref, res_ref, *rest, scale, emit_next):
    acc = jnp.dot(a_ref[...], w_ref[...], preferred_element_type=F32)
    x_new = res_ref[...] + scale * acc
    if emit_next:
        g_ref, o_ref, xg_ref, ssq_ref = rest
        _emit_next_norm_input(x_new, g_ref, xg_ref, ssq_ref)
    else:
        (o_ref,) = rest
    o_ref[...] = x_new


def _resid_matmul(a, w, widx, res, scale, next_g=None):
    m, k = a.shape
    n = w.shape[-1]
    bm = _tile(m, 1024, 8)
    bn = _tile(n, 512, LANES)
    tile = pl.BlockSpec((bm, bn), lambda i, j: (i, j))
    in_specs = [pl.BlockSpec((bm, k), lambda i, j: (i, 0)),
                _w_spec(widx, k, bn, lambda i, j: j), tile]
    args = [a, w, res]
    out_specs, out_shapes = [tile], [jax.ShapeDtypeStruct((m, n), F32)]
    if next_g is not None:
        g_spec, extra_specs, extra_shapes = _next_norm_specs(m, n, bm, bn)
        in_specs.append(g_spec)
        args.append(next_g.reshape(1, n))
        out_specs += extra_specs
        out_shapes += extra_shapes
    out = pl.pallas_call(
        functools.partial(_resid_body, scale=scale, emit_next=next_g is not None),
        grid=(m // bm, n // bn),
        in_specs=in_specs,
        out_specs=out_specs,
        out_shape=out_shapes,
        compiler_params=_params("parallel", "parallel"),
        name="matmul_residual",
    )(*args)
    return out[0] if len(out) == 1 else tuple(out)


def _plain_body(a_ref, w_ref, o_ref):
    o_ref[...] = jnp.dot(a_ref[...], w_ref[...], preferred_element_type=F32)


def _batched_matmul(a, w, widx, n_batch):
    _, m, k = a.shape
    n = w.shape[-1]
    bm = _tile(m, 1024, 8)
    bn = _tile(n, 1024, LANES)
    return pl.pallas_call(
        _plain_body,
        grid=(n_batch, m // bm, n // bn),
        in_specs=[pl.BlockSpec((None, bm, k), lambda c, i, j: (c, i, 0)),
                  pl.BlockSpec((None,) * (len(widx) + 1) + (k, bn),
                               lambda c, i, j: widx + (c, 0, j))],
        out_specs=pl.BlockSpec((None, bm, bn), lambda c, i, j: (c, i, j)),
        out_shape=jax.ShapeDtypeStruct((n_batch, m, n), F32),
        compiler_params=_params("parallel", "parallel", "parallel"),
        name="matmul_rkv",
    )(a, w)


RWKV_TIME_BLOCK = 512
RWKV_PAIRS_PER_STEP = 4


def _bdot(a, b, dims=_NN):
    return lax.dot_general(a.astype(BF16), b.astype(BF16), dims, preferred_element_type=F32)


def _split_dot(a, b, split, passes):
    acc = None
    rem = (a, b)[split]
    for _ in range(passes):
        part = rem.astype(BF16)
        term = (lax.dot_general(part, b, _NN, preferred_element_type=F32) if split == 0 else
                lax.dot_general(a, part, _NN, preferred_element_type=F32))
        acc = term if acc is None else acc + term
        rem = rem - part.astype(F32)
    return acc


def _rwkv_body(r_ref, k_ref, v_ref, lw_ref, a_ref, g_ref, kk_ref, ka_ref, rk_ref, lnw_ref, lnb_ref,
               *rest, n_chunks, n_pairs, has_vres):
    L, N, W = CHUNK, HEAD, LANES
    if has_vres:
        vfirst_ref, vgate_ref, o_ref, s_ref = rest
    else:
        o_ref, s_ref = rest

    @pl.when(pl.program_id(2) == 0)
    def _():
        s_ref[...] = jnp.zeros_like(s_ref)

    def iota(shape, dim):
        return lax.broadcasted_iota(jnp.int32, shape, dim)

    tri = (iota((L, L), 1) <= iota((L, L), 0)).astype(BF16)
    same_head = (iota((W, W), 0) // N) == (iota((W, W), 1) // N)
    seg = same_head.astype(BF16)
    g_row, g_col = iota((2 * L, 2 * W), 0), iota((2 * L, 2 * W), 1)
    mask_g = (g_col % N) < (g_row % L) + (g_row >= L).astype(jnp.int32)
    n_double = max(1, (L - 1).bit_length())

    def stack(x):
        head0 = (iota(x.shape, 1) % W) < N
        zero = jnp.zeros_like(x)
        return jnp.concatenate([jnp.where(head0, x, zero), jnp.where(head0, zero, x)], axis=0)

    def seg_sum(x, passes):
        return _split_dot(x, seg, 0, passes)

    eye2 = ((iota((L, W), 1) % N) == iota((L, W), 0)).astype(F32)
    state = {pr: s_ref[pr] for pr in range(n_pairs)}

    def chunk_stages(c, pr):
        rows, lanes = slice(c * L, (c + 1) * L), slice(pr * W, (pr + 1) * W)
        r, k, v = r_ref[rows, lanes], k_ref[rows, lanes], v_ref[rows, lanes]
        if has_vres:
            v = v + (vfirst_ref[rows, lanes] - v) * vgate_ref[rows, lanes]
        lw, a = lw_ref[rows, lanes], a_ref[rows, lanes]
        kk_p, ka_p, rk_p = kk_ref[:, lanes], ka_ref[:, lanes], rk_ref[:, lanes]
        kkr = k * kk_p
        kh = k * (1.0 + (a - 1.0) * ka_p)
        ssq = seg_sum(kkr * kkr, 2)
        lp = _split_dot(tri, lw, 1, 3)
        bonus_w = seg_sum(r * kh * rk_p, 1)
        yield
        kk = kkr / jnp.maximum(jnp.sqrt(ssq), 1e-12)
        ba = kk * a
        lp_end = lp[L - 1:L, :]
        e_neg = jnp.exp(-lp)
        e_end = jnp.exp(lp_end - lp)
        at = -kk * jnp.exp(lp - lw)
        rt = r * jnp.exp(lp)
        v16 = v.astype(BF16)
        ar = jnp.concatenate([at, rt], axis=0)
        bk = jnp.concatenate([stack((ba * e_neg).astype(BF16)),
                              stack((kh * e_neg).astype(BF16))], axis=0)
        gm = _bdot(ar, bk, _NT)
        bk_end_t = jnp.concatenate([ba * e_end, kh * e_end], axis=0).T.astype(BF16)
        p_col = jnp.broadcast_to(jnp.exp(lp_end), (W, W)).T
        yield
        gm = jnp.where(mask_g, gm, 0.0)
        nk, gak = gm[:L, :W], gm[:L, W:]
        grb, grk = gm[L:, :W], gm[L:, W:]
        sv = stack(v16)
        nk16 = nk.astype(BF16)
        aakv = _bdot(gak, sv)
        npow = _bdot(nk16, stack(nk16))
        yield
        t = eye2 + nk
        for it in range(1, n_double):
            np16, t16 = npow.astype(BF16), t.astype(BF16)
            if it + 1 < n_double:
                p = _bdot(np16, jnp.concatenate([stack(np16), stack(t16)], axis=1))
                yield
                npow, t = p[:, :W], t + p[:, W:]
            else:
                p = _bdot(np16, stack(t16))
                yield
                t = t + p
        w16 = _bdot(t, stack(jnp.concatenate([at, aakv], axis=1).astype(BF16))).astype(BF16)
        yield
        p2 = _bdot(grb, stack(w16))
        p3 = _bdot(grk, sv)
        below = jnp.concatenate([jnp.zeros((L, W), BF16), v16], axis=1)
        mt = _bdot(bk_end_t, jnp.concatenate([w16, below], axis=0))
        yield
        rp = rt + p2[:, :W]
        yv = p2[:, W:] + p3
        zero = jnp.zeros((W, W), F32)
        m_bd_t = jnp.where(same_head, mt[:, :W], zero)
        sv_bd_t = jnp.where(same_head, mt[:, W:], zero)
        sst = state[pr]
        sst16 = sst.astype(BF16)
        ys = _bdot(rp, sst16)
        state[pr] = p_col * sst + _bdot(m_bd_t, sst16) + sv_bd_t
        yield
        y = yv + ys
        mean = seg_sum(y, 1) * (1.0 / N)
        yield
        yc = y - mean
        var = seg_sum(yc * yc, 1) * (1.0 / N)
        yield
        yn = yc * lax.rsqrt(var + GN_EPS) * lnw_ref[:, lanes] + lnb_ref[:, lanes]
        o_ref[rows, lanes] = ((yn + bonus_w * v) * g_ref[rows, lanes]).astype(o_ref.dtype)

    pending = {(c, pr): chunk_stages(c, pr) for c in range(n_chunks) for pr in range(n_pairs)}
    slot = 0
    while pending:
        for key in sorted(pending):
            if key[0] <= slot and next(pending[key], True):
                del pending[key]
        slot += 1
    for pr in range(n_pairs):
        s_ref[pr] = state[pr]


def _rwkv_recurrence(rkv, v_mix, lw, a, g, k_k, k_a, r_k, ln_w, ln_b):
    _, b, t, d = rkv.shape
    tb = _tile(t, RWKV_TIME_BLOCK, CHUNK)
    width = _tile(d, RWKV_PAIRS_PER_STEP * LANES, LANES)
    tok = pl.BlockSpec((None, tb, width), lambda bi, hi, ti: (bi, ti, hi))
    par = pl.BlockSpec((1, width), lambda bi, hi, ti: (0, hi))

    def stacked(c):
        return pl.BlockSpec((None, None, tb, width), lambda bi, hi, ti: (c, bi, ti, hi))

    in_specs = [stacked(0), stacked(1), stacked(2), tok, tok, tok] + [par] * 5
    args = [rkv, rkv, rkv, lw, a, g, *(p.reshape(1, d) for p in (k_k, k_a, r_k, ln_w, ln_b))]
    if v_mix is not None:
        in_specs += [stacked(2), tok]
        args += list(v_mix)
    return pl.pallas_call(
        functools.partial(_rwkv_body, n_chunks=tb // CHUNK, n_pairs=width // LANES,
                          has_vres=v_mix is not None),
        grid=(b, d // width, t // tb),
        in_specs=in_specs,
        out_specs=tok,
        out_shape=jax.ShapeDtypeStruct((b, t, d), BF16),
        scratch_shapes=[pltpu.VMEM((width // LANES, LANES, LANES), F32)],
        compiler_params=_params("parallel", "parallel", "arbitrary"),
        name="rwkv7_recurrence",
    )(*args)


def _rwkv_layer(x, seq, v_first, norm_g, mu, w_rkv, w_o, widx, w0, w_l1, w_l2, a0, a_l1, a_l2,
                g_l1, g_l2, k_k, k_a, r_k, ln_w, ln_b, v_res, next_g):
    m, d = x.shape
    bsz = m // seq
    xmix, lw, a, g, *v_gate = _rwkv_front(x, norm_g, mu, seq, (w0, w_l1, w_l2), (a0, a_l1, a_l2),
                                          (g_l1, g_l2), v_res)
    rkv = _batched_matmul(xmix, w_rkv, widx, 3)
    sh = (bsz, seq, d)
    rkv4 = rkv.reshape((3,) + sh)
    if v_res is None:
        v_first = rkv4
        v_mix = None
    else:
        v_mix = (v_first, v_gate[0].reshape(sh))
    yg = _rwkv_recurrence(rkv4, v_mix, lw.reshape(sh), a.reshape(sh), g.reshape(sh),
                          k_k, k_a, r_k, ln_w, ln_b)
    return _resid_matmul(yg.reshape(m, d), w_o, widx, x, 1.0, next_g=next_g), v_first


def _s5_body(u_ref, wt_ref, bbt_ref, ca_ref, dsk_ref, vt_ref, c1_ref, c2_ref, o_ref, tt_ref,
             *, n_chunks, state):
    L, C = CHUNK, S5_GROUP
    LC = L * C
    taps = lax.dot_general(bbt_ref[...], ca_ref[...], _NN, precision=lax.Precision.HIGHEST,
                           preferred_element_type=F32)
    on_diag = (lax.broadcasted_iota(jnp.int32, taps.shape, 1)
               == lax.broadcasted_iota(jnp.int32, taps.shape, 0))
    taps = taps + jnp.where(on_diag, dsk_ref[...], 0.0)
    padded = jnp.concatenate([jnp.zeros_like(taps), taps], axis=1)
    lags_per_tile = LANES // C
    for sub in range(lags_per_tile):
        shifted = padded if sub == 0 else pltpu.roll(padded, sub * C, axis=1)
        shifted = shifted.astype(BF16)
        for q in range(L // lags_per_tile):
            lag = q * lags_per_tile + sub
            tt_ref[lag * C:(lag + 1) * C, :] = shifted[:, LC - q * LANES:2 * LC - q * LANES]
    u = u_ref[...]
    s = jnp.dot(u, wt_ref[...], preferred_element_type=F32)
    chunk_idx = lax.broadcasted_iota(jnp.int32, s.shape, 0) % n_chunks
    step = 1
    it = 0
    while step < n_chunks:
        sp = jnp.where(chunk_idx >= step, pltpu.roll(s, step, axis=0), 0.0)
        s = s + c1_ref[it:it + 1, :] * sp + c2_ref[it:it + 1, :] * pltpu.roll(sp, state, axis=1)
        step *= 2
        it += 1
    s_start = jnp.where(chunk_idx >= 1, pltpu.roll(s, 1, axis=0), 0.0)
    y = (jnp.dot(u, tt_ref[...], preferred_element_type=F32)
         + jnp.dot(s_start.astype(BF16), vt_ref[...], preferred_element_type=F32))
    o_ref[...] = jax.nn.gelu(y).astype(o_ref.dtype)


def _s5_operators(lam_re, lam_im, log_step, b_re, b_im, c_re, c_im, d_skip, n_chunks):
    g, p = lam_re.shape
    c = S5_GROUP
    L = CHUNK
    lr = jnp.minimum(lam_re.astype(F32), LAMBDA_RE_MAX)
    li = lam_im.astype(F32)
    dt = jnp.exp(log_step.astype(F32))[:, None]
    ldt_re, ldt_im = lr * dt, li * dt
    mag = jnp.exp(ldt_re)
    ab_re, ab_im = mag * jnp.cos(ldt_im), mag * jnp.sin(ldt_im)
    den = lr * lr + li * li
    q_re = ((ab_re - 1.0) * lr + ab_im * li) / den
    q_im = (ab_im * lr - (ab_re - 1.0) * li) / den
    br, bi = b_re.astype(F32), b_im.astype(F32)
    bb_re = q_re[..., None] * br - q_im[..., None] * bi
    bb_im = q_re[..., None] * bi + q_im[..., None] * br
    cr, ci = c_re.astype(F32), c_im.astype(F32)
    dsk = d_skip.astype(F32).reshape(g, c)

    def power(n):
        m_ = jnp.exp(n * ldt_re)
        return m_ * jnp.cos(n * ldt_im), m_ * jnp.sin(n * ldt_im)

    lag = jnp.arange(0, L + 1, dtype=F32)[:, None, None]
    pw_re, pw_im = power(lag)
    pwt_re, pwt_im = pw_re.transpose(1, 2, 0), pw_im.transpose(1, 2, 0)
    crt, cit = cr.transpose(0, 2, 1), ci.transpose(0, 2, 1)
    ca = jnp.concatenate([
        crt[:, :, None, :] * pwt_re[..., None] - cit[:, :, None, :] * pwt_im[..., None],
        -(crt[:, :, None, :] * pwt_im[..., None] + cit[:, :, None, :] * pwt_re[..., None])],
        axis=1)
    ca_lo = ca[:, :, :L].reshape(g, 2 * p, L * c)
    vt = ca[:, :, 1:].reshape(g, 2 * p, L * c).astype(BF16)
    bbt_re, bbt_im = bb_re.transpose(0, 2, 1), bb_im.transpose(0, 2, 1)
    bbt = jnp.concatenate([bbt_re, bbt_im], axis=-1)
    idx = jnp.arange(L)
    rev_re = pw_re[L - 1 - idx].transpose(1, 0, 2)[:, :, None, :]
    rev_im = pw_im[L - 1 - idx].transpose(1, 0, 2)[:, :, None, :]
    wt = jnp.concatenate([rev_re * bbt_re[:, None] - rev_im * bbt_im[:, None],
                          rev_re * bbt_im[:, None] + rev_im * bbt_re[:, None]],
                         axis=-1).reshape(g, L * c, 2 * p).astype(BF16)
    n_steps = max(1, (n_chunks - 1).bit_length())
    hop = (L * 2.0 ** jnp.arange(n_steps, dtype=F32))[:, None, None]
    hop_re, hop_im = power(hop)
    c1 = jnp.concatenate([hop_re, hop_re], axis=-1).transpose(1, 0, 2)
    c2 = jnp.concatenate([-hop_im, hop_im], axis=-1).transpose(1, 0, 2)
    return wt, bbt, ca_lo, dsk[:, :, None], vt, c1, c2


def _s5_core(h, seq, lam_re, lam_im, log_step, b_re, b_im, c_re, c_im, d_skip):
    m, d = h.shape
    g, p = lam_re.shape
    c, L = S5_GROUP, CHUNK
    n_chunks = seq // L
    rows = m // L
    wt, bbt, ca_lo, dsk, vt, c1, c2 = _s5_operators(lam_re, lam_im, log_step, b_re, b_im, c_re,
                                                    c_im, d_skip, n_chunks)
    n_steps = c1.shape[1]
    u = h.reshape(rows, L, g, c).transpose(2, 0, 1, 3).reshape(g, rows, L * c)
    grp = lambda *shape: pl.BlockSpec((None,) + shape, lambda gi: (gi, 0, 0))
    y = pl.pallas_call(
        functools.partial(_s5_body, n_chunks=n_chunks, state=p),
        grid=(g,),
        in_specs=[grp(rows, L * c), grp(L * c, 2 * p), grp(c, 2 * p), grp(2 * p, L * c),
                  grp(c, 1), grp(2 * p, L * c), grp(n_steps, 2 * p), grp(n_steps, 2 * p)],
        out_specs=grp(rows, L * c),
        out_shape=jax.ShapeDtypeStruct((g, rows, L * c), BF16),
        scratch_shapes=[pltpu.VMEM((L * c, L * c), BF16)],
        compiler_params=_params("parallel"),
        name="s5_chunk_scan",
    )(u, wt, bbt, ca_lo, dsk, vt, c1, c2)
    return y.reshape(g, rows, L, c).transpose(1, 2, 0, 3).reshape(m, d)


def _ffn(x, pre, w_in, w_out, widx, norm_g, next_g):
    if pre is None:
        act = _dual_matmul(_rmsnorm(x, norm_g, BF16), w_in, widx, "swiglu")
    else:
        act = _dual_matmul(pre[0], w_in, widx, "swiglu", ssq=pre[1])
    return _resid_matmul(act, w_out, widx, x, 0.5, next_g=next_g)


def kernel(x, ffn_norm, ffn_w_in, ffn_w_out, mix_norm, rwkv_mu, rwkv_w_rkv, rwkv_w_o, rwkv_w0, rwkv_w_l1, rwkv_w_l2, rwkv_a0, rwkv_a_l1, rwkv_a_l2, rwkv_v0, rwkv_v_l1, rwkv_v_l2, rwkv_g_l1, rwkv_g_l2, rwkv_k_k, rwkv_k_a, rwkv_r_k, rwkv_ln_w, rwkv_ln_b, s5_lam_re, s5_lam_im, s5_log_step, s5_b_re, s5_b_im, s5_c_re, s5_c_im, s5_d, s5_w_glu, final_norm):
    bsz, seq, d = x.shape
    depth = ffn_norm.shape[0]
    n_mixers = 2
    w_in, w_out = ffn_w_in.astype(BF16), ffn_w_out.astype(BF16)
    w_rkv, w_o, w_glu = rwkv_w_rkv.astype(BF16), rwkv_w_o.astype(BF16), s5_w_glu.astype(BF16)
    x = x.reshape(bsz * seq, d)
    v_first = None
    pre = None
    for i in range(depth):
        x = _ffn(x, pre, w_in, w_out, (i, 0), ffn_norm[i, 0], None)
        j = i // n_mixers
        if i % n_mixers == 0:
            v_res = None if j == 0 else (rwkv_v0[j - 1], rwkv_v_l1[j - 1], rwkv_v_l2[j - 1])
            (x, xg, ssq), v_first = _rwkv_layer(
                x, seq, v_first, mix_norm[i], rwkv_mu[j], w_rkv, w_o, (j,),
                rwkv_w0[j], rwkv_w_l1[j], rwkv_w_l2[j], rwkv_a0[j], rwkv_a_l1[j], rwkv_a_l2[j],
                rwkv_g_l1[j], rwkv_g_l2[j], rwkv_k_k[j], rwkv_k_a[j], rwkv_r_k[j],
                rwkv_ln_w[j], rwkv_ln_b[j], v_res, ffn_norm[i, 1])
        else:
            h = _rmsnorm(x, mix_norm[i], BF16)
            y = _s5_core(h, seq, s5_lam_re[j], s5_lam_im[j], s5_log_step[j], s5_b_re[j],
                         s5_b_im[j], s5_c_re[j], s5_c_im[j], s5_d[j])
            x, xg, ssq = _dual_matmul(y, w_glu, (j,), "glu_residual", res=x,
                                      next_g=ffn_norm[i, 1])
        if i + 1 < depth:
            x, xg, ssq = _ffn(x, (xg, ssq), w_in, w_out, (i, 1), None, ffn_norm[i + 1, 0])
            pre = (xg, ssq)
        else:
            x = _ffn(x, (xg, ssq), w_in, w_out, (i, 1), None, None)
    return _rmsnorm(x, final_norm, F32).reshape(bsz, seq, d)
```

```python
import functools
import math

import jax
import jax.numpy as jnp
from jax import lax
from jax.experimental import pallas as pl
from jax.experimental.pallas import tpu as pltpu

F32 = jnp.float32
BF16 = jnp.bfloat16

RMS_EPS = 1e-6
GN_EPS = 64e-5
LAMBDA_RE_MAX = -1e-4
HEAD = 64
CHUNK = 64
S5_CHUNK = 32
S5_GROUP = 16
LANES = 128
VMEM_LIMIT = 56 * 1024 * 1024

_NN = (((1,), (0,)), ((), ()))
_NT = (((1,), (1,)), ((), ()))
_TN = (((0,), (0,)), ((), ()))


def _params(*sem):
    return pltpu.CompilerParams(dimension_semantics=sem, vmem_limit_bytes=VMEM_LIMIT)


def _tile(n, pref, quantum):
    best = None
    t = quantum
    while t <= min(n, pref):
        if n % t == 0:
            best = t
        t += quantum
    return best if best is not None else n


def _rms_body(x_ref, g_ref, o_ref):
    x = x_ref[...]
    ms = jnp.mean(x * x, axis=-1, keepdims=True)
    o_ref[...] = (x * lax.rsqrt(ms + RMS_EPS) * g_ref[...]).astype(o_ref.dtype)


def _rmsnorm(x, g, out_dtype):
    m, d = x.shape
    bm = _tile(m, 256, 8)
    return pl.pallas_call(
        _rms_body,
        grid=(m // bm,),
        in_specs=[pl.BlockSpec((bm, d), lambda i: (i, 0)),
                  pl.BlockSpec((1, d), lambda i: (0, 0))],
        out_specs=pl.BlockSpec((bm, d), lambda i: (i, 0)),
        out_shape=jax.ShapeDtypeStruct((m, d), out_dtype),
        compiler_params=_params("parallel"),
        name="rmsnorm",
    )(x, g.reshape(1, d))


_DECAY_SCALE = math.exp(-0.5)


def _rwkv_front_body(*refs, tiles_per_seq, has_vres):
    x_ref, g_ref, mu_ref = refs[:3]
    n_lora = 11 if has_vres else 8
    lora = refs[3:3 + n_lora]
    outs = refs[3 + n_lora:-1]
    carry_ref = refs[-1]
    wl1, wl2, w0, al1, al2, a0, gl1, gl2 = lora[:8]
    xmix_ref, lw_ref, a_ref, gate_ref = outs[:4]
    i = pl.program_id(0)

    @pl.when(i % tiles_per_seq == 0)
    def _():
        carry_ref[...] = jnp.zeros_like(carry_ref)

    x = x_ref[...]
    ms = jnp.mean(x * x, axis=-1, keepdims=True)
    h = x * lax.rsqrt(ms + RMS_EPS) * g_ref[...]
    bm = h.shape[0]
    row = lax.broadcasted_iota(jnp.int32, h.shape, 0)
    prev = jnp.where(row == 0, carry_ref[...], pltpu.roll(h, 1, axis=0))
    carry_ref[...] = h[bm - 1:bm, :]
    dx = prev - h

    def mixed(c):
        return (h + dx * mu_ref[c:c + 1, :]).astype(BF16)

    def low_rank(xc, l1_ref, l2_ref, inner):
        t = inner(jnp.dot(xc, l1_ref[...], preferred_element_type=F32))
        return jnp.dot(t.astype(BF16), l2_ref[...], preferred_element_type=F32)

    xmix_ref[0] = mixed(0)
    xmix_ref[1] = mixed(1)
    xv = mixed(2)
    xmix_ref[2] = xv
    lw_ref[...] = -_DECAY_SCALE * jax.nn.sigmoid(low_rank(mixed(3), wl1, wl2, jnp.tanh) + w0[...])
    a_ref[...] = jax.nn.sigmoid(low_rank(mixed(4), al1, al2, lambda t: t) + a0[...])
    gate_ref[...] = low_rank(mixed(5), gl1, gl2, jax.nn.sigmoid)
    if has_vres:
        vl1, vl2, v0 = lora[8:]
        outs[4][...] = jax.nn.sigmoid(low_rank(xv, vl1, vl2, lambda t: t) + v0[...])


def _pad_rank(l1, l2):
    pad = (-l1.shape[1]) % LANES
    return (jnp.pad(l1, ((0, 0), (0, pad))).astype(BF16),
            jnp.pad(l2, ((0, pad), (0, 0))).astype(BF16))


def _rwkv_front(x, g, mu, seq, w_lora, a_lora, g_lora, v_lora):
    m, d = x.shape
    bm = _tile(seq, 128, 8)
    row = pl.BlockSpec((bm, d), lambda i: (i, 0))
    vec = pl.BlockSpec((1, d), lambda i: (0, 0))
    in_specs = [row, vec, pl.BlockSpec(mu.shape, lambda i: (0, 0))]
    args = [x, g.reshape(1, d), mu]

    def add_pair(l1, l2):
        l1, l2 = _pad_rank(l1, l2)
        in_specs.extend([pl.BlockSpec(l1.shape, lambda i: (0, 0)),
                         pl.BlockSpec(l2.shape, lambda i: (0, 0))])
        args.extend([l1, l2])

    def add_bias(b):
        in_specs.append(vec)
        args.append(b.reshape(1, d))

    w0, w_l1, w_l2 = w_lora
    a0, a_l1, a_l2 = a_lora
    add_pair(w_l1, w_l2)
    add_bias(w0)
    add_pair(a_l1, a_l2)
    add_bias(a0)
    add_pair(*g_lora)
    n_tok_out = 3
    if v_lora is not None:
        v0, v_l1, v_l2 = v_lora
        add_pair(v_l1, v_l2)
        add_bias(v0)
        n_tok_out = 4
    return pl.pallas_call(
        functools.partial(_rwkv_front_body, tiles_per_seq=seq // bm, has_vres=v_lora is not None),
        grid=(m // bm,),
        in_specs=in_specs,
        out_specs=[pl.BlockSpec((3, bm, d), lambda i: (0, i, 0))] + [row] * n_tok_out,
        out_shape=[jax.ShapeDtypeStruct((3, m, d), BF16)]
        + [jax.ShapeDtypeStruct((m, d), F32)] * n_tok_out,
        scratch_shapes=[pltpu.VMEM((1, d), F32)],
        compiler_params=_params("arbitrary"),
        name="rwkv_front",
    )(*args)


def _w_spec(widx, k, bn, col):
    return pl.BlockSpec((None,) * len(widx) + (k, bn), lambda *gi: widx + (0, col(*gi)))


def _emit_next_norm_input(x_new, g_ref, xg_ref, ssq_ref):
    xg_ref[...] = (x_new * g_ref[...]).astype(xg_ref.dtype)
    ssq_ref[...] = jnp.sum(x_new * x_new, axis=-1, keepdims=True)


def _next_norm_specs(m, n, bm, bn):
    g_spec = pl.BlockSpec((1, bn), lambda i, j: (0, j))
    out_specs = [pl.BlockSpec((bm, bn), lambda i, j: (i, j)),
                 pl.BlockSpec((None, bm, 1), lambda i, j: (j, i, 0))]
    out_shapes = [jax.ShapeDtypeStruct((m, n), BF16),
                  jax.ShapeDtypeStruct((n // bn, m, 1), F32)]
    return g_spec, out_specs, out_shapes


def _dual_body(*refs, mode, normed, emit_next, k_dim):
    refs = list(refs)
    a_ref, w1_ref, w2_ref = refs[:3]
    del refs[:3]
    a = a_ref[...]
    p1 = jnp.dot(a, w1_ref[...], preferred_element_type=F32)
    p2 = jnp.dot(a, w2_ref[...], preferred_element_type=F32)
    if not normed:
        ssq_ref = refs.pop(0)
        rstd = lax.rsqrt(jnp.sum(ssq_ref[...], axis=0) * (1.0 / k_dim) + RMS_EPS)
        p1, p2 = p1 * rstd, p2 * rstd
    if mode == "swiglu":
        (o_ref,) = refs
        o_ref[...] = (jax.nn.silu(p1) * p2).astype(o_ref.dtype)
        return
    res_ref = refs.pop(0)
    x_new = res_ref[...] + p1 * jax.nn.sigmoid(p2)
    if emit_next:
        g_ref, o_ref, xg_ref, ssq_out_ref = refs
        _emit_next_norm_input(x_new, g_ref, xg_ref, ssq_out_ref)
    else:
        (o_ref,) = refs
    o_ref[...] = x_new


def _dual_matmul(a, w, widx, mode, ssq=None, res=None, next_g=None):
    m, k = a.shape
    n = w.shape[-1] // 2
    bm = _tile(m, 1024, 8)
    bn = _tile(n, 512, LANES)
    nj = n // bn
    in_specs = [pl.BlockSpec((bm, k), lambda i, j: (i, 0)),
                _w_spec(widx, k, bn, lambda i, j: j),
                _w_spec(widx, k, bn, lambda i, j: j + nj)]
    args = [a, w, w]
    if ssq is not None:
        in_specs.append(pl.BlockSpec((ssq.shape[0], bm, 1), lambda i, j: (0, i, 0)))
        args.append(ssq)
    tile = pl.BlockSpec((bm, bn), lambda i, j: (i, j))
    out_specs, out_shapes = [tile], [jax.ShapeDtypeStruct((m, n), BF16 if mode == "swiglu" else F32)]
    if mode == "glu_residual":
        in_specs.append(tile)
        args.append(res)
        if next_g is not None:
            g_spec, extra_specs, extra_shapes = _next_norm_specs(m, n, bm, bn)
            in_specs.append(g_spec)
            args.append(next_g.reshape(1, n))
            out_specs += extra_specs
            out_shapes += extra_shapes
    out = pl.pallas_call(
        functools.partial(_dual_body, mode=mode, normed=ssq is None,
                          emit_next=next_g is not None, k_dim=k),
        grid=(m // bm, nj),
        in_specs=in_specs,
        out_specs=out_specs,
        out_shape=out_shapes,
        compiler_params=_params("parallel", "parallel"),
        name="matmul_" + mode,
    )(*args)
    return out[0] if len(out) == 1 else tuple(out)


def _resid_body(a_ref, w_ref, res_ref, *rest, scale, emit_next):
    acc = jnp.dot(a_ref[...], w_ref[...], preferred_element_type=F32)
    x_new = res_ref[...] + scale * acc
    if emit_next:
        g_ref, o_ref, xg_ref, ssq_ref = rest
        _emit_next_norm_input(x_new, g_ref, xg_ref, ssq_ref)
    else:
        (o_ref,) = rest
    o_ref[...] = x_new


def _resid_matmul(a, w, widx, res, scale, next_g=None):
    m, k = a.shape
    n = w.shape[-1]
    bm = _tile(m, 1024, 8)
    bn = _tile(n, 512, LANES)
    tile = pl.BlockSpec((bm, bn), lambda i, j: (i, j))
    in_specs = [pl.BlockSpec((bm, k), lambda i, j: (i, 0)),
                _w_spec(widx, k, bn, lambda i, j: j), tile]
    args = [a, w, res]
    out_specs, out_shapes = [tile], [jax.ShapeDtypeStruct((m, n), F32)]
    if next_g is not None:
        g_spec, extra_specs, extra_shapes = _next_norm_specs(m, n, bm, bn)
        in_specs.append(g_spec)
        args.append(next_g.reshape(1, n))
        out_specs += extra_specs
        out_shapes += extra_shapes
    out = pl.pallas_call(
        functools.partial(_resid_body, scale=scale, emit_next=next_g is not None),
        grid=(m // bm, n // bn),
        in_specs=in_specs,
        out_specs=out_specs,
        out_shape=out_shapes,
        compiler_params=_params("parallel", "parallel"),
        name="matmul_residual",
    )(*args)
    return out[0] if len(out) == 1 else tuple(out)


def _plain_body(a_ref, w_ref, o_ref):
    o_ref[...] = jnp.dot(a_ref[...], w_ref[...], preferred_element_type=F32)


def _batched_matmul(a, w, widx, n_batch):
    _, m, k = a.shape
    n = w.shape[-1]
    bm = _tile(m, 1024, 8)
    bn = _tile(n, 1024, LANES)
    return pl.pallas_call(
        _plain_body,
        grid=(n_batch, m // bm, n // bn),
        in_specs=[pl.BlockSpec((None, bm, k), lambda c, i, j: (c, i, 0)),
                  pl.BlockSpec((None,) * (len(widx) + 1) + (k, bn),
                               lambda c, i, j: widx + (c, 0, j))],
        out_specs=pl.BlockSpec((None, bm, bn), lambda c, i, j: (c, i, j)),
        out_shape=jax.ShapeDtypeStruct((n_batch, m, n), F32),
        compiler_params=_params("parallel", "parallel", "parallel"),
        name="matmul_rkv",
    )(a, w)


RWKV_TIME_BLOCK = 512
RWKV_PAIRS_PER_STEP = 4


def _bdot(a, b, dims=_NN):
    return lax.dot_general(a.astype(BF16), b.astype(BF16), dims, preferred_element_type=F32)


def _split_dot(a, b, split, passes):
    acc = None
    rem = (a, b)[split]
    for _ in range(passes):
        part = rem.astype(BF16)
        term = (lax.dot_general(part, b, _NN, preferred_element_type=F32) if split == 0 else
                lax.dot_general(a, part, _NN, preferred_element_type=F32))
        acc = term if acc is None else acc + term
        rem = rem - part.astype(F32)
    return acc


def _rwkv_body(r_ref, k_ref, v_ref, lw_ref, a_ref, g_ref, kk_ref, ka_ref, rk_ref, lnw_ref, lnb_ref,
               *rest, n_chunks, n_pairs, has_vres):
    L, N, W = CHUNK, HEAD, LANES
    if has_vres:
        vfirst_ref, vgate_ref, o_ref, s_ref = rest
    else:
        o_ref, s_ref = rest

    @pl.when(pl.program_id(2) == 0)
    def _():
        s_ref[...] = jnp.zeros_like(s_ref)

    def iota(shape, dim):
        return lax.broadcasted_iota(jnp.int32, shape, dim)

    tri = (iota((L, L), 1) <= iota((L, L), 0)).astype(BF16)
    same_head = (iota((W, W), 0) // N) == (iota((W, W), 1) // N)
    seg = same_head.astype(BF16)
    g_row, g_col = iota((2 * L, 2 * W), 0), iota((2 * L, 2 * W), 1)
    mask_g = (g_col % N) < (g_row % L) + (g_row >= L).astype(jnp.int32)
    n_double = max(1, (L - 1).bit_length())

    def stack(x):
        head0 = (iota(x.shape, 1) % W) < N
        zero = jnp.zeros_like(x)
        return jnp.concatenate([jnp.where(head0, x, zero), jnp.where(head0, zero, x)], axis=0)

    def seg_sum(x, passes):
        return _split_dot(x, seg, 0, passes)

    eye2 = ((iota((L, W), 1) % N) == iota((L, W), 0)).astype(F32)
    state = {pr: s_ref[pr] for pr in range(n_pairs)}

    def chunk_stages(c, pr):
        rows, lanes = slice(c * L, (c + 1) * L), slice(pr * W, (pr + 1) * W)
        r, k, v = r_ref[rows, lanes], k_ref[rows, lanes], v_ref[rows, lanes]
        if has_vres:
            v = v + (vfirst_ref[rows, lanes] - v) * vgate_ref[rows, lanes]
        lw, a = lw_ref[rows, lanes], a_ref[rows, lanes]
        kk_p, ka_p, rk_p = kk_ref[:, lanes], ka_ref[:, lanes], rk_ref[:, lanes]
        kkr = k * kk_p
        kh = k * (1.0 + (a - 1.0) * ka_p)
        ssq = seg_sum(kkr * kkr, 2)
        lp = _split_dot(tri, lw, 1, 3)
        bonus_w = seg_sum(r * kh * rk_p, 1)
        yield
        kk = kkr / jnp.maximum(jnp.sqrt(ssq), 1e-12)
        ba = kk * a
        lp_end = lp[L - 1:L, :]
        e_neg = jnp.exp(-lp)
        e_end = jnp.exp(lp_end - lp)
        at = -kk * jnp.exp(lp - lw)
        rt = r * jnp.exp(lp)
        v16 = v.astype(BF16)
        ar = jnp.concatenate([at, rt], axis=0)
        bk = jnp.concatenate([stack((ba * e_neg).astype(BF16)),
                              stack((kh * e_neg).astype(BF16))], axis=0)
        gm = _bdot(ar, bk, _NT)
        bk_end_t = jnp.concatenate([ba * e_end, kh * e_end], axis=0).T.astype(BF16)
        p_col = jnp.broadcast_to(jnp.exp(lp_end), (W, W)).T
        yield
        gm = jnp.where(mask_g, gm, 0.0)
        nk, gak = gm[:L, :W], gm[:L, W:]
        grb, grk = gm[L:, :W], gm[L:, W:]
        sv = stack(v16)
        nk16 = nk.astype(BF16)
        aakv = _bdot(gak, sv)
        npow = _bdot(nk16, stack(nk16))
        yield
        t = eye2 + nk
        for it in range(1, n_double):
            np16, t16 = npow.astype(BF16), t.astype(BF16)
            if it + 1 < n_double:
                p = _bdot(np16, jnp.concatenate([stack(np16), stack(t16)], axis=1))
                yield
                npow, t = p[:, :W], t + p[:, W:]
            else:
                p = _bdot(np16, stack(t16))
                yield
                t = t + p
        w16 = _bdot(t, stack(jnp.concatenate([at, aakv], axis=1).astype(BF16))).astype(BF16)
        yield
        p2 = _bdot(grb, stack(w16))
        p3 = _bdot(grk, sv)
        below = jnp.concatenate([jnp.zeros((L, W), BF16), v16], axis=1)
        mt = _bdot(bk_end_t, jnp.concatenate([w16, below], axis=0))
        yield
        rp = rt + p2[:, :W]
        yv = p2[:, W:] + p3
        zero = jnp.zeros((W, W), F32)
        m_bd_t = jnp.where(same_head, mt[:, :W], zero)
        sv_bd_t = jnp.where(same_head, mt[:, W:], zero)
        sst = state[pr]
        sst16 = sst.astype(BF16)
        ys = _bdot(rp, sst16)
        state[pr] = p_col * sst + _bdot(m_bd_t, sst16) + sv_bd_t
        yield
        y = yv + ys
        mean = seg_sum(y, 1) * (1.0 / N)
        yield
        yc = y - mean
        var = seg_sum(yc * yc, 1) * (1.0 / N)
        yield
        yn = yc * lax.rsqrt(var + GN_EPS) * lnw_ref[:, lanes] + lnb_ref[:, lanes]
        o_ref[rows, lanes] = ((yn + bonus_w * v) * g_ref[rows, lanes]).astype(o_ref.dtype)

    pending = {(c, pr): chunk_stages(c, pr) for c in range(n_chunks) for pr in range(n_pairs)}
    slot = 0
    while pending:
        for key in sorted(pending):
            if key[0] <= slot and next(pending[key], True):
                del pending[key]
        slot += 1
    for pr in range(n_pairs):
        s_ref[pr] = state[pr]


def _rwkv_recurrence(rkv, v_mix, lw, a, g, k_k, k_a, r_k, ln_w, ln_b):
    _, b, t, d = rkv.shape
    tb = _tile(t, RWKV_TIME_BLOCK, CHUNK)
    width = _tile(d, RWKV_PAIRS_PER_STEP * LANES, LANES)
    tok = pl.BlockSpec((None, tb, width), lambda bi, hi, ti: (bi, ti, hi))
    par = pl.BlockSpec((1, width), lambda bi, hi, ti: (0, hi))

    def stacked(c):
        return pl.BlockSpec((None, None, tb, width), lambda bi, hi, ti: (c, bi, ti, hi))

    in_specs = [stacked(0), stacked(1), stacked(2), tok, tok, tok] + [par] * 5
    args = [rkv, rkv, rkv, lw, a, g, *(p.reshape(1, d) for p in (k_k, k_a, r_k, ln_w, ln_b))]
    if v_mix is not None:
        in_specs += [stacked(2), tok]
        args += list(v_mix)
    return pl.pallas_call(
        functools.partial(_rwkv_body, n_chunks=tb // CHUNK, n_pairs=width // LANES,
                          has_vres=v_mix is not None),
        grid=(b, d // width, t // tb),
        in_specs=in_specs,
        out_specs=tok,
        out_shape=jax.ShapeDtypeStruct((b, t, d), BF16),
        scratch_shapes=[pltpu.VMEM((width // LANES, LANES, LANES), F32)],
        compiler_params=_params("parallel", "parallel", "arbitrary"),
        name="rwkv7_recurrence",
    )(*args)


def _rwkv_layer(x, seq, v_first, norm_g, mu, w_rkv, w_o, widx, w0, w_l1, w_l2, a0, a_l1, a_l2,
                g_l1, g_l2, k_k, k_a, r_k, ln_w, ln_b, v_res, next_g):
    m, d = x.shape
    bsz = m // seq
    xmix, lw, a, g, *v_gate = _rwkv_front(x, norm_g, mu, seq, (w0, w_l1, w_l2), (a0, a_l1, a_l2),
                                          (g_l1, g_l2), v_res)
    rkv = _batched_matmul(xmix, w_rkv, widx, 3)
    sh = (bsz, seq, d)
    rkv4 = rkv.reshape((3,) + sh)
    if v_res is None:
        v_first = rkv4
        v_mix = None
    else:
        v_mix = (v_first, v_gate[0].reshape(sh))
    yg = _rwkv_recurrence(rkv4, v_mix, lw.reshape(sh), a.reshape(sh), g.reshape(sh),
                          k_k, k_a, r_k, ln_w, ln_b)
    return _resid_matmul(yg.reshape(m, d), w_o, widx, x, 1.0, next_g=next_g), v_first


S5_GROUPS_PER_STEP = 8


def _s5_body(u_ref, wt_ref, bbt_ref, ca_ref, dsk_ref, vt_ref, c1_ref, c2_ref, o_ref, tt_ref,
             *, n_chunks, state):
    L, C = S5_CHUNK, S5_GROUP
    LC = L * C
    lags_per_tile = LANES // C

    def group_stages(gi):
        u = u_ref[gi]
        taps = lax.dot_general(bbt_ref[gi], ca_ref[gi], _NN, precision=lax.Precision.HIGHEST,
                               preferred_element_type=F32)
        s = jnp.dot(u, wt_ref[gi], preferred_element_type=F32)
        yield
        on_diag = (lax.broadcasted_iota(jnp.int32, taps.shape, 1)
                   == lax.broadcasted_iota(jnp.int32, taps.shape, 0))
        taps = taps + jnp.where(on_diag, dsk_ref[gi], 0.0)
        padded = jnp.concatenate([jnp.zeros_like(taps), taps], axis=1)
        for sub in range(lags_per_tile):
            shifted = padded if sub == 0 else pltpu.roll(padded, sub * C, axis=1)
            shifted = shifted.astype(BF16)
            for q in range(L // lags_per_tile):
                lag = q * lags_per_tile + sub
                tt_ref[gi, lag * C:(lag + 1) * C, :] = (
                    shifted[:, LC - q * LANES:2 * LC - q * LANES])
        yield
        chunk_idx = lax.broadcasted_iota(jnp.int32, s.shape, 0) % n_chunks
        step = 1
        it = 0
        while step < n_chunks:
            sp = jnp.where(chunk_idx >= step, pltpu.roll(s, step, axis=0), 0.0)
            s = (s + c1_ref[gi, it:it + 1, :] * sp
                 + c2_ref[gi, it:it + 1, :] * pltpu.roll(sp, state, axis=1))
            step *= 2
            it += 1
            yield
        s_start = jnp.where(chunk_idx >= 1, pltpu.roll(s, 1, axis=0), 0.0)
        y = (jnp.dot(u, tt_ref[gi], preferred_element_type=F32)
             + jnp.dot(s_start.astype(BF16), vt_ref[gi], preferred_element_type=F32))
        yield
        o_ref[gi] = jax.nn.gelu(y).astype(o_ref.dtype)

    pending = [group_stages(gi) for gi in range(u_ref.shape[0])]
    while pending:
        pending = [gen for gen in pending if not next(gen, True)]


def _s5_operators(lam_re, lam_im, log_step, b_re, b_im, c_re, c_im, d_skip, n_chunks):
    g, p = lam_re.shape
    c = S5_GROUP
    L = S5_CHUNK
    lr = jnp.minimum(lam_re.astype(F32), LAMBDA_RE_MAX)
    li = lam_im.astype(F32)
    dt = jnp.exp(log_step.astype(F32))[:, None]
    ldt_re, ldt_im = lr * dt, li * dt
    mag = jnp.exp(ldt_re)
    ab_re, ab_im = mag * jnp.cos(ldt_im), mag * jnp.sin(ldt_im)
    den = lr * lr + li * li
    q_re = ((ab_re - 1.0) * lr + ab_im * li) / den
    q_im = (ab_im * lr - (ab_re - 1.0) * li) / den
    br, bi = b_re.astype(F32), b_im.astype(F32)
    bb_re = q_re[..., None] * br - q_im[..., None] * bi
    bb_im = q_re[..., None] * bi + q_im[..., None] * br
    cr, ci = c_re.astype(F32), c_im.astype(F32)
    dsk = d_skip.astype(F32).reshape(g, c)

    def power(n):
        m_ = jnp.exp(n * ldt_re)
        return m_ * jnp.cos(n * ldt_im), m_ * jnp.sin(n * ldt_im)

    lag = jnp.arange(0, L + 1, dtype=F32)[:, None, None]
    pw_re, pw_im = power(lag)
    pwt_re, pwt_im = pw_re.transpose(1, 2, 0), pw_im.transpose(1, 2, 0)
    crt, cit = cr.transpose(0, 2, 1), ci.transpose(0, 2, 1)
    ca = jnp.concatenate([
        crt[:, :, None, :] * pwt_re[..., None] - cit[:, :, None, :] * pwt_im[..., None],
        -(crt[:, :, None, :] * pwt_im[..., None] + cit[:, :, None, :] * pwt_re[..., None])],
        axis=1)
    ca_lo = ca[:, :, :L].reshape(g, 2 * p, L * c)
    vt = ca[:, :, 1:].reshape(g, 2 * p, L * c).astype(BF16)
    bbt_re, bbt_im = bb_re.transpose(0, 2, 1), bb_im.transpose(0, 2, 1)
    bbt = jnp.concatenate([bbt_re, bbt_im], axis=-1)
    idx = jnp.arange(L)
    rev_re = pw_re[L - 1 - idx].transpose(1, 0, 2)[:, :, None, :]
    rev_im = pw_im[L - 1 - idx].transpose(1, 0, 2)[:, :, None, :]
    wt = jnp.concatenate([rev_re * bbt_re[:, None] - rev_im * bbt_im[:, None],
                          rev_re * bbt_im[:, None] + rev_im * bbt_re[:, None]],
                         axis=-1).reshape(g, L * c, 2 * p).astype(BF16)
    n_steps = max(1, (n_chunks - 1).bit_length())
    hop = (L * 2.0 ** jnp.arange(n_steps, dtype=F32))[:, None, None]
    hop_re, hop_im = power(hop)
    c1 = jnp.concatenate([hop_re, hop_re], axis=-1).transpose(1, 0, 2)
    c2 = jnp.concatenate([-hop_im, hop_im], axis=-1).transpose(1, 0, 2)
    return wt, bbt, ca_lo, dsk[:, :, None], vt, c1, c2


def _s5_core(h, seq, lam_re, lam_im, log_step, b_re, b_im, c_re, c_im, d_skip):
    m, d = h.shape
    g, p = lam_re.shape
    c, L = S5_GROUP, S5_CHUNK
    n_chunks = seq // L
    rows = m // L
    wt, bbt, ca_lo, dsk, vt, c1, c2 = _s5_operators(lam_re, lam_im, log_step, b_re, b_im, c_re,
                                                    c_im, d_skip, n_chunks)
    n_steps = c1.shape[1]
    u = h.reshape(rows, L, g, c).transpose(2, 0, 1, 3).reshape(g, rows, L * c)
    gb = _tile(g, S5_GROUPS_PER_STEP, 1)
    grp = lambda *shape: pl.BlockSpec((gb,) + shape, lambda gi: (gi, 0, 0))
    y = pl.pallas_call(
        functools.partial(_s5_body, n_chunks=n_chunks, state=p),
        grid=(g // gb,),
        in_specs=[grp(rows, L * c), grp(L * c, 2 * p), grp(c, 2 * p), grp(2 * p, L * c),
                  grp(c, 1), grp(2 * p, L * c), grp(n_steps, 2 * p), grp(n_steps, 2 * p)],
        out_specs=grp(rows, L * c),
        out_shape=jax.ShapeDtypeStruct((g, rows, L * c), BF16),
        scratch_shapes=[pltpu.VMEM((gb, L * c, L * c), BF16)],
        compiler_params=_params("parallel"),
        name="s5_chunk_scan",
    )(u, wt, bbt, ca_lo, dsk, vt, c1, c2)
    return y.reshape(g, rows, L, c).transpose(1, 2, 0, 3).reshape(m, d)


def _ffn(x, pre, w_in, w_out, widx, norm_g, next_g):
    if pre is None:
        act = _dual_matmul(_rmsnorm(x, norm_g, BF16), w_in, widx, "swiglu")
    else:
        act = _dual_matmul(pre[0], w_in, widx, "swiglu", ssq=pre[1])
    return _resid_matmul(act, w_out, widx, x, 0.5, next_g=next_g)


def kernel(x, ffn_norm, ffn_w_in, ffn_w_out, mix_norm, rwkv_mu, rwkv_w_rkv, rwkv_w_o, rwkv_w0, rwkv_w_l1, rwkv_w_l2, rwkv_a0, rwkv_a_l1, rwkv_a_l2, rwkv_v0, rwkv_v_l1, rwkv_v_l2, rwkv_g_l1, rwkv_g_l2, rwkv_k_k, rwkv_k_a, rwkv_r_k, rwkv_ln_w, rwkv_ln_b, s5_lam_re, s5_lam_im, s5_log_step, s5_b_re, s5_b_im, s5_c_re, s5_c_im, s5_d, s5_w_glu, final_norm):
    bsz, seq, d = x.shape
    depth = ffn_norm.shape[0]
    n_mixers = 2
    w_in, w_out = ffn_w_in.astype(BF16), ffn_w_out.astype(BF16)
    w_rkv, w_o, w_glu = rwkv_w_rkv.astype(BF16), rwkv_w_o.astype(BF16), s5_w_glu.astype(BF16)
    x = x.reshape(bsz * seq, d)
    v_first = None
    pre = None
    for i in range(depth):
        x = _ffn(x, pre, w_in, w_out, (i, 0), ffn_norm[i, 0], None)
        j = i // n_mixers
        if i % n_mixers == 0:
            v_res = None if j == 0 else (rwkv_v0[j - 1], rwkv_v_l1[j - 1], rwkv_v_l2[j - 1])
            (x, xg, ssq), v_first = _rwkv_layer(
                x, seq, v_first, mix_norm[i], rwkv_mu[j], w_rkv, w_o, (j,),
                rwkv_w0[j], rwkv_w_l1[j], rwkv_w_l2[j], rwkv_a0[j], rwkv_a_l1[j], rwkv_a_l2[j],
                rwkv_g_l1[j], rwkv_g_l2[j], rwkv_k_k[j], rwkv_k_a[j], rwkv_r_k[j],
                rwkv_ln_w[j], rwkv_ln_b[j], v_res, ffn_norm[i, 1])
        else:
            h = _rmsnorm(x, mix_norm[i], BF16)
            y = _s5_core(h, seq, s5_lam_re[j], s5_lam_im[j], s5_log_step[j], s5_b_re[j],
                         s5_b_im[j], s5_c_re[j], s5_c_im[j], s5_d[j])
            x, xg, ssq = _dual_matmul(y, w_glu, (j,), "glu_residual", res=x,
                                      next_g=ffn_norm[i, 1])
        if i + 1 < depth:
            x, xg, ssq = _ffn(x, (xg, ssq), w_in, w_out, (i, 1), None, ffn_norm[i + 1, 0])
            pre = (xg, ssq)
        else:
            x = _ffn(x, (xg, ssq), w_in, w_out, (i, 1), None, None)
    return _rmsnorm(x, final_norm, F32).reshape(bsz, seq, d)
```

```python
import functools
import math

import jax
import jax.numpy as jnp
from jax import lax
from jax.experimental import pallas as pl
from jax.experimental.pallas import tpu as pltpu

F32 = jnp.float32
BF16 = jnp.bfloat16

RMS_EPS = 1e-6
GN_EPS = 64e-5
LAMBDA_RE_MAX = -1e-4
HEAD = 64
CHUNK = 64
S5_CHUNK = 64
S5_GROUP = 16
LANES = 128
VMEM_LIMIT = 56 * 1024 * 1024

_NN = (((1,), (0,)), ((), ()))
_NT = (((1,), (1,)), ((), ()))
_TN = (((0,), (0,)), ((), ()))


def _params(*sem):
    return pltpu.CompilerParams(dimension_semantics=sem, vmem_limit_bytes=VMEM_LIMIT)


def _tile(n, pref, quantum):
    best = None
    t = quantum
    while t <= min(n, pref):
        if n % t == 0:
            best = t
        t += quantum
    return best if best is not None else n


def _rms_body(x_ref, g_ref, o_ref):
    x = x_ref[...]
    ms = jnp.mean(x * x, axis=-1, keepdims=True)
    o_ref[...] = (x * lax.rsqrt(ms + RMS_EPS) * g_ref[...]).astype(o_ref.dtype)


def _rmsnorm(x, g, out_dtype):
    m, d = x.shape
    bm = _tile(m, 256, 8)
    return pl.pallas_call(
        _rms_body,
        grid=(m // bm,),
        in_specs=[pl.BlockSpec((bm, d), lambda i: (i, 0)),
                  pl.BlockSpec((1, d), lambda i: (0, 0))],
        out_specs=pl.BlockSpec((bm, d), lambda i: (i, 0)),
        out_shape=jax.ShapeDtypeStruct((m, d), out_dtype),
        compiler_params=_params("parallel"),
        name="rmsnorm",
    )(x, g.reshape(1, d))


_DECAY_SCALE = math.exp(-0.5)


def _rwkv_front_body(*refs, tiles_per_seq, has_vres):
    x_ref, g_ref, mu_ref = refs[:3]
    n_lora = 11 if has_vres else 8
    lora = refs[3:3 + n_lora]
    outs = refs[3 + n_lora:-1]
    carry_ref = refs[-1]
    wl1, wl2, w0, al1, al2, a0, gl1, gl2 = lora[:8]
    xmix_ref, lw_ref, a_ref, gate_ref = outs[:4]
    i = pl.program_id(0)

    @pl.when(i % tiles_per_seq == 0)
    def _():
        carry_ref[...] = jnp.zeros_like(carry_ref)

    x = x_ref[...]
    ms = jnp.mean(x * x, axis=-1, keepdims=True)
    h = x * lax.rsqrt(ms + RMS_EPS) * g_ref[...]
    bm = h.shape[0]
    row = lax.broadcasted_iota(jnp.int32, h.shape, 0)
    prev = jnp.where(row == 0, carry_ref[...], pltpu.roll(h, 1, axis=0))
    carry_ref[...] = h[bm - 1:bm, :]
    dx = prev - h

    def mixed(c):
        return (h + dx * mu_ref[c:c + 1, :]).astype(BF16)

    def low_rank(xc, l1_ref, l2_ref, inner):
        t = inner(jnp.dot(xc, l1_ref[...], preferred_element_type=F32))
        return jnp.dot(t.astype(BF16), l2_ref[...], preferred_element_type=F32)

    xmix_ref[0] = mixed(0)
    xmix_ref[1] = mixed(1)
    xv = mixed(2)
    xmix_ref[2] = xv
    lw_ref[...] = -_DECAY_SCALE * jax.nn.sigmoid(low_rank(mixed(3), wl1, wl2, jnp.tanh) + w0[...])
    a_ref[...] = jax.nn.sigmoid(low_rank(mixed(4), al1, al2, lambda t: t) + a0[...])
    gate_ref[...] = low_rank(mixed(5), gl1, gl2, jax.nn.sigmoid)
    if has_vres:
        vl1, vl2, v0 = lora[8:]
        outs[4][...] = jax.nn.sigmoid(low_rank(xv, vl1, vl2, lambda t: t) + v0[...])


def _pad_rank(l1, l2):
    pad = (-l1.shape[1]) % LANES
    return (jnp.pad(l1, ((0, 0), (0, pad))).astype(BF16),
            jnp.pad(l2, ((0, pad), (0, 0))).astype(BF16))


def _rwkv_front(x, g, mu, seq, w_lora, a_lora, g_lora, v_lora):
    m, d = x.shape
    bm = _tile(seq, 128, 8)
    row = pl.BlockSpec((bm, d), lambda i: (i, 0))
    vec = pl.BlockSpec((1, d), lambda i: (0, 0))
    in_specs = [row, vec, pl.BlockSpec(mu.shape, lambda i: (0, 0))]
    args = [x, g.reshape(1, d), mu]

    def add_pair(l1, l2):
        l1, l2 = _pad_rank(l1, l2)
        in_specs.extend([pl.BlockSpec(l1.shape, lambda i: (0, 0)),
                         pl.BlockSpec(l2.shape, lambda i: (0, 0))])
        args.extend([l1, l2])

    def add_bias(b):
        in_specs.append(vec)
        args.append(b.reshape(1, d))

    w0, w_l1, w_l2 = w_lora
    a0, a_l1, a_l2 = a_lora
    add_pair(w_l1, w_l2)
    add_bias(w0)
    add_pair(a_l1, a_l2)
    add_bias(a0)
    add_pair(*g_lora)
    n_tok_out = 3
    if v_lora is not None:
        v0, v_l1, v_l2 = v_lora
        add_pair(v_l1, v_l2)
        add_bias(v0)
        n_tok_out = 4
    return pl.pallas_call(
        functools.partial(_rwkv_front_body, tiles_per_seq=seq // bm, has_vres=v_lora is not None),
        grid=(m // bm,),
        in_specs=in_specs,
        out_specs=[pl.BlockSpec((3, bm, d), lambda i: (0, i, 0))] + [row] * n_tok_out,
        out_shape=[jax.ShapeDtypeStruct((3, m, d), BF16)]
        + [jax.ShapeDtypeStruct((m, d), F32)] * n_tok_out,
        scratch_shapes=[pltpu.VMEM((1, d), F32)],
        compiler_params=_params("arbitrary"),
        name="rwkv_front",
    )(*args)


def _w_spec(widx, k, bn, col):
    return pl.BlockSpec((None,) * len(widx) + (k, bn), lambda *gi: widx + (0, col(*gi)))


def _emit_next_norm_input(x_new, g_ref, xg_ref, ssq_ref):
    xg_ref[...] = (x_new * g_ref[...]).astype(xg_ref.dtype)
    ssq_ref[...] = jnp.sum(x_new * x_new, axis=-1, keepdims=True)


def _next_norm_specs(m, n, bm, bn):
    g_spec = pl.BlockSpec((1, bn), lambda i, j: (0, j))
    out_specs = [pl.BlockSpec((bm, bn), lambda i, j: (i, j)),
                 pl.BlockSpec((None, bm, 1), lambda i, j: (j, i, 0))]
    out_shapes = [jax.ShapeDtypeStruct((m, n), BF16),
                  jax.ShapeDtypeStruct((n // bn, m, 1), F32)]
    return g_spec, out_specs, out_shapes


def _dual_body(*refs, mode, normed, emit_next, k_dim):
    refs = list(refs)
    a_ref, w1_ref, w2_ref = refs[:3]
    del refs[:3]
    a = a_ref[...]
    p1 = jnp.dot(a, w1_ref[...], preferred_element_type=F32)
    p2 = jnp.dot(a, w2_ref[...], preferred_element_type=F32)
    if not normed:
        ssq_ref = refs.pop(0)
        rstd = lax.rsqrt(jnp.sum(ssq_ref[...], axis=0) * (1.0 / k_dim) + RMS_EPS)
        p1, p2 = p1 * rstd, p2 * rstd
    if mode == "swiglu":
        (o_ref,) = refs
        o_ref[...] = (jax.nn.silu(p1) * p2).astype(o_ref.dtype)
        return
    res_ref = refs.pop(0)
    x_new = res_ref[...] + p1 * jax.nn.sigmoid(p2)
    if emit_next:
        g_ref, o_ref, xg_ref, ssq_out_ref = refs
        _emit_next_norm_input(x_new, g_ref, xg_ref, ssq_out_ref)
    else:
        (o_ref,) = refs
    o_ref[...] = x_new


def _dual_matmul(a, w, widx, mode, ssq=None, res=None, next_g=None):
    m, k = a.shape
    n = w.shape[-1] // 2
    bm = _tile(m, 1024, 8)
    bn = _tile(n, 512, LANES)
    nj = n // bn
    in_specs = [pl.BlockSpec((bm, k), lambda i, j: (i, 0)),
                _w_spec(widx, k, bn, lambda i, j: j),
                _w_spec(widx, k, bn, lambda i, j: j + nj)]
    args = [a, w, w]
    if ssq is not None:
        in_specs.append(pl.BlockSpec((ssq.shape[0], bm, 1), lambda i, j: (0, i, 0)))
        args.append(ssq)
    tile = pl.BlockSpec((bm, bn), lambda i, j: (i, j))
    out_specs, out_shapes = [tile], [jax.ShapeDtypeStruct((m, n), BF16 if mode == "swiglu" else F32)]
    if mode == "glu_residual":
        in_specs.append(tile)
        args.append(res)
        if next_g is not None:
            g_spec, extra_specs, extra_shapes = _next_norm_specs(m, n, bm, bn)
            in_specs.append(g_spec)
            args.append(next_g.reshape(1, n))
            out_specs += extra_specs
            out_shapes += extra_shapes
    out = pl.pallas_call(
        functools.partial(_dual_body, mode=mode, normed=ssq is None,
                          emit_next=next_g is not None, k_dim=k),
        grid=(m // bm, nj),
        in_specs=in_specs,
        out_specs=out_specs,
        out_shape=out_shapes,
        compiler_params=_params("parallel", "parallel"),
        name="matmul_" + mode,
    )(*args)
    return out[0] if len(out) == 1 else tuple(out)


def _resid_body(a_ref, w_ref, res_ref, *rest, scale, emit_next):
    acc = jnp.dot(a_ref[...], w_ref[...], preferred_element_type=F32)
    x_new = res_ref[...] + scale * acc
    if emit_next:
        g_ref, o_ref, xg_ref, ssq_ref = rest
        _emit_next_norm_input(x_new, g_ref, xg_ref, ssq_ref)
    else:
        (o_ref,) = rest
    o_ref[...] = x_new


def _resid_matmul(a, w, widx, res, scale, next_g=None):
    m, k = a.shape
    n = w.shape[-1]
    bm = _tile(m, 1024, 8)
    bn = _tile(n, 512, LANES)
    tile = pl.BlockSpec((bm, bn), lambda i, j: (i, j))
    in_specs = [pl.BlockSpec((bm, k), lambda i, j: (i, 0)),
                _w_spec(widx, k, bn, lambda i, j: j), tile]
    args = [a, w, res]
    out_specs, out_shapes = [tile], [jax.ShapeDtypeStruct((m, n), F32)]
    if next_g is not None:
        g_spec, extra_specs, extra_shapes = _next_norm_specs(m, n, bm, bn)
        in_specs.append(g_spec)
        args.append(next_g.reshape(1, n))
        out_specs += extra_specs
        out_shapes += extra_shapes
    out = pl.pallas_call(
        functools.partial(_resid_body, scale=scale, emit_next=next_g is not None),
        grid=(m // bm, n // bn),
        in_specs=in_specs,
        out_specs=out_specs,
        out_shape=out_shapes,
        compiler_params=_params("parallel", "parallel"),
        name="matmul_residual",
    )(*args)
    return out[0] if len(out) == 1 else tuple(out)


def _plain_body(a_ref, w_ref, o_ref):
    o_ref[...] = jnp.dot(a_ref[...], w_ref[...], preferred_element_type=F32)


def _batched_matmul(a, w, widx, n_batch):
    _, m, k = a.shape
    n = w.shape[-1]
    bm = _tile(m, 1024, 8)
    bn = _tile(n, 1024, LANES)
    return pl.pallas_call(
        _plain_body,
        grid=(n_batch, m // bm, n // bn),
        in_specs=[pl.BlockSpec((None, bm, k), lambda c, i, j: (c, i, 0)),
                  pl.BlockSpec((None,) * (len(widx) + 1) + (k, bn),
                               lambda c, i, j: widx + (c, 0, j))],
        out_specs=pl.BlockSpec((None, bm, bn), lambda c, i, j: (c, i, j)),
        out_shape=jax.ShapeDtypeStruct((n_batch, m, n), F32),
        compiler_params=_params("parallel", "parallel", "parallel"),
        name="matmul_rkv",
    )(a, w)


RWKV_TIME_BLOCK = 512
RWKV_PAIRS_PER_STEP = 4


def _bdot(a, b, dims=_NN):
    return lax.dot_general(a.astype(BF16), b.astype(BF16), dims, preferred_element_type=F32)


def _split_dot(a, b, split, passes):
    acc = None
    rem = (a, b)[split]
    for _ in range(passes):
        part = rem.astype(BF16)
        term = (lax.dot_general(part, b, _NN, preferred_element_type=F32) if split == 0 else
                lax.dot_general(a, part, _NN, preferred_element_type=F32))
        acc = term if acc is None else acc + term
        rem = rem - part.astype(F32)
    return acc


def _rwkv_body(r_ref, k_ref, v_ref, lw_ref, a_ref, g_ref, kk_ref, ka_ref, rk_ref, lnw_ref, lnb_ref,
               *rest, n_chunks, n_pairs, has_vres):
    L, N, W = CHUNK, HEAD, LANES
    if has_vres:
        vfirst_ref, vgate_ref, o_ref, s_ref = rest
    else:
        o_ref, s_ref = rest

    @pl.when(pl.program_id(2) == 0)
    def _():
        s_ref[...] = jnp.zeros_like(s_ref)

    def iota(shape, dim):
        return lax.broadcasted_iota(jnp.int32, shape, dim)

    tri = (iota((L, L), 1) <= iota((L, L), 0)).astype(BF16)
    same_head = (iota((W, W), 0) // N) == (iota((W, W), 1) // N)
    seg = same_head.astype(BF16)
    g_row, g_col = iota((2 * L, 2 * W), 0), iota((2 * L, 2 * W), 1)
    mask_g = (g_col % N) < (g_row % L) + (g_row >= L).astype(jnp.int32)
    n_double = max(1, (L - 1).bit_length())

    def stack(x):
        head0 = (iota(x.shape, 1) % W) < N
        zero = jnp.zeros_like(x)
        return jnp.concatenate([jnp.where(head0, x, zero), jnp.where(head0, zero, x)], axis=0)

    def seg_sum(x, passes):
        return _split_dot(x, seg, 0, passes)

    eye2 = ((iota((L, W), 1) % N) == iota((L, W), 0)).astype(F32)
    state = {pr: s_ref[pr] for pr in range(n_pairs)}

    def chunk_stages(c, pr):
        rows, lanes = slice(c * L, (c + 1) * L), slice(pr * W, (pr + 1) * W)
        r, k, v = r_ref[rows, lanes], k_ref[rows, lanes], v_ref[rows, lanes]
        if has_vres:
            v = v + (vfirst_ref[rows, lanes] - v) * vgate_ref[rows, lanes]
        lw, a = lw_ref[rows, lanes], a_ref[rows, lanes]
        kk_p, ka_p, rk_p = kk_ref[:, lanes], ka_ref[:, lanes], rk_ref[:, lanes]
        kkr = k * kk_p
        kh = k * (1.0 + (a - 1.0) * ka_p)
        ssq = seg_sum(kkr * kkr, 2)
        lp = _split_dot(tri, lw, 1, 3)
        bonus_w = seg_sum(r * kh * rk_p, 1)
        yield
        kk = kkr / jnp.maximum(jnp.sqrt(ssq), 1e-12)
        ba = kk * a
        lp_end = lp[L - 1:L, :]
        e_neg = jnp.exp(-lp)
        e_end = jnp.exp(lp_end - lp)
        at = -kk * jnp.exp(lp - lw)
        rt = r * jnp.exp(lp)
        v16 = v.astype(BF16)
        ar = jnp.concatenate([at, rt], axis=0)
        bk = jnp.concatenate([stack((ba * e_neg).astype(BF16)),
                              stack((kh * e_neg).astype(BF16))], axis=0)
        gm = _bdot(ar, bk, _NT)
        bk_end_t = jnp.concatenate([ba * e_end, kh * e_end], axis=0).T.astype(BF16)
        p_col = jnp.broadcast_to(jnp.exp(lp_end), (W, W)).T
        yield
        gm = jnp.where(mask_g, gm, 0.0)
        nk, gak = gm[:L, :W], gm[:L, W:]
        grb, grk = gm[L:, :W], gm[L:, W:]
        sv = stack(v16)
        nk16 = nk.astype(BF16)
        aakv = _bdot(gak, sv)
        npow = _bdot(nk16, stack(nk16))
        yield
        t = eye2 + nk
        for it in range(1, n_double):
            np16, t16 = npow.astype(BF16), t.astype(BF16)
            if it + 1 < n_double:
                p = _bdot(np16, jnp.concatenate([stack(np16), stack(t16)], axis=1))
                yield
                npow, t = p[:, :W], t + p[:, W:]
            else:
                p = _bdot(np16, stack(t16))
                yield
                t = t + p
        w16 = _bdot(t, stack(jnp.concatenate([at, aakv], axis=1).astype(BF16))).astype(BF16)
        yield
        p2 = _bdot(grb, stack(w16))
        p3 = _bdot(grk, sv)
        below = jnp.concatenate([jnp.zeros((L, W), BF16), v16], axis=1)
        mt = _bdot(bk_end_t, jnp.concatenate([w16, below], axis=0))
        yield
        rp = rt + p2[:, :W]
        yv = p2[:, W:] + p3
        zero = jnp.zeros((W, W), F32)
        m_bd_t = jnp.where(same_head, mt[:, :W], zero)
        sv_bd_t = jnp.where(same_head, mt[:, W:], zero)
        sst = state[pr]
        sst16 = sst.astype(BF16)
        ys = _bdot(rp, sst16)
        state[pr] = p_col * sst + _bdot(m_bd_t, sst16) + sv_bd_t
        yield
        y = yv + ys
        mean = seg_sum(y, 1) * (1.0 / N)
        yield
        yc = y - mean
        var = seg_sum(yc * yc, 1) * (1.0 / N)
        yield
        yn = yc * lax.rsqrt(var + GN_EPS) * lnw_ref[:, lanes] + lnb_ref[:, lanes]
        o_ref[rows, lanes] = ((yn + bonus_w * v) * g_ref[rows, lanes]).astype(o_ref.dtype)

    pending = {(c, pr): chunk_stages(c, pr) for c in range(n_chunks) for pr in range(n_pairs)}
    slot = 0
    while pending:
        for key in sorted(pending):
            if key[0] <= slot and next(pending[key], True):
                del pending[key]
        slot += 1
    for pr in range(n_pairs):
        s_ref[pr] = state[pr]


def _rwkv_recurrence(rkv, v_mix, lw, a, g, k_k, k_a, r_k, ln_w, ln_b):
    _, b, t, d = rkv.shape
    tb = _tile(t, RWKV_TIME_BLOCK, CHUNK)
    width = _tile(d, RWKV_PAIRS_PER_STEP * LANES, LANES)
    tok = pl.BlockSpec((None, tb, width), lambda bi, hi, ti: (bi, ti, hi))
    par = pl.BlockSpec((1, width), lambda bi, hi, ti: (0, hi))

    def stacked(c):
        return pl.BlockSpec((None, None, tb, width), lambda bi, hi, ti: (c, bi, ti, hi))

    in_specs = [stacked(0), stacked(1), stacked(2), tok, tok, tok] + [par] * 5
    args = [rkv, rkv, rkv, lw, a, g, *(p.reshape(1, d) for p in (k_k, k_a, r_k, ln_w, ln_b))]
    if v_mix is not None:
        in_specs += [stacked(2), tok]
        args += list(v_mix)
    return pl.pallas_call(
        functools.partial(_rwkv_body, n_chunks=tb // CHUNK, n_pairs=width // LANES,
                          has_vres=v_mix is not None),
        grid=(b, d // width, t // tb),
        in_specs=in_specs,
        out_specs=tok,
        out_shape=jax.ShapeDtypeStruct((b, t, d), BF16),
        scratch_shapes=[pltpu.VMEM((width // LANES, LANES, LANES), F32)],
        compiler_params=_params("parallel", "parallel", "arbitrary"),
        name="rwkv7_recurrence",
    )(*args)


def _rwkv_layer(x, seq, v_first, norm_g, mu, w_rkv, w_o, widx, w0, w_l1, w_l2, a0, a_l1, a_l2,
                g_l1, g_l2, k_k, k_a, r_k, ln_w, ln_b, v_res, next_g):
    m, d = x.shape
    bsz = m // seq
    xmix, lw, a, g, *v_gate = _rwkv_front(x, norm_g, mu, seq, (w0, w_l1, w_l2), (a0, a_l1, a_l2),
                                          (g_l1, g_l2), v_res)
    rkv = _batched_matmul(xmix, w_rkv, widx, 3)
    sh = (bsz, seq, d)
    rkv4 = rkv.reshape((3,) + sh)
    if v_res is None:
        v_first = rkv4
        v_mix = None
    else:
        v_mix = (v_first, v_gate[0].reshape(sh))
    yg = _rwkv_recurrence(rkv4, v_mix, lw.reshape(sh), a.reshape(sh), g.reshape(sh),
                          k_k, k_a, r_k, ln_w, ln_b)
    return _resid_matmul(yg.reshape(m, d), w_o, widx, x, 1.0, next_g=next_g), v_first


S5_GROUPS_PER_STEP = 8


def _s5_body(u_ref, pwlo_ref, pwhi_ref, cc_ref, bbt_ref, rr_ref, ri_ref, dsk_ref, c1_ref, c2_ref,
             o_ref, tt_ref, *, n_chunks, state):
    L, C, P = S5_CHUNK, S5_GROUP, state
    LC = L * C
    lags_per_tile = LANES // C

    def iota(shape, dim):
        return lax.broadcasted_iota(jnp.int32, shape, dim)

    rep_lag = (iota((L, LC), 1) // C == iota((L, LC), 0)).astype(BF16)
    rep_chan = (iota((C, LANES), 1) % C == iota((C, LANES), 0)).astype(BF16)

    def rows_lag(x):
        return jnp.broadcast_to(x[:, None, :], (L, C, x.shape[1])).reshape(LC, x.shape[1])

    def group_stages(gi):
        u = u_ref[gi]
        pw_lo = _split_dot(pwlo_ref[gi], rep_lag, 0, 2)
        pw_hi = _split_dot(pwhi_ref[gi], rep_lag, 0, 1)
        cc = jnp.tile(_split_dot(cc_ref[gi], rep_chan, 0, 2), (1, LC // LANES))
        rr, ri = rows_lag(rr_ref[gi]), rows_lag(ri_ref[gi])
        bb = jnp.broadcast_to(bbt_ref[gi][None], (L, C, 2 * P)).reshape(LC, 2 * P)
        yield

        def c_times_power(pw):
            return jnp.concatenate([cc[:P] * pw[:P] - cc[P:] * pw[P:],
                                    -(cc[:P] * pw[P:] + cc[P:] * pw[:P])], axis=0)

        vt = c_times_power(pw_hi).astype(BF16)
        wt = (rr * bb + ri * pltpu.roll(bb, P, axis=1)).astype(BF16)
        taps = lax.dot_general(bbt_ref[gi], c_times_power(pw_lo), _NN,
                               precision=lax.Precision.HIGHEST,
                               preferred_element_type=F32)
        s = jnp.dot(u, wt, preferred_element_type=F32)
        yield
        on_diag = iota(taps.shape, 1) == iota(taps.shape, 0)
        taps = taps + jnp.where(on_diag, dsk_ref[gi], 0.0)
        padded = jnp.concatenate([jnp.zeros_like(taps), taps], axis=1)
        for sub in range(lags_per_tile):
            shifted = padded if sub == 0 else pltpu.roll(padded, sub * C, axis=1)
            shifted = shifted.astype(BF16)
            for q in range(L // lags_per_tile):
                lag = q * lags_per_tile + sub
                tt_ref[gi, lag * C:(lag + 1) * C, :] = (
                    shifted[:, LC - q * LANES:2 * LC - q * LANES])
        yield
        chunk_idx = iota(s.shape, 0) % n_chunks
        step = 1
        it = 0
        while step < n_chunks:
            sp = jnp.where(chunk_idx >= step, pltpu.roll(s, step, axis=0), 0.0)
            s = (s + c1_ref[gi, it:it + 1, :] * sp
                 + c2_ref[gi, it:it + 1, :] * pltpu.roll(sp, P, axis=1))
            step *= 2
            it += 1
            yield
        s_start = jnp.where(chunk_idx >= 1, pltpu.roll(s, 1, axis=0), 0.0)
        y = (jnp.dot(u, tt_ref[gi], preferred_element_type=F32)
             + jnp.dot(s_start.astype(BF16), vt, preferred_element_type=F32))
        yield
        o_ref[gi] = jax.nn.gelu(y).astype(o_ref.dtype)

    pending = [group_stages(gi) for gi in range(u_ref.shape[0])]
    while pending:
        pending = [gen for gen in pending if not next(gen, True)]


def _s5_tables(lam_re, lam_im, log_step, b_re, b_im, c_re, c_im, d_skip, n_chunks):
    g, p = lam_re.shape
    c = S5_GROUP
    L = S5_CHUNK
    lr = jnp.minimum(lam_re.astype(F32), LAMBDA_RE_MAX)
    li = lam_im.astype(F32)
    dt = jnp.exp(log_step.astype(F32))[:, None]
    ldt_re, ldt_im = lr * dt, li * dt
    mag = jnp.exp(ldt_re)
    ab_re, ab_im = mag * jnp.cos(ldt_im), mag * jnp.sin(ldt_im)
    den = lr * lr + li * li
    q_re = ((ab_re - 1.0) * lr + ab_im * li) / den
    q_im = (ab_im * lr - (ab_re - 1.0) * li) / den
    br, bi = b_re.astype(F32), b_im.astype(F32)
    bb_re = q_re[..., None] * br - q_im[..., None] * bi
    bb_im = q_re[..., None] * bi + q_im[..., None] * br
    dsk = d_skip.astype(F32).reshape(g, c)

    def power(n):
        m_ = jnp.exp(n * ldt_re)
        return m_ * jnp.cos(n * ldt_im), m_ * jnp.sin(n * ldt_im)

    lag = jnp.arange(0, L + 1, dtype=F32)[:, None, None]
    pw_re, pw_im = power(lag)
    pwt = jnp.concatenate([pw_re, pw_im], axis=-1).transpose(1, 2, 0)
    cc = jnp.concatenate([c_re.astype(F32), c_im.astype(F32)], axis=-1).transpose(0, 2, 1)
    bbt = jnp.concatenate([bb_re, bb_im], axis=1).transpose(0, 2, 1)
    rev_re = pw_re[::-1][1:].transpose(1, 0, 2)
    rev_im = pw_im[::-1][1:].transpose(1, 0, 2)
    rr = jnp.concatenate([rev_re, rev_re], axis=-1)
    ri = jnp.concatenate([-rev_im, rev_im], axis=-1)
    n_steps = max(1, (n_chunks - 1).bit_length())
    hop = (L * 2.0 ** jnp.arange(n_steps, dtype=F32))[:, None, None]
    hop_re, hop_im = power(hop)
    c1 = jnp.concatenate([hop_re, hop_re], axis=-1).transpose(1, 0, 2)
    c2 = jnp.concatenate([-hop_im, hop_im], axis=-1).transpose(1, 0, 2)
    return pwt[:, :, :L], pwt[:, :, 1:], cc, bbt, rr, ri, dsk[:, :, None], c1, c2


def _s5_core(h, seq, lam_re, lam_im, log_step, b_re, b_im, c_re, c_im, d_skip):
    m, d = h.shape
    g, p = lam_re.shape
    c, L = S5_GROUP, S5_CHUNK
    n_chunks = seq // L
    rows = m // L
    tables = _s5_tables(lam_re, lam_im, log_step, b_re, b_im, c_re, c_im, d_skip, n_chunks)
    u = h.reshape(rows, L, g, c).transpose(2, 0, 1, 3).reshape(g, rows, L * c)
    gb = _tile(g, S5_GROUPS_PER_STEP, 1)

    def grp(shape):
        return pl.BlockSpec((gb,) + tuple(shape), lambda gi: (gi, 0, 0))

    y = pl.pallas_call(
        functools.partial(_s5_body, n_chunks=n_chunks, state=p),
        grid=(g // gb,),
        in_specs=[grp((rows, L * c))] + [grp(t.shape[1:]) for t in tables],
        out_specs=grp((rows, L * c)),
        out_shape=jax.ShapeDtypeStruct((g, rows, L * c), BF16),
        scratch_shapes=[pltpu.VMEM((gb, L * c, L * c), BF16)],
        compiler_params=_params("parallel"),
        name="s5_chunk_scan",
    )(u, *tables)
    return y.reshape(g, rows, L, c).transpose(1, 2, 0, 3).reshape(m, d)


def _ffn(x, pre, w_in, w_out, widx, norm_g, next_g):
    if pre is None:
        act = _dual_matmul(_rmsnorm(x, norm_g, BF16), w_in, widx, "swiglu")
    else:
        act = _dual_matmul(pre[0], w_in, widx, "swiglu", ssq=pre[1])
    return _resid_matmul(act, w_out, widx, x, 0.5, next_g=next_g)


def kernel(x, ffn_norm, ffn_w_in, ffn_w_out, mix_norm, rwkv_mu, rwkv_w_rkv, rwkv_w_o, rwkv_w0, rwkv_w_l1, rwkv_w_l2, rwkv_a0, rwkv_a_l1, rwkv_a_l2, rwkv_v0, rwkv_v_l1, rwkv_v_l2, rwkv_g_l1, rwkv_g_l2, rwkv_k_k, rwkv_k_a, rwkv_r_k, rwkv_ln_w, rwkv_ln_b, s5_lam_re, s5_lam_im, s5_log_step, s5_b_re, s5_b_im, s5_c_re, s5_c_im, s5_d, s5_w_glu, final_norm):
    bsz, seq, d = x.shape
    depth = ffn_norm.shape[0]
    n_mixers = 2
    w_in, w_out = ffn_w_in.astype(BF16), ffn_w_out.astype(BF16)
    w_rkv, w_o, w_glu = rwkv_w_rkv.astype(BF16), rwkv_w_o.astype(BF16), s5_w_glu.astype(BF16)
    x = x.reshape(bsz * seq, d)
    v_first = None
    pre = None
    for i in range(depth):
        x = _ffn(x, pre, w_in, w_out, (i, 0), ffn_norm[i, 0], None)
        j = i // n_mixers
        if i % n_mixers == 0:
            v_res = None if j == 0 else (rwkv_v0[j - 1], rwkv_v_l1[j - 1], rwkv_v_l2[j - 1])
            (x, xg, ssq), v_first = _rwkv_layer(
                x, seq, v_first, mix_norm[i], rwkv_mu[j], w_rkv, w_o, (j,),
                rwkv_w0[j], rwkv_w_l1[j], rwkv_w_l2[j], rwkv_a0[j], rwkv_a_l1[j], rwkv_a_l2[j],
                rwkv_g_l1[j], rwkv_g_l2[j], rwkv_k_k[j], rwkv_k_a[j], rwkv_r_k[j],
                rwkv_ln_w[j], rwkv_ln_b[j], v_res, ffn_norm[i, 1])
        else:
            h = _rmsnorm(x, mix_norm[i], BF16)
            y = _s5_core(h, seq, s5_lam_re[j], s5_lam_im[j], s5_log_step[j], s5_b_re[j],
                         s5_b_im[j], s5_c_re[j], s5_c_im[j], s5_d[j])
            x, xg, ssq = _dual_matmul(y, w_glu, (j,), "glu_residual", res=x,
                                      next_g=ffn_norm[i, 1])
        if i + 1 < depth:
            x, xg, ssq = _ffn(x, (xg, ssq), w_in, w_out, (i, 1), None, ffn_norm[i + 1, 0])
            pre = (xg, ssq)
        else:
            x = _ffn(x, (xg, ssq), w_in, w_out, (i, 1), None, None)
    return _rmsnorm(x, final_norm, F32).reshape(bsz, seq, d)
```

```python
import functools
import math

import jax
import jax.numpy as jnp
from jax import lax
from jax.experimental import pallas as pl
from jax.experimental.pallas import tpu as pltpu

F32 = jnp.float32
BF16 = jnp.bfloat16

RMS_EPS = 1e-6
GN_EPS = 64e-5
LAMBDA_RE_MAX = -1e-4
HEAD = 64
CHUNK = 64
S5_CHUNK = 64
S5_GROUP = 16
LANES = 128
VMEM_LIMIT = 56 * 1024 * 1024

_NN = (((1,), (0,)), ((), ()))
_NT = (((1,), (1,)), ((), ()))
_TN = (((0,), (0,)), ((), ()))


def _params(*sem):
    return pltpu.CompilerParams(dimension_semantics=sem, vmem_limit_bytes=VMEM_LIMIT)


def _tile(n, pref, quantum):
    best = None
    t = quantum
    while t <= min(n, pref):
        if n % t == 0:
            best = t
        t += quantum
    return best if best is not None else n


def _rms_body(x_ref, g_ref, o_ref):
    x = x_ref[...]
    ms = jnp.mean(x * x, axis=-1, keepdims=True)
    o_ref[...] = (x * lax.rsqrt(ms + RMS_EPS) * g_ref[...]).astype(o_ref.dtype)


def _rmsnorm(x, g, out_dtype):
    m, d = x.shape
    bm = _tile(m, 256, 8)
    return pl.pallas_call(
        _rms_body,
        grid=(m // bm,),
        in_specs=[pl.BlockSpec((bm, d), lambda i: (i, 0)),
                  pl.BlockSpec((1, d), lambda i: (0, 0))],
        out_specs=pl.BlockSpec((bm, d), lambda i: (i, 0)),
        out_shape=jax.ShapeDtypeStruct((m, d), out_dtype),
        compiler_params=_params("parallel"),
        name="rmsnorm",
    )(x, g.reshape(1, d))


_DECAY_SCALE = math.exp(-0.5)


def _rwkv_front_body(*refs, tiles_per_seq, has_vres):
    x_ref, g_ref, mu_ref = refs[:3]
    n_lora = 11 if has_vres else 8
    lora = refs[3:3 + n_lora]
    outs = refs[3 + n_lora:-1]
    carry_ref = refs[-1]
    wl1, wl2, w0, al1, al2, a0, gl1, gl2 = lora[:8]
    xmix_ref, lw_ref, a_ref, gate_ref = outs[:4]
    i = pl.program_id(0)

    @pl.when(i % tiles_per_seq == 0)
    def _():
        carry_ref[...] = jnp.zeros_like(carry_ref)

    x = x_ref[...]
    ms = jnp.mean(x * x, axis=-1, keepdims=True)
    h = x * lax.rsqrt(ms + RMS_EPS) * g_ref[...]
    bm = h.shape[0]
    row = lax.broadcasted_iota(jnp.int32, h.shape, 0)
    prev = jnp.where(row == 0, carry_ref[...], pltpu.roll(h, 1, axis=0))
    carry_ref[...] = h[bm - 1:bm, :]
    dx = prev - h

    def mixed(c):
        return (h + dx * mu_ref[c:c + 1, :]).astype(BF16)

    def low_rank(xc, l1_ref, l2_ref, inner):
        t = inner(jnp.dot(xc, l1_ref[...], preferred_element_type=F32))
        return jnp.dot(t.astype(BF16), l2_ref[...], preferred_element_type=F32)

    xmix_ref[0] = mixed(0)
    xmix_ref[1] = mixed(1)
    xv = mixed(2)
    xmix_ref[2] = xv
    lw_ref[...] = -_DECAY_SCALE * jax.nn.sigmoid(low_rank(mixed(3), wl1, wl2, jnp.tanh) + w0[...])
    a_ref[...] = jax.nn.sigmoid(low_rank(mixed(4), al1, al2, lambda t: t)
                                + a0[...]).astype(a_ref.dtype)
    gate_ref[...] = low_rank(mixed(5), gl1, gl2, jax.nn.sigmoid).astype(gate_ref.dtype)
    if has_vres:
        vl1, vl2, v0 = lora[8:]
        outs[4][...] = jax.nn.sigmoid(low_rank(xv, vl1, vl2, lambda t: t)
                                      + v0[...]).astype(outs[4].dtype)


def _pad_rank(l1, l2):
    pad = (-l1.shape[1]) % LANES
    return (jnp.pad(l1, ((0, 0), (0, pad))).astype(BF16),
            jnp.pad(l2, ((0, pad), (0, 0))).astype(BF16))


def _rwkv_front(x, g, mu, seq, w_lora, a_lora, g_lora, v_lora):
    m, d = x.shape
    bm = _tile(seq, 128, 8)
    row = pl.BlockSpec((bm, d), lambda i: (i, 0))
    vec = pl.BlockSpec((1, d), lambda i: (0, 0))
    in_specs = [row, vec, pl.BlockSpec(mu.shape, lambda i: (0, 0))]
    args = [x, g.reshape(1, d), mu]

    def add_pair(l1, l2):
        l1, l2 = _pad_rank(l1, l2)
        in_specs.extend([pl.BlockSpec(l1.shape, lambda i: (0, 0)),
                         pl.BlockSpec(l2.shape, lambda i: (0, 0))])
        args.extend([l1, l2])

    def add_bias(b):
        in_specs.append(vec)
        args.append(b.reshape(1, d))

    w0, w_l1, w_l2 = w_lora
    a0, a_l1, a_l2 = a_lora
    add_pair(w_l1, w_l2)
    add_bias(w0)
    add_pair(a_l1, a_l2)
    add_bias(a0)
    add_pair(*g_lora)
    n_tok_out = 3
    if v_lora is not None:
        v0, v_l1, v_l2 = v_lora
        add_pair(v_l1, v_l2)
        add_bias(v0)
        n_tok_out = 4
    return pl.pallas_call(
        functools.partial(_rwkv_front_body, tiles_per_seq=seq // bm, has_vres=v_lora is not None),
        grid=(m // bm,),
        in_specs=in_specs,
        out_specs=[pl.BlockSpec((3, bm, d), lambda i: (0, i, 0))] + [row] * n_tok_out,
        out_shape=[jax.ShapeDtypeStruct((3, m, d), BF16)]
        + [jax.ShapeDtypeStruct((m, d), F32)]
        + [jax.ShapeDtypeStruct((m, d), BF16)] * (n_tok_out - 1),
        scratch_shapes=[pltpu.VMEM((1, d), F32)],
        compiler_params=_params("arbitrary"),
        name="rwkv_front",
    )(*args)


def _w_spec(widx, k, bn, col):
    return pl.BlockSpec((None,) * len(widx) + (k, bn), lambda *gi: widx + (0, col(*gi)))


def _emit_next_norm_input(x_new, g_ref, xg_ref, ssq_ref):
    xg_ref[...] = (x_new * g_ref[...]).astype(xg_ref.dtype)
    ssq_ref[...] = jnp.sum(x_new * x_new, axis=-1, keepdims=True)


def _next_norm_specs(m, n, bm, bn):
    g_spec = pl.BlockSpec((1, bn), lambda i, j: (0, j))
    out_specs = [pl.BlockSpec((bm, bn), lambda i, j: (i, j)),
                 pl.BlockSpec((None, bm, 1), lambda i, j: (j, i, 0))]
    out_shapes = [jax.ShapeDtypeStruct((m, n), BF16),
                  jax.ShapeDtypeStruct((n // bn, m, 1), F32)]
    return g_spec, out_specs, out_shapes


def _dual_body(*refs, mode, normed, emit_next, k_dim):
    refs = list(refs)
    a_ref, w1_ref, w2_ref = refs[:3]
    del refs[:3]
    a = a_ref[...]
    p1 = jnp.dot(a, w1_ref[...], preferred_element_type=F32)
    p2 = jnp.dot(a, w2_ref[...], preferred_element_type=F32)
    if not normed:
        ssq_ref = refs.pop(0)
        rstd = lax.rsqrt(jnp.sum(ssq_ref[...], axis=0) * (1.0 / k_dim) + RMS_EPS)
        p1, p2 = p1 * rstd, p2 * rstd
    if mode == "swiglu":
        (o_ref,) = refs
        o_ref[...] = (jax.nn.silu(p1) * p2).astype(o_ref.dtype)
        return
    res_ref = refs.pop(0)
    x_new = res_ref[...] + p1 * jax.nn.sigmoid(p2)
    if emit_next:
        g_ref, o_ref, xg_ref, ssq_out_ref = refs
        _emit_next_norm_input(x_new, g_ref, xg_ref, ssq_out_ref)
    else:
        (o_ref,) = refs
    o_ref[...] = x_new


def _dual_matmul(a, w, widx, mode, ssq=None, res=None, next_g=None):
    m, k = a.shape
    n = w.shape[-1] // 2
    bm = _tile(m, 1024, 8)
    bn = _tile(n, 512, LANES)
    nj = n // bn
    in_specs = [pl.BlockSpec((bm, k), lambda i, j: (i, 0)),
                _w_spec(widx, k, bn, lambda i, j: j),
                _w_spec(widx, k, bn, lambda i, j: j + nj)]
    args = [a, w, w]
    if ssq is not None:
        in_specs.append(pl.BlockSpec((ssq.shape[0], bm, 1), lambda i, j: (0, i, 0)))
        args.append(ssq)
    tile = pl.BlockSpec((bm, bn), lambda i, j: (i, j))
    out_specs, out_shapes = [tile], [jax.ShapeDtypeStruct((m, n), BF16 if mode == "swiglu" else F32)]
    if mode == "glu_residual":
        in_specs.append(tile)
        args.append(res)
        if next_g is not None:
            g_spec, extra_specs, extra_shapes = _next_norm_specs(m, n, bm, bn)
            in_specs.append(g_spec)
            args.append(next_g.reshape(1, n))
            out_specs += extra_specs
            out_shapes += extra_shapes
    out = pl.pallas_call(
        functools.partial(_dual_body, mode=mode, normed=ssq is None,
                          emit_next=next_g is not None, k_dim=k),
        grid=(m // bm, nj),
        in_specs=in_specs,
        out_specs=out_specs,
        out_shape=out_shapes,
        compiler_params=_params("parallel", "parallel"),
        name="matmul_" + mode,
    )(*args)
    return out[0] if len(out) == 1 else tuple(out)


def _resid_body(a_ref, w_ref, res_ref, *rest, scale, emit_next):
    acc = jnp.dot(a_ref[...], w_ref[...], preferred_element_type=F32)
    x_new = res_ref[...] + scale * acc
    if emit_next:
        g_ref, o_ref, xg_ref, ssq_ref = rest
        _emit_next_norm_input(x_new, g_ref, xg_ref, ssq_ref)
    else:
        (o_ref,) = rest
    o_ref[...] = x_new


def _resid_matmul(a, w, widx, res, scale, next_g=None):
    m, k = a.shape
    n = w.shape[-1]
    bm = _tile(m, 1024, 8)
    bn = _tile(n, 512, LANES)
    tile = pl.BlockSpec((bm, bn), lambda i, j: (i, j))
    in_specs = [pl.BlockSpec((bm, k), lambda i, j: (i, 0)),
                _w_spec(widx, k, bn, lambda i, j: j), tile]
    args = [a, w, res]
    out_specs, out_shapes = [tile], [jax.ShapeDtypeStruct((m, n), F32)]
    if next_g is not None:
        g_spec, extra_specs, extra_shapes = _next_norm_specs(m, n, bm, bn)
        in_specs.append(g_spec)
        args.append(next_g.reshape(1, n))
        out_specs += extra_specs
        out_shapes += extra_shapes
    out = pl.pallas_call(
        functools.partial(_resid_body, scale=scale, emit_next=next_g is not None),
        grid=(m // bm, n // bn),
        in_specs=in_specs,
        out_specs=out_specs,
        out_shape=out_shapes,
        compiler_params=_params("parallel", "parallel"),
        name="matmul_residual",
    )(*args)
    return out[0] if len(out) == 1 else tuple(out)


def _plain_body(a_ref, w_ref, o_ref):
    o_ref[...] = jnp.dot(a_ref[...], w_ref[...], preferred_element_type=F32)


def _batched_matmul(a, w, widx, n_batch):
    _, m, k = a.shape
    n = w.shape[-1]
    bm = _tile(m, 1024, 8)
    bn = _tile(n, 1024, LANES)
    return pl.pallas_call(
        _plain_body,
        grid=(n_batch, m // bm, n // bn),
        in_specs=[pl.BlockSpec((None, bm, k), lambda c, i, j: (c, i, 0)),
                  pl.BlockSpec((None,) * (len(widx) + 1) + (k, bn),
                               lambda c, i, j: widx + (c, 0, j))],
        out_specs=pl.BlockSpec((None, bm, bn), lambda c, i, j: (c, i, j)),
        out_shape=jax.ShapeDtypeStruct((n_batch, m, n), F32),
        compiler_params=_params("parallel", "parallel", "parallel"),
        name="matmul_rkv",
    )(a, w)


RWKV_TIME_BLOCK = 512
RWKV_PAIRS_PER_STEP = 4


def _bdot(a, b, dims=_NN):
    return lax.dot_general(a.astype(BF16), b.astype(BF16), dims, preferred_element_type=F32)


def _split_dot(a, b, split, passes):
    acc = None
    rem = (a, b)[split]
    for _ in range(passes):
        part = rem.astype(BF16)
        term = (lax.dot_general(part, b, _NN, preferred_element_type=F32) if split == 0 else
                lax.dot_general(a, part, _NN, preferred_element_type=F32))
        acc = term if acc is None else acc + term
        rem = rem - part.astype(F32)
    return acc


def _rwkv_body(r_ref, k_ref, v_ref, lw_ref, a_ref, g_ref, kk_ref, ka_ref, rk_ref, lnw_ref, lnb_ref,
               *rest, n_chunks, n_pairs, has_vres):
    L, N, W = CHUNK, HEAD, LANES
    if has_vres:
        vfirst_ref, vgate_ref, o_ref, s_ref = rest
    else:
        o_ref, s_ref = rest

    @pl.when(pl.program_id(2) == 0)
    def _():
        s_ref[...] = jnp.zeros_like(s_ref)

    def iota(shape, dim):
        return lax.broadcasted_iota(jnp.int32, shape, dim)

    same_head = (iota((W, W), 0) // N) == (iota((W, W), 1) // N)
    g_row, g_col = iota((2 * L, 2 * W), 0), iota((2 * L, 2 * W), 1)
    mask_g = (g_col % N) < (g_row % L) + (g_row >= L).astype(jnp.int32)
    n_double = max(1, (L - 1).bit_length())

    def stack(x):
        head0 = (iota(x.shape, 1) % W) < N
        zero = jnp.zeros_like(x)
        return jnp.concatenate([jnp.where(head0, x, zero), jnp.where(head0, zero, x)], axis=0)

    head0_lanes = iota((L, W), 1) < N
    step_idx = iota((L, W), 0)

    def seg_sum(x):
        s0 = jnp.sum(jnp.where(head0_lanes, x, 0.0), axis=-1, keepdims=True)
        s1 = jnp.sum(jnp.where(head0_lanes, 0.0, x), axis=-1, keepdims=True)
        return jnp.where(head0_lanes, s0, s1)

    def running_sum(x):
        shift = 1
        while shift < L:
            x = x + jnp.where(step_idx >= shift, pltpu.roll(x, shift, axis=0), 0.0)
            shift *= 2
        return x

    eye2 = ((iota((L, W), 1) % N) == iota((L, W), 0)).astype(F32)
    state = {pr: s_ref[pr] for pr in range(n_pairs)}

    def chunk_stages(c, pr):
        rows, lanes = slice(c * L, (c + 1) * L), slice(pr * W, (pr + 1) * W)
        r, k, v = r_ref[rows, lanes], k_ref[rows, lanes], v_ref[rows, lanes]
        if has_vres:
            v = v + (vfirst_ref[rows, lanes] - v) * vgate_ref[rows, lanes].astype(F32)
        lw, a = lw_ref[rows, lanes], a_ref[rows, lanes].astype(F32)
        kk_p, ka_p, rk_p = kk_ref[:, lanes], ka_ref[:, lanes], rk_ref[:, lanes]
        kkr = k * kk_p
        kh = k * (1.0 + (a - 1.0) * ka_p)
        ssq = seg_sum(kkr * kkr)
        lp = running_sum(lw)
        bonus_w = seg_sum(r * kh * rk_p)
        yield
        kk = kkr / jnp.maximum(jnp.sqrt(ssq), 1e-12)
        ba = kk * a
        lp_end = lp[L - 1:L, :]
        e_neg = jnp.exp(-lp)
        e_end = jnp.exp(lp_end - lp)
        at = -kk * jnp.exp(lp - lw)
        rt = r * jnp.exp(lp)
        v16 = v.astype(BF16)
        ar = jnp.concatenate([at, rt], axis=0)
        bk = jnp.concatenate([stack((ba * e_neg).astype(BF16)),
                              stack((kh * e_neg).astype(BF16))], axis=0)
        gm = _bdot(ar, bk, _NT)
        bk_end_t = jnp.concatenate([ba * e_end, kh * e_end], axis=0).T.astype(BF16)
        p_col = jnp.broadcast_to(jnp.exp(lp_end), (W, W)).T
        yield
        gm = jnp.where(mask_g, gm, 0.0)
        nk, gak = gm[:L, :W], gm[:L, W:]
        grb, grk = gm[L:, :W], gm[L:, W:]
        sv = stack(v16)
        nk16 = nk.astype(BF16)
        aakv = _bdot(gak, sv)
        npow = _bdot(nk16, stack(nk16))
        yield
        t = eye2 + nk
        for it in range(1, n_double):
            np16, t16 = npow.astype(BF16), t.astype(BF16)
            if it + 1 < n_double:
                p = _bdot(np16, jnp.concatenate([stack(np16), stack(t16)], axis=1))
                yield
                npow, t = p[:, :W], t + p[:, W:]
            else:
                p = _bdot(np16, stack(t16))
                yield
                t = t + p
        w16 = _bdot(t, stack(jnp.concatenate([at, aakv], axis=1).astype(BF16))).astype(BF16)
        yield
        p2 = _bdot(grb, stack(w16))
        p3 = _bdot(grk, sv)
        below = jnp.concatenate([jnp.zeros((L, W), BF16), v16], axis=1)
        mt = _bdot(bk_end_t, jnp.concatenate([w16, below], axis=0))
        yield
        rp = rt + p2[:, :W]
        yv = p2[:, W:] + p3
        zero = jnp.zeros((W, W), F32)
        m_bd_t = jnp.where(same_head, mt[:, :W], zero)
        sv_bd_t = jnp.where(same_head, mt[:, W:], zero)
        sst = state[pr]
        sst16 = sst.astype(BF16)
        ys = _bdot(rp, sst16)
        state[pr] = p_col * sst + _bdot(m_bd_t, sst16) + sv_bd_t
        yield
        y = yv + ys
        mean = seg_sum(y) * (1.0 / N)
        yield
        yc = y - mean
        var = seg_sum(yc * yc) * (1.0 / N)
        yield
        yn = yc * lax.rsqrt(var + GN_EPS) * lnw_ref[:, lanes] + lnb_ref[:, lanes]
        gate = g_ref[rows, lanes].astype(F32)
        o_ref[rows, lanes] = ((yn + bonus_w * v) * gate).astype(o_ref.dtype)

    pending = {(c, pr): chunk_stages(c, pr) for c in range(n_chunks) for pr in range(n_pairs)}
    slot = 0
    while pending:
        for key in sorted(pending):
            if key[0] <= slot and next(pending[key], True):
                del pending[key]
        slot += 1
    for pr in range(n_pairs):
        s_ref[pr] = state[pr]


def _rwkv_recurrence(rkv, v_mix, lw, a, g, k_k, k_a, r_k, ln_w, ln_b):
    _, b, t, d = rkv.shape
    tb = _tile(t, RWKV_TIME_BLOCK, CHUNK)
    width = _tile(d, RWKV_PAIRS_PER_STEP * LANES, LANES)
    tok = pl.BlockSpec((None, tb, width), lambda bi, hi, ti: (bi, ti, hi))
    par = pl.BlockSpec((1, width), lambda bi, hi, ti: (0, hi))

    def stacked(c):
        return pl.BlockSpec((None, None, tb, width), lambda bi, hi, ti: (c, bi, ti, hi))

    in_specs = [stacked(0), stacked(1), stacked(2), tok, tok, tok] + [par] * 5
    args = [rkv, rkv, rkv, lw, a, g, *(p.reshape(1, d) for p in (k_k, k_a, r_k, ln_w, ln_b))]
    if v_mix is not None:
        in_specs += [stacked(2), tok]
        args += list(v_mix)
    return pl.pallas_call(
        functools.partial(_rwkv_body, n_chunks=tb // CHUNK, n_pairs=width // LANES,
                          has_vres=v_mix is not None),
        grid=(b, d // width, t // tb),
        in_specs=in_specs,
        out_specs=tok,
        out_shape=jax.ShapeDtypeStruct((b, t, d), BF16),
        scratch_shapes=[pltpu.VMEM((width // LANES, LANES, LANES), F32)],
        compiler_params=_params("parallel", "parallel", "arbitrary"),
        name="rwkv7_recurrence",
    )(*args)


def _rwkv_layer(x, seq, v_first, norm_g, mu, w_rkv, w_o, widx, w0, w_l1, w_l2, a0, a_l1, a_l2,
                g_l1, g_l2, k_k, k_a, r_k, ln_w, ln_b, v_res, next_g):
    m, d = x.shape
    bsz = m // seq
    xmix, lw, a, g, *v_gate = _rwkv_front(x, norm_g, mu, seq, (w0, w_l1, w_l2), (a0, a_l1, a_l2),
                                          (g_l1, g_l2), v_res)
    rkv = _batched_matmul(xmix, w_rkv, widx, 3)
    sh = (bsz, seq, d)
    rkv4 = rkv.reshape((3,) + sh)
    if v_res is None:
        v_first = rkv4
        v_mix = None
    else:
        v_mix = (v_first, v_gate[0].reshape(sh))
    yg = _rwkv_recurrence(rkv4, v_mix, lw.reshape(sh), a.reshape(sh), g.reshape(sh),
                          k_k, k_a, r_k, ln_w, ln_b)
    return _resid_matmul(yg.reshape(m, d), w_o, widx, x, 1.0, next_g=next_g), v_first


S5_GROUPS_PER_STEP = 8


def _s5_body(u_ref, pwlo_ref, pwhi_ref, cc_ref, bbt_ref, rr_ref, ri_ref, dsk_ref, c1_ref, c2_ref,
             o_ref, tt_ref, *, n_chunks, state):
    L, C, P = S5_CHUNK, S5_GROUP, state
    LC = L * C
    lags_per_tile = LANES // C

    def iota(shape, dim):
        return lax.broadcasted_iota(jnp.int32, shape, dim)

    rep_lag = (iota((L, LC), 1) // C == iota((L, LC), 0)).astype(BF16)
    rep_chan = (iota((C, LANES), 1) % C == iota((C, LANES), 0)).astype(BF16)

    def rows_lag(x):
        return jnp.broadcast_to(x[:, None, :], (L, C, x.shape[1])).reshape(LC, x.shape[1])

    def group_stages(gi):
        u = u_ref[gi]
        pw_lo = _split_dot(pwlo_ref[gi], rep_lag, 0, 2)
        pw_hi = _split_dot(pwhi_ref[gi], rep_lag, 0, 1)
        cc = jnp.tile(_split_dot(cc_ref[gi], rep_chan, 0, 2), (1, LC // LANES))
        rr, ri = rows_lag(rr_ref[gi]), rows_lag(ri_ref[gi])
        bb = jnp.broadcast_to(bbt_ref[gi][None], (L, C, 2 * P)).reshape(LC, 2 * P)
        yield

        def c_times_power(pw):
            return jnp.concatenate([cc[:P] * pw[:P] - cc[P:] * pw[P:],
                                    -(cc[:P] * pw[P:] + cc[P:] * pw[:P])], axis=0)

        vt = c_times_power(pw_hi).astype(BF16)
        wt = (rr * bb + ri * pltpu.roll(bb, P, axis=1)).astype(BF16)
        taps = lax.dot_general(bbt_ref[gi], c_times_power(pw_lo), _NN,
                               precision=lax.Precision.HIGHEST,
                               preferred_element_type=F32)
        s = jnp.dot(u, wt, preferred_element_type=F32)
        yield
        on_diag = iota(taps.shape, 1) == iota(taps.shape, 0)
        taps = taps + jnp.where(on_diag, dsk_ref[gi], 0.0)
        padded = jnp.concatenate([jnp.zeros_like(taps), taps], axis=1)
        for sub in range(lags_per_tile):
            shifted = padded if sub == 0 else pltpu.roll(padded, sub * C, axis=1)
            shifted = shifted.astype(BF16)
            for q in range(L // lags_per_tile):
                lag = q * lags_per_tile + sub
                tt_ref[gi, lag * C:(lag + 1) * C, :] = (
                    shifted[:, LC - q * LANES:2 * LC - q * LANES])
        yield
        chunk_idx = iota(s.shape, 0) % n_chunks
        step = 1
        it = 0
        while step < n_chunks:
            sp = jnp.where(chunk_idx >= step, pltpu.roll(s, step, axis=0), 0.0)
            s = (s + c1_ref[gi, it:it + 1, :] * sp
                 + c2_ref[gi, it:it + 1, :] * pltpu.roll(sp, P, axis=1))
            step *= 2
            it += 1
            yield
        s_start = jnp.where(chunk_idx >= 1, pltpu.roll(s, 1, axis=0), 0.0)
        y = (jnp.dot(u, tt_ref[gi], preferred_element_type=F32)
             + jnp.dot(s_start.astype(BF16), vt, preferred_element_type=F32))
        yield
        o_ref[gi] = jax.nn.gelu(y).astype(o_ref.dtype)

    pending = [group_stages(gi) for gi in range(u_ref.shape[0])]
    while pending:
        pending = [gen for gen in pending if not next(gen, True)]


def _s5_tables(lam_re, lam_im, log_step, b_re, b_im, c_re, c_im, d_skip, n_chunks):
    g, p = lam_re.shape
    c = S5_GROUP
    L = S5_CHUNK
    lr = jnp.minimum(lam_re.astype(F32), LAMBDA_RE_MAX)
    li = lam_im.astype(F32)
    dt = jnp.exp(log_step.astype(F32))[:, None]
    ldt_re, ldt_im = lr * dt, li * dt
    mag = jnp.exp(ldt_re)
    ab_re, ab_im = mag * jnp.cos(ldt_im), mag * jnp.sin(ldt_im)
    den = lr * lr + li * li
    q_re = ((ab_re - 1.0) * lr + ab_im * li) / den
    q_im = (ab_im * lr - (ab_re - 1.0) * li) / den
    br, bi = b_re.astype(F32), b_im.astype(F32)
    bb_re = q_re[..., None] * br - q_im[..., None] * bi
    bb_im = q_re[..., None] * bi + q_im[..., None] * br
    dsk = d_skip.astype(F32).reshape(g, c)

    def power(n):
        m_ = jnp.exp(n * ldt_re)
        return m_ * jnp.cos(n * ldt_im), m_ * jnp.sin(n * ldt_im)

    lag = jnp.arange(0, L + 1, dtype=F32)[:, None, None]
    pw_re, pw_im = power(lag)
    pwt = jnp.concatenate([pw_re, pw_im], axis=-1).transpose(1, 2, 0)
    cc = jnp.concatenate([c_re.astype(F32), c_im.astype(F32)], axis=-1).transpose(0, 2, 1)
    bbt = jnp.concatenate([bb_re, bb_im], axis=1).transpose(0, 2, 1)
    rev_re = pw_re[::-1][1:].transpose(1, 0, 2)
    rev_im = pw_im[::-1][1:].transpose(1, 0, 2)
    rr = jnp.concatenate([rev_re, rev_re], axis=-1)
    ri = jnp.concatenate([-rev_im, rev_im], axis=-1)
    n_steps = max(1, (n_chunks - 1).bit_length())
    hop = (L * 2.0 ** jnp.arange(n_steps, dtype=F32))[:, None, None]
    hop_re, hop_im = power(hop)
    c1 = jnp.concatenate([hop_re, hop_re], axis=-1).transpose(1, 0, 2)
    c2 = jnp.concatenate([-hop_im, hop_im], axis=-1).transpose(1, 0, 2)
    return pwt[:, :, :L], pwt[:, :, 1:], cc, bbt, rr, ri, dsk[:, :, None], c1, c2


def _s5_core(h, seq, lam_re, lam_im, log_step, b_re, b_im, c_re, c_im, d_skip):
    m, d = h.shape
    g, p = lam_re.shape
    c, L = S5_GROUP, S5_CHUNK
    n_chunks = seq // L
    rows = m // L
    tables = _s5_tables(lam_re, lam_im, log_step, b_re, b_im, c_re, c_im, d_skip, n_chunks)
    u = h.reshape(rows, L, g, c).transpose(2, 0, 1, 3).reshape(g, rows, L * c)
    gb = _tile(g, S5_GROUPS_PER_STEP, 1)

    def grp(shape):
        return pl.BlockSpec((gb,) + tuple(shape), lambda gi: (gi, 0, 0))

    y = pl.pallas_call(
        functools.partial(_s5_body, n_chunks=n_chunks, state=p),
        grid=(g // gb,),
        in_specs=[grp((rows, L * c))] + [grp(t.shape[1:]) for t in tables],
        out_specs=grp((rows, L * c)),
        out_shape=jax.ShapeDtypeStruct((g, rows, L * c), BF16),
        scratch_shapes=[pltpu.VMEM((gb, L * c, L * c), BF16)],
        compiler_params=_params("parallel"),
        name="s5_chunk_scan",
    )(u, *tables)
    return y.reshape(g, rows, L, c).transpose(1, 2, 0, 3).reshape(m, d)


def _ffn(x, pre, w_in, w_out, widx, norm_g, next_g):
    if pre is None:
        act = _dual_matmul(_rmsnorm(x, norm_g, BF16), w_in, widx, "swiglu")
    else:
        act = _dual_matmul(pre[0], w_in, widx, "swiglu", ssq=pre[1])
    return _resid_matmul(act, w_out, widx, x, 0.5, next_g=next_g)


def kernel(x, ffn_norm, ffn_w_in, ffn_w_out, mix_norm, rwkv_mu, rwkv_w_rkv, rwkv_w_o, rwkv_w0, rwkv_w_l1, rwkv_w_l2, rwkv_a0, rwkv_a_l1, rwkv_a_l2, rwkv_v0, rwkv_v_l1, rwkv_v_l2, rwkv_g_l1, rwkv_g_l2, rwkv_k_k, rwkv_k_a, rwkv_r_k, rwkv_ln_w, rwkv_ln_b, s5_lam_re, s5_lam_im, s5_log_step, s5_b_re, s5_b_im, s5_c_re, s5_c_im, s5_d, s5_w_glu, final_norm):
    bsz, seq, d = x.shape
    depth = ffn_norm.shape[0]
    n_mixers = 2
    w_in, w_out = ffn_w_in.astype(BF16), ffn_w_out.astype(BF16)
    w_rkv, w_o, w_glu = rwkv_w_rkv.astype(BF16), rwkv_w_o.astype(BF16), s5_w_glu.astype(BF16)
    x = x.reshape(bsz * seq, d)
    v_first = None
    pre = None
    for i in range(depth):
        x = _ffn(x, pre, w_in, w_out, (i, 0), ffn_norm[i, 0], None)
        j = i // n_mixers
        if i % n_mixers == 0:
            v_res = None if j == 0 else (rwkv_v0[j - 1], rwkv_v_l1[j - 1], rwkv_v_l2[j - 1])
            (x, xg, ssq), v_first = _rwkv_layer(
                x, seq, v_first, mix_norm[i], rwkv_mu[j], w_rkv, w_o, (j,),
                rwkv_w0[j], rwkv_w_l1[j], rwkv_w_l2[j], rwkv_a0[j], rwkv_a_l1[j], rwkv_a_l2[j],
                rwkv_g_l1[j], rwkv_g_l2[j], rwkv_k_k[j], rwkv_k_a[j], rwkv_r_k[j],
                rwkv_ln_w[j], rwkv_ln_b[j], v_res, ffn_norm[i, 1])
        else:
            h = _rmsnorm(x, mix_norm[i], BF16)
            y = _s5_core(h, seq, s5_lam_re[j], s5_lam_im[j], s5_log_step[j], s5_b_re[j],
                         s5_b_im[j], s5_c_re[j], s5_c_im[j], s5_d[j])
            x, xg, ssq = _dual_matmul(y, w_glu, (j,), "glu_residual", res=x,
                                      next_g=ffn_norm[i, 1])
        if i + 1 < depth:
            x, xg, ssq = _ffn(x, (xg, ssq), w_in, w_out, (i, 1), None, ffn_norm[i + 1, 0])
            pre = (xg, ssq)
        else:
            x = _ffn(x, (xg, ssq), w_in, w_out, (i, 1), None, None)
    return _rmsnorm(x, final_norm, F32).reshape(bsz, seq, d)
```

```python
import functools
import math

import jax
import jax.numpy as jnp
from jax import lax
from jax.experimental import pallas as pl
from jax.experimental.pallas import tpu as pltpu

F32 = jnp.float32
BF16 = jnp.bfloat16

RMS_EPS = 1e-6
GN_EPS = 64e-5
LAMBDA_RE_MAX = -1e-4
HEAD = 64
CHUNK = 64
S5_CHUNK = 64
S5_GROUP = 16
LANES = 128
VMEM_LIMIT = 56 * 1024 * 1024

_NN = (((1,), (0,)), ((), ()))
_NT = (((1,), (1,)), ((), ()))
_TN = (((0,), (0,)), ((), ()))


def _params(*sem):
    return pltpu.CompilerParams(dimension_semantics=sem, vmem_limit_bytes=VMEM_LIMIT)


def _tile(n, pref, quantum):
    best = None
    t = quantum
    while t <= min(n, pref):
        if n % t == 0:
            best = t
        t += quantum
    return best if best is not None else n


def _rms_body(x_ref, g_ref, o_ref):
    x = x_ref[...]
    ms = jnp.mean(x * x, axis=-1, keepdims=True)
    o_ref[...] = (x * lax.rsqrt(ms + RMS_EPS) * g_ref[...]).astype(o_ref.dtype)


def _rmsnorm(x, g, out_dtype):
    m, d = x.shape
    bm = _tile(m, 256, 8)
    return pl.pallas_call(
        _rms_body,
        grid=(m // bm,),
        in_specs=[pl.BlockSpec((bm, d), lambda i: (i, 0)),
                  pl.BlockSpec((1, d), lambda i: (0, 0))],
        out_specs=pl.BlockSpec((bm, d), lambda i: (i, 0)),
        out_shape=jax.ShapeDtypeStruct((m, d), out_dtype),
        compiler_params=_params("parallel"),
        name="rmsnorm",
    )(x, g.reshape(1, d))


_DECAY_SCALE = math.exp(-0.5)


def _rwkv_front_body(*refs, tiles_per_seq, has_vres):
    x_ref, g_ref, mu_ref = refs[:3]
    n_lora = 11 if has_vres else 8
    lora = refs[3:3 + n_lora]
    outs = refs[3 + n_lora:-1]
    carry_ref = refs[-1]
    wl1, wl2, w0, al1, al2, a0, gl1, gl2 = lora[:8]
    xmix_ref, lw_ref, a_ref, gate_ref = outs[:4]
    i = pl.program_id(0)

    @pl.when(i % tiles_per_seq == 0)
    def _():
        carry_ref[...] = jnp.zeros_like(carry_ref)

    x = x_ref[...]
    ms = jnp.mean(x * x, axis=-1, keepdims=True)
    h = x * lax.rsqrt(ms + RMS_EPS) * g_ref[...]
    bm = h.shape[0]
    row = lax.broadcasted_iota(jnp.int32, h.shape, 0)
    prev = jnp.where(row == 0, carry_ref[...], pltpu.roll(h, 1, axis=0))
    carry_ref[...] = h[bm - 1:bm, :]
    dx = prev - h

    def mixed(c):
        return (h + dx * mu_ref[c:c + 1, :]).astype(BF16)

    def low_rank(xc, l1_ref, l2_ref, inner):
        t = inner(jnp.dot(xc, l1_ref[...], preferred_element_type=F32))
        return jnp.dot(t.astype(BF16), l2_ref[...], preferred_element_type=F32)

    xmix_ref[0] = mixed(0)
    xmix_ref[1] = mixed(1)
    xv = mixed(2)
    xmix_ref[2] = xv
    lw_ref[...] = -_DECAY_SCALE * jax.nn.sigmoid(low_rank(mixed(3), wl1, wl2, jnp.tanh) + w0[...])
    a_ref[...] = jax.nn.sigmoid(low_rank(mixed(4), al1, al2, lambda t: t)
                                + a0[...]).astype(a_ref.dtype)
    gate_ref[...] = low_rank(mixed(5), gl1, gl2, jax.nn.sigmoid).astype(gate_ref.dtype)
    if has_vres:
        vl1, vl2, v0 = lora[8:]
        outs[4][...] = jax.nn.sigmoid(low_rank(xv, vl1, vl2, lambda t: t)
                                      + v0[...]).astype(outs[4].dtype)


def _pad_rank(l1, l2):
    pad = (-l1.shape[1]) % LANES
    return (jnp.pad(l1, ((0, 0), (0, pad))).astype(BF16),
            jnp.pad(l2, ((0, pad), (0, 0))).astype(BF16))


def _rwkv_front(x, g, mu, seq, w_lora, a_lora, g_lora, v_lora):
    m, d = x.shape
    bm = _tile(seq, 128, 8)
    row = pl.BlockSpec((bm, d), lambda i: (i, 0))
    vec = pl.BlockSpec((1, d), lambda i: (0, 0))
    in_specs = [row, vec, pl.BlockSpec(mu.shape, lambda i: (0, 0))]
    args = [x, g.reshape(1, d), mu]

    def add_pair(l1, l2):
        l1, l2 = _pad_rank(l1, l2)
        in_specs.extend([pl.BlockSpec(l1.shape, lambda i: (0, 0)),
                         pl.BlockSpec(l2.shape, lambda i: (0, 0))])
        args.extend([l1, l2])

    def add_bias(b):
        in_specs.append(vec)
        args.append(b.reshape(1, d))

    w0, w_l1, w_l2 = w_lora
    a0, a_l1, a_l2 = a_lora
    add_pair(w_l1, w_l2)
    add_bias(w0)
    add_pair(a_l1, a_l2)
    add_bias(a0)
    add_pair(*g_lora)
    n_tok_out = 3
    if v_lora is not None:
        v0, v_l1, v_l2 = v_lora
        add_pair(v_l1, v_l2)
        add_bias(v0)
        n_tok_out = 4
    return pl.pallas_call(
        functools.partial(_rwkv_front_body, tiles_per_seq=seq // bm, has_vres=v_lora is not None),
        grid=(m // bm,),
        in_specs=in_specs,
        out_specs=[pl.BlockSpec((3, bm, d), lambda i: (0, i, 0))] + [row] * n_tok_out,
        out_shape=[jax.ShapeDtypeStruct((3, m, d), BF16)]
        + [jax.ShapeDtypeStruct((m, d), F32)]
        + [jax.ShapeDtypeStruct((m, d), BF16)] * (n_tok_out - 1),
        scratch_shapes=[pltpu.VMEM((1, d), F32)],
        compiler_params=_params("arbitrary"),
        name="rwkv_front",
    )(*args)


def _w_spec(widx, k, bn, col):
    return pl.BlockSpec((None,) * len(widx) + (k, bn), lambda *gi: widx + (0, col(*gi)))


def _emit_next_norm_input(x_new, g_ref, xg_ref, ssq_ref):
    xg_ref[...] = (x_new * g_ref[...]).astype(xg_ref.dtype)
    ssq_ref[...] = jnp.sum(x_new * x_new, axis=-1, keepdims=True)


def _next_norm_specs(m, n, bm, bn):
    g_spec = pl.BlockSpec((1, bn), lambda i, j: (0, j))
    out_specs = [pl.BlockSpec((bm, bn), lambda i, j: (i, j)),
                 pl.BlockSpec((None, bm, 1), lambda i, j: (j, i, 0))]
    out_shapes = [jax.ShapeDtypeStruct((m, n), BF16),
                  jax.ShapeDtypeStruct((n // bn, m, 1), F32)]
    return g_spec, out_specs, out_shapes


def _dual_body(*refs, mode, normed, emit_next, k_dim):
    refs = list(refs)
    a_ref, w1_ref, w2_ref = refs[:3]
    del refs[:3]
    a = a_ref[...]
    p1 = jnp.dot(a, w1_ref[...], preferred_element_type=F32)
    p2 = jnp.dot(a, w2_ref[...], preferred_element_type=F32)
    if not normed:
        ssq_ref = refs.pop(0)
        rstd = lax.rsqrt(jnp.sum(ssq_ref[...], axis=0) * (1.0 / k_dim) + RMS_EPS)
        p1, p2 = p1 * rstd, p2 * rstd
    if mode == "swiglu":
        (o_ref,) = refs
        o_ref[...] = (jax.nn.silu(p1) * p2).astype(o_ref.dtype)
        return
    res_ref = refs.pop(0)
    x_new = res_ref[...] + p1 * jax.nn.sigmoid(p2)
    if emit_next:
        g_ref, o_ref, xg_ref, ssq_out_ref = refs
        _emit_next_norm_input(x_new, g_ref, xg_ref, ssq_out_ref)
    else:
        (o_ref,) = refs
    o_ref[...] = x_new


def _dual_matmul(a, w, widx, mode, ssq=None, res=None, next_g=None):
    m, k = a.shape
    n = w.shape[-1] // 2
    bm = _tile(m, 1024, 8)
    bn = _tile(n, 512, LANES)
    nj = n // bn
    in_specs = [pl.BlockSpec((bm, k), lambda i, j: (i, 0)),
                _w_spec(widx, k, bn, lambda i, j: j),
                _w_spec(widx, k, bn, lambda i, j: j + nj)]
    args = [a, w, w]
    if ssq is not None:
        in_specs.append(pl.BlockSpec((ssq.shape[0], bm, 1), lambda i, j: (0, i, 0)))
        args.append(ssq)
    tile = pl.BlockSpec((bm, bn), lambda i, j: (i, j))
    out_specs, out_shapes = [tile], [jax.ShapeDtypeStruct((m, n), BF16 if mode == "swiglu" else F32)]
    if mode == "glu_residual":
        in_specs.append(tile)
        args.append(res)
        if next_g is not None:
            g_spec, extra_specs, extra_shapes = _next_norm_specs(m, n, bm, bn)
            in_specs.append(g_spec)
            args.append(next_g.reshape(1, n))
            out_specs += extra_specs
            out_shapes += extra_shapes
    out = pl.pallas_call(
        functools.partial(_dual_body, mode=mode, normed=ssq is None,
                          emit_next=next_g is not None, k_dim=k),
        grid=(m // bm, nj),
        in_specs=in_specs,
        out_specs=out_specs,
        out_shape=out_shapes,
        compiler_params=_params("parallel", "parallel"),
        name="matmul_" + mode,
    )(*args)
    return out[0] if len(out) == 1 else tuple(out)


def _resid_body(a_ref, w_ref, res_ref, *rest, scale, emit_next):
    acc = jnp.dot(a_ref[...], w_ref[...], preferred_element_type=F32)
    x_new = res_ref[...] + scale * acc
    if emit_next:
        g_ref, o_ref, xg_ref, ssq_ref = rest
        _emit_next_norm_input(x_new, g_ref, xg_ref, ssq_ref)
    else:
        (o_ref,) = rest
    o_ref[...] = x_new


def _resid_matmul(a, w, widx, res, scale, next_g=None):
    m, k = a.shape
    n = w.shape[-1]
    bm = _tile(m, 1024, 8)
    bn = _tile(n, 512, LANES)
    tile = pl.BlockSpec((bm, bn), lambda i, j: (i, j))
    in_specs = [pl.BlockSpec((bm, k), lambda i, j: (i, 0)),
                _w_spec(widx, k, bn, lambda i, j: j), tile]
    args = [a, w, res]
    out_specs, out_shapes = [tile], [jax.ShapeDtypeStruct((m, n), F32)]
    if next_g is not None:
        g_spec, extra_specs, extra_shapes = _next_norm_specs(m, n, bm, bn)
        in_specs.append(g_spec)
        args.append(next_g.reshape(1, n))
        out_specs += extra_specs
        out_shapes += extra_shapes
    out = pl.pallas_call(
        functools.partial(_resid_body, scale=scale, emit_next=next_g is not None),
        grid=(m // bm, n // bn),
        in_specs=in_specs,
        out_specs=out_specs,
        out_shape=out_shapes,
        compiler_params=_params("parallel", "parallel"),
        name="matmul_residual",
    )(*args)
    return out[0] if len(out) == 1 else tuple(out)


def _plain_body(a_ref, w_ref, o_ref):
    o_ref[...] = jnp.dot(a_ref[...], w_ref[...], preferred_element_type=F32)


def _batched_matmul(a, w, widx, n_batch):
    _, m, k = a.shape
    n = w.shape[-1]
    bm = _tile(m, 1024, 8)
    bn = _tile(n, 1024, LANES)
    return pl.pallas_call(
        _plain_body,
        grid=(n_batch, m // bm, n // bn),
        in_specs=[pl.BlockSpec((None, bm, k), lambda c, i, j: (c, i, 0)),
                  pl.BlockSpec((None,) * (len(widx) + 1) + (k, bn),
                               lambda c, i, j: widx + (c, 0, j))],
        out_specs=pl.BlockSpec((None, bm, bn), lambda c, i, j: (c, i, j)),
        out_shape=jax.ShapeDtypeStruct((n_batch, m, n), F32),
        compiler_params=_params("parallel", "parallel", "parallel"),
        name="matmul_rkv",
    )(a, w)


RWKV_TIME_BLOCK = 512
RWKV_PAIRS_PER_STEP = 4


def _bdot(a, b, dims=_NN):
    return lax.dot_general(a.astype(BF16), b.astype(BF16), dims, preferred_element_type=F32)


def _split_dot(a, b, split, passes):
    acc = None
    rem = (a, b)[split]
    for _ in range(passes):
        part = rem.astype(BF16)
        term = (lax.dot_general(part, b, _NN, preferred_element_type=F32) if split == 0 else
                lax.dot_general(a, part, _NN, preferred_element_type=F32))
        acc = term if acc is None else acc + term
        rem = rem - part.astype(F32)
    return acc


def _rwkv_body(r_ref, k_ref, v_ref, lw_ref, a_ref, g_ref, kk_ref, ka_ref, rk_ref, lnw_ref, lnb_ref,
               *rest, n_chunks, n_pairs, has_vres):
    L, N, W = CHUNK, HEAD, LANES
    if has_vres:
        vfirst_ref, vgate_ref, o_ref, s_ref = rest
    else:
        o_ref, s_ref = rest

    @pl.when(pl.program_id(2) == 0)
    def _():
        s_ref[...] = jnp.zeros_like(s_ref)

    def iota(shape, dim):
        return lax.broadcasted_iota(jnp.int32, shape, dim)

    same_head = (iota((W, W), 0) // N) == (iota((W, W), 1) // N)
    g_row, g_col = iota((2 * L, 2 * W), 0), iota((2 * L, 2 * W), 1)
    mask_g = (g_col % N) < (g_row % L) + (g_row >= L).astype(jnp.int32)
    n_double = max(1, (L - 1).bit_length())

    def stack(x):
        head0 = (iota(x.shape, 1) % W) < N
        zero = jnp.zeros_like(x)
        return jnp.concatenate([jnp.where(head0, x, zero), jnp.where(head0, zero, x)], axis=0)

    head0_lanes = iota((L, W), 1) < N
    step_idx = iota((L, W), 0)

    def seg_sum(x):
        s0 = jnp.sum(jnp.where(head0_lanes, x, 0.0), axis=-1, keepdims=True)
        s1 = jnp.sum(jnp.where(head0_lanes, 0.0, x), axis=-1, keepdims=True)
        return jnp.where(head0_lanes, s0, s1)

    def running_sum(x):
        shift = 1
        while shift < L:
            x = x + jnp.where(step_idx >= shift, pltpu.roll(x, shift, axis=0), 0.0)
            shift *= 2
        return x

    eye2 = ((iota((L, W), 1) % N) == iota((L, W), 0)).astype(F32)
    state = {pr: s_ref[pr] for pr in range(n_pairs)}

    def chunk_stages(c, pr):
        rows, lanes = slice(c * L, (c + 1) * L), slice(pr * W, (pr + 1) * W)
        r, k, v = r_ref[rows, lanes], k_ref[rows, lanes], v_ref[rows, lanes]
        if has_vres:
            v = v + (vfirst_ref[rows, lanes] - v) * vgate_ref[rows, lanes].astype(F32)
        lw, a = lw_ref[rows, lanes], a_ref[rows, lanes].astype(F32)
        kk_p, ka_p, rk_p = kk_ref[:, lanes], ka_ref[:, lanes], rk_ref[:, lanes]
        kkr = k * kk_p
        kh = k * (1.0 + (a - 1.0) * ka_p)
        ssq = seg_sum(kkr * kkr)
        lp = running_sum(lw)
        bonus_w = seg_sum(r * kh * rk_p)
        yield
        kk = kkr / jnp.maximum(jnp.sqrt(ssq), 1e-12)
        ba = kk * a
        lp_end = lp[L - 1:L, :]
        e_neg = jnp.exp(-lp)
        e_end = jnp.exp(lp_end - lp)
        at = -kk * jnp.exp(lp - lw)
        rt = r * jnp.exp(lp)
        v16 = v.astype(BF16)
        ar = jnp.concatenate([at, rt], axis=0)
        bk = jnp.concatenate([stack((ba * e_neg).astype(BF16)),
                              stack((kh * e_neg).astype(BF16))], axis=0)
        gm = _bdot(ar, bk, _NT)
        bk_end_t = jnp.concatenate([ba * e_end, kh * e_end], axis=0).T.astype(BF16)
        p_col = jnp.broadcast_to(jnp.exp(lp_end), (W, W)).T
        yield
        gm = jnp.where(mask_g, gm, 0.0)
        nk, gak = gm[:L, :W], gm[:L, W:]
        grb, grk = gm[L:, :W], gm[L:, W:]
        sv = stack(v16)
        nk16 = nk.astype(BF16)
        aakv = _bdot(gak, sv)
        npow = _bdot(nk16, stack(nk16))
        yield
        t = eye2 + nk
        for it in range(1, n_double):
            np16, t16 = npow.astype(BF16), t.astype(BF16)
            if it + 1 < n_double:
                p = _bdot(np16, jnp.concatenate([stack(np16), stack(t16)], axis=1))
                yield
                npow, t = p[:, :W], t + p[:, W:]
            else:
                p = _bdot(np16, stack(t16))
                yield
                t = t + p
        w16 = _bdot(t, stack(jnp.concatenate([at, aakv], axis=1).astype(BF16))).astype(BF16)
        yield
        p2 = _bdot(grb, stack(w16))
        p3 = _bdot(grk, sv)
        below = jnp.concatenate([jnp.zeros((L, W), BF16), v16], axis=1)
        mt = _bdot(bk_end_t, jnp.concatenate([w16, below], axis=0))
        yield
        rp = rt + p2[:, :W]
        yv = p2[:, W:] + p3
        zero = jnp.zeros((W, W), F32)
        m_bd_t = jnp.where(same_head, mt[:, :W], zero)
        sv_bd_t = jnp.where(same_head, mt[:, W:], zero)
        sst = state[pr]
        sst16 = sst.astype(BF16)
        ys = _bdot(rp, sst16)
        state[pr] = p_col * sst + _bdot(m_bd_t, sst16) + sv_bd_t
        yield
        y = yv + ys
        mean = seg_sum(y) * (1.0 / N)
        yield
        yc = y - mean
        var = seg_sum(yc * yc) * (1.0 / N)
        yield
        yn = yc * lax.rsqrt(var + GN_EPS) * lnw_ref[:, lanes] + lnb_ref[:, lanes]
        gate = g_ref[rows, lanes].astype(F32)
        o_ref[rows, lanes] = ((yn + bonus_w * v) * gate).astype(o_ref.dtype)

    pending = {(c, pr): chunk_stages(c, pr) for c in range(n_chunks) for pr in range(n_pairs)}
    slot = 0
    while pending:
        for key in sorted(pending):
            if key[0] <= slot and next(pending[key], True):
                del pending[key]
        slot += 1
    for pr in range(n_pairs):
        s_ref[pr] = state[pr]


def _rwkv_recurrence(rkv, v_mix, lw, a, g, k_k, k_a, r_k, ln_w, ln_b):
    _, b, t, d = rkv.shape
    tb = _tile(t, RWKV_TIME_BLOCK, CHUNK)
    width = _tile(d, RWKV_PAIRS_PER_STEP * LANES, LANES)
    tok = pl.BlockSpec((None, tb, width), lambda bi, hi, ti: (bi, ti, hi))
    par = pl.BlockSpec((1, width), lambda bi, hi, ti: (0, hi))

    def stacked(c):
        return pl.BlockSpec((None, None, tb, width), lambda bi, hi, ti: (c, bi, ti, hi))

    in_specs = [stacked(0), stacked(1), stacked(2), tok, tok, tok] + [par] * 5
    args = [rkv, rkv, rkv, lw, a, g, *(p.reshape(1, d) for p in (k_k, k_a, r_k, ln_w, ln_b))]
    if v_mix is not None:
        in_specs += [stacked(2), tok]
        args += list(v_mix)
    return pl.pallas_call(
        functools.partial(_rwkv_body, n_chunks=tb // CHUNK, n_pairs=width // LANES,
                          has_vres=v_mix is not None),
        grid=(b, d // width, t // tb),
        in_specs=in_specs,
        out_specs=tok,
        out_shape=jax.ShapeDtypeStruct((b, t, d), BF16),
        scratch_shapes=[pltpu.VMEM((width // LANES, LANES, LANES), F32)],
        compiler_params=_params("parallel", "parallel", "arbitrary"),
        name="rwkv7_recurrence",
    )(*args)


def _rwkv_layer(x, seq, v_first, norm_g, mu, w_rkv, w_o, widx, w0, w_l1, w_l2, a0, a_l1, a_l2,
                g_l1, g_l2, k_k, k_a, r_k, ln_w, ln_b, v_res, next_g):
    m, d = x.shape
    bsz = m // seq
    xmix, lw, a, g, *v_gate = _rwkv_front(x, norm_g, mu, seq, (w0, w_l1, w_l2), (a0, a_l1, a_l2),
                                          (g_l1, g_l2), v_res)
    rkv = _batched_matmul(xmix, w_rkv, widx, 3)
    sh = (bsz, seq, d)
    rkv4 = rkv.reshape((3,) + sh)
    if v_res is None:
        v_first = rkv4
        v_mix = None
    else:
        v_mix = (v_first, v_gate[0].reshape(sh))
    yg = _rwkv_recurrence(rkv4, v_mix, lw.reshape(sh), a.reshape(sh), g.reshape(sh),
                          k_k, k_a, r_k, ln_w, ln_b)
    return _resid_matmul(yg.reshape(m, d), w_o, widx, x, 1.0, next_g=next_g), v_first


S5_GROUPS_PER_STEP = 8


def _s5_body(u_ref, pwlo_ref, pwhi_ref, cc_ref, bbt_ref, rr_ref, ri_ref, dsk_ref, c1_ref, c2_ref,
             o_ref, tt_ref, *, n_chunks, state):
    L, C, P = S5_CHUNK, S5_GROUP, state
    LC = L * C
    lags_per_tile = LANES // C

    def iota(shape, dim):
        return lax.broadcasted_iota(jnp.int32, shape, dim)

    rep_lag = (iota((L, LC), 1) // C == iota((L, LC), 0)).astype(BF16)
    rep_chan = (iota((C, LANES), 1) % C == iota((C, LANES), 0)).astype(BF16)

    def rows_lag(x):
        return jnp.broadcast_to(x[:, None, :], (L, C, x.shape[1])).reshape(LC, x.shape[1])

    def group_stages(gi):
        u = u_ref[gi]
        pw_lo = _split_dot(pwlo_ref[gi], rep_lag, 0, 2)
        pw_hi = _split_dot(pwhi_ref[gi], rep_lag, 0, 1)
        cc = jnp.tile(_split_dot(cc_ref[gi], rep_chan, 0, 2), (1, LC // LANES))
        rr, ri = rows_lag(rr_ref[gi]), rows_lag(ri_ref[gi])
        bb = jnp.broadcast_to(bbt_ref[gi][None], (L, C, 2 * P)).reshape(LC, 2 * P)
        yield

        def c_times_power(pw):
            return jnp.concatenate([cc[:P] * pw[:P] - cc[P:] * pw[P:],
                                    -(cc[:P] * pw[P:] + cc[P:] * pw[:P])], axis=0)

        vt = c_times_power(pw_hi).astype(BF16)
        wt = (rr * bb + ri * pltpu.roll(bb, P, axis=1)).astype(BF16)
        taps = lax.dot_general(bbt_ref[gi], c_times_power(pw_lo), _NN,
                               precision=lax.Precision.HIGHEST,
                               preferred_element_type=F32)
        s = jnp.dot(u, wt, preferred_element_type=F32)
        yield
        on_diag = iota(taps.shape, 1) == iota(taps.shape, 0)
        taps = taps + jnp.where(on_diag, dsk_ref[gi], 0.0)
        padded = jnp.concatenate([jnp.zeros_like(taps), taps], axis=1)
        for sub in range(lags_per_tile):
            shifted = padded if sub == 0 else pltpu.roll(padded, sub * C, axis=1)
            shifted = shifted.astype(BF16)
            for q in range(L // lags_per_tile):
                lag = q * lags_per_tile + sub
                tt_ref[gi, lag * C:(lag + 1) * C, :] = (
                    shifted[:, LC - q * LANES:2 * LC - q * LANES])
        yield
        chunk_idx = iota(s.shape, 0) % n_chunks
        step = 1
        it = 0
        while step < n_chunks:
            sp = jnp.where(chunk_idx >= step, pltpu.roll(s, step, axis=0), 0.0)
            s = (s + c1_ref[gi, it:it + 1, :] * sp
                 + c2_ref[gi, it:it + 1, :] * pltpu.roll(sp, P, axis=1))
            step *= 2
            it += 1
            yield
        s_start = jnp.where(chunk_idx >= 1, pltpu.roll(s, 1, axis=0), 0.0)
        y = (jnp.dot(u, tt_ref[gi], preferred_element_type=F32)
             + jnp.dot(s_start.astype(BF16), vt, preferred_element_type=F32))
        yield
        o_ref[gi] = jax.nn.gelu(y).astype(o_ref.dtype)

    pending = [group_stages(gi) for gi in range(u_ref.shape[0])]
    while pending:
        pending = [gen for gen in pending if not next(gen, True)]


def _s5_tables(lam_re, lam_im, log_step, b_re, b_im, c_re, c_im, d_skip, n_chunks):
    g, p = lam_re.shape
    c = S5_GROUP
    L = S5_CHUNK
    lr = jnp.minimum(lam_re.astype(F32), LAMBDA_RE_MAX)
    li = lam_im.astype(F32)
    dt = jnp.exp(log_step.astype(F32))[:, None]
    ldt_re, ldt_im = lr * dt, li * dt
    mag = jnp.exp(ldt_re)
    ab_re, ab_im = mag * jnp.cos(ldt_im), mag * jnp.sin(ldt_im)
    den = lr * lr + li * li
    q_re = ((ab_re - 1.0) * lr + ab_im * li) / den
    q_im = (ab_im * lr - (ab_re - 1.0) * li) / den
    br, bi = b_re.astype(F32), b_im.astype(F32)
    bb_re = q_re[..., None] * br - q_im[..., None] * bi
    bb_im = q_re[..., None] * bi + q_im[..., None] * br
    dsk = d_skip.astype(F32).reshape(g, c)

    def power(n):
        m_ = jnp.exp(n * ldt_re)
        return m_ * jnp.cos(n * ldt_im), m_ * jnp.sin(n * ldt_im)

    lag = jnp.arange(0, L + 1, dtype=F32)[:, None, None]
    pw_re, pw_im = power(lag)
    pwt = jnp.concatenate([pw_re, pw_im], axis=-1).transpose(1, 2, 0)
    cc = jnp.concatenate([c_re.astype(F32), c_im.astype(F32)], axis=-1).transpose(0, 2, 1)
    bbt = jnp.concatenate([bb_re, bb_im], axis=1).transpose(0, 2, 1)
    rev_re = pw_re[::-1][1:].transpose(1, 0, 2)
    rev_im = pw_im[::-1][1:].transpose(1, 0, 2)
    rr = jnp.concatenate([rev_re, rev_re], axis=-1)
    ri = jnp.concatenate([-rev_im, rev_im], axis=-1)
    n_steps = max(1, (n_chunks - 1).bit_length())
    hop = (L * 2.0 ** jnp.arange(n_steps, dtype=F32))[:, None, None]
    hop_re, hop_im = power(hop)
    c1 = jnp.concatenate([hop_re, hop_re], axis=-1).transpose(1, 0, 2)
    c2 = jnp.concatenate([-hop_im, hop_im], axis=-1).transpose(1, 0, 2)
    return pwt[:, :, :L], pwt[:, :, 1:], cc, bbt, rr, ri, dsk[:, :, None], c1, c2


def _group_rows_body(*refs, rows, per_tile):
    *lag_refs, perm_ref, o_ref = refs
    n_tiles = len(lag_refs) // per_tile
    a = jnp.concatenate(
        [jnp.concatenate([lag_refs[q * per_tile + s][...] for s in range(per_tile)], axis=1)
         for q in range(n_tiles)], axis=0)
    res = jnp.dot(a, perm_ref[...], preferred_element_type=F32)
    for gi in range(per_tile):
        for q in range(n_tiles):
            o_ref[gi, :, q * LANES:(q + 1) * LANES] = (
                res[q * rows:(q + 1) * rows, gi * LANES:(gi + 1) * LANES].astype(o_ref.dtype))


def _group_rows(h, n_groups):
    m, d = h.shape
    c, L = S5_GROUP, S5_CHUNK
    rows = m // L
    per_tile = LANES // c
    n_blocks = d // LANES
    idx = jnp.arange(per_tile * LANES)
    s_in, g_in, c_in = idx // LANES, (idx % LANES) // c, idx % c
    target = g_in * LANES + s_in * c + c_in
    perm = (target[:, None] == idx[None, :]).astype(BF16)
    h2 = h.reshape(rows, L * d)
    lag_specs = [pl.BlockSpec((rows, LANES), lambda b, l=l: (0, l * n_blocks + b))
                 for l in range(L)]
    return pl.pallas_call(
        functools.partial(_group_rows_body, rows=rows, per_tile=per_tile),
        grid=(n_blocks,),
        in_specs=lag_specs + [pl.BlockSpec(perm.shape, lambda b: (0, 0))],
        out_specs=pl.BlockSpec((per_tile, rows, L * c), lambda b: (b, 0, 0)),
        out_shape=jax.ShapeDtypeStruct((n_groups, rows, L * c), BF16),
        compiler_params=_params("parallel"),
        name="s5_group_rows",
    )(*([h2] * L), perm)


def _s5_core(h, seq, lam_re, lam_im, log_step, b_re, b_im, c_re, c_im, d_skip):
    m, d = h.shape
    g, p = lam_re.shape
    c, L = S5_GROUP, S5_CHUNK
    n_chunks = seq // L
    rows = m // L
    tables = _s5_tables(lam_re, lam_im, log_step, b_re, b_im, c_re, c_im, d_skip, n_chunks)
    u = _group_rows(h, g)
    gb = _tile(g, S5_GROUPS_PER_STEP, 1)

    def grp(shape):
        return pl.BlockSpec((gb,) + tuple(shape), lambda gi: (gi, 0, 0))

    y = pl.pallas_call(
        functools.partial(_s5_body, n_chunks=n_chunks, state=p),
        grid=(g // gb,),
        in_specs=[grp((rows, L * c))] + [grp(t.shape[1:]) for t in tables],
        out_specs=grp((rows, L * c)),
        out_shape=jax.ShapeDtypeStruct((g, rows, L * c), BF16),
        scratch_shapes=[pltpu.VMEM((gb, L * c, L * c), BF16)],
        compiler_params=_params("parallel"),
        name="s5_chunk_scan",
    )(u, *tables)
    return y.reshape(g, rows, L, c).transpose(1, 2, 0, 3).reshape(m, d)


def _ffn(x, pre, w_in, w_out, widx, norm_g, next_g):
    if pre is None:
        act = _dual_matmul(_rmsnorm(x, norm_g, BF16), w_in, widx, "swiglu")
    else:
        act = _dual_matmul(pre[0], w_in, widx, "swiglu", ssq=pre[1])
    return _resid_matmul(act, w_out, widx, x, 0.5, next_g=next_g)


def kernel(x, ffn_norm, ffn_w_in, ffn_w_out, mix_norm, rwkv_mu, rwkv_w_rkv, rwkv_w_o, rwkv_w0, rwkv_w_l1, rwkv_w_l2, rwkv_a0, rwkv_a_l1, rwkv_a_l2, rwkv_v0, rwkv_v_l1, rwkv_v_l2, rwkv_g_l1, rwkv_g_l2, rwkv_k_k, rwkv_k_a, rwkv_r_k, rwkv_ln_w, rwkv_ln_b, s5_lam_re, s5_lam_im, s5_log_step, s5_b_re, s5_b_im, s5_c_re, s5_c_im, s5_d, s5_w_glu, final_norm):
    bsz, seq, d = x.shape
    depth = ffn_norm.shape[0]
    n_mixers = 2
    w_in, w_out = ffn_w_in.astype(BF16), ffn_w_out.astype(BF16)
    w_rkv, w_o, w_glu = rwkv_w_rkv.astype(BF16), rwkv_w_o.astype(BF16), s5_w_glu.astype(BF16)
    x = x.reshape(bsz * seq, d)
    v_first = None
    pre = None
    for i in range(depth):
        x = _ffn(x, pre, w_in, w_out, (i, 0), ffn_norm[i, 0], None)
        j = i // n_mixers
        if i % n_mixers == 0:
            v_res = None if j == 0 else (rwkv_v0[j - 1], rwkv_v_l1[j - 1], rwkv_v_l2[j - 1])
            (x, xg, ssq), v_first = _rwkv_layer(
                x, seq, v_first, mix_norm[i], rwkv_mu[j], w_rkv, w_o, (j,),
                rwkv_w0[j], rwkv_w_l1[j], rwkv_w_l2[j], rwkv_a0[j], rwkv_a_l1[j], rwkv_a_l2[j],
                rwkv_g_l1[j], rwkv_g_l2[j], rwkv_k_k[j], rwkv_k_a[j], rwkv_r_k[j],
                rwkv_ln_w[j], rwkv_ln_b[j], v_res, ffn_norm[i, 1])
        else:
            h = _rmsnorm(x, mix_norm[i], BF16)
            y = _s5_core(h, seq, s5_lam_re[j], s5_lam_im[j], s5_log_step[j], s5_b_re[j],
                         s5_b_im[j], s5_c_re[j], s5_c_im[j], s5_d[j])
            x, xg, ssq = _dual_matmul(y, w_glu, (j,), "glu_residual", res=x,
                                      next_g=ffn_norm[i, 1])
        if i + 1 < depth:
            x, xg, ssq = _ffn(x, (xg, ssq), w_in, w_out, (i, 1), None, ffn_norm[i + 1, 0])
            pre = (xg, ssq)
        else:
            x = _ffn(x, (xg, ssq), w_in, w_out, (i, 1), None, None)
    return _rmsnorm(x, final_norm, F32).reshape(bsz, seq, d)
```

```python
import functools
import math

import jax
import jax.numpy as jnp
from jax import lax
from jax.experimental import pallas as pl
from jax.experimental.pallas import tpu as pltpu

F32 = jnp.float32
BF16 = jnp.bfloat16

RMS_EPS = 1e-6
GN_EPS = 64e-5
LAMBDA_RE_MAX = -1e-4
HEAD = 64
CHUNK = 64
S5_CHUNK = 64
S5_GROUP = 16
LANES = 128
VMEM_LIMIT = 56 * 1024 * 1024

_NN = (((1,), (0,)), ((), ()))
_NT = (((1,), (1,)), ((), ()))
_TN = (((0,), (0,)), ((), ()))


def _params(*sem):
    return pltpu.CompilerParams(dimension_semantics=sem, vmem_limit_bytes=VMEM_LIMIT)


def _tile(n, pref, quantum):
    best = None
    t = quantum
    while t <= min(n, pref):
        if n % t == 0:
            best = t
        t += quantum
    return best if best is not None else n


def _rms_body(x_ref, g_ref, o_ref):
    x = x_ref[...]
    ms = jnp.mean(x * x, axis=-1, keepdims=True)
    o_ref[...] = (x * lax.rsqrt(ms + RMS_EPS) * g_ref[...]).astype(o_ref.dtype)


def _rmsnorm(x, g, out_dtype):
    m, d = x.shape
    bm = _tile(m, 256, 8)
    return pl.pallas_call(
        _rms_body,
        grid=(m // bm,),
        in_specs=[pl.BlockSpec((bm, d), lambda i: (i, 0)),
                  pl.BlockSpec((1, d), lambda i: (0, 0))],
        out_specs=pl.BlockSpec((bm, d), lambda i: (i, 0)),
        out_shape=jax.ShapeDtypeStruct((m, d), out_dtype),
        compiler_params=_params("parallel"),
        name="rmsnorm",
    )(x, g.reshape(1, d))


_DECAY_SCALE = math.exp(-0.5)


def _rwkv_front_body(*refs, tiles_per_seq, has_vres):
    x_ref, g_ref, mu_ref = refs[:3]
    n_lora = 11 if has_vres else 8
    lora = refs[3:3 + n_lora]
    outs = refs[3 + n_lora:-1]
    carry_ref = refs[-1]
    wl1, wl2, w0, al1, al2, a0, gl1, gl2 = lora[:8]
    xmix_ref, lw_ref, a_ref, gate_ref = outs[:4]
    i = pl.program_id(0)

    @pl.when(i % tiles_per_seq == 0)
    def _():
        carry_ref[...] = jnp.zeros_like(carry_ref)

    x = x_ref[...]
    ms = jnp.mean(x * x, axis=-1, keepdims=True)
    h = x * lax.rsqrt(ms + RMS_EPS) * g_ref[...]
    bm = h.shape[0]
    row = lax.broadcasted_iota(jnp.int32, h.shape, 0)
    prev = jnp.where(row == 0, carry_ref[...], pltpu.roll(h, 1, axis=0))
    carry_ref[...] = h[bm - 1:bm, :]
    dx = prev - h

    def mixed(c):
        return (h + dx * mu_ref[c:c + 1, :]).astype(BF16)

    def low_rank(xc, l1_ref, l2_ref, inner):
        t = inner(jnp.dot(xc, l1_ref[...], preferred_element_type=F32))
        return jnp.dot(t.astype(BF16), l2_ref[...], preferred_element_type=F32)

    xmix_ref[0] = mixed(0)
    xmix_ref[1] = mixed(1)
    xv = mixed(2)
    xmix_ref[2] = xv
    lw_ref[...] = -_DECAY_SCALE * jax.nn.sigmoid(low_rank(mixed(3), wl1, wl2, jnp.tanh) + w0[...])
    a_ref[...] = jax.nn.sigmoid(low_rank(mixed(4), al1, al2, lambda t: t)
                                + a0[...]).astype(a_ref.dtype)
    gate_ref[...] = low_rank(mixed(5), gl1, gl2, jax.nn.sigmoid).astype(gate_ref.dtype)
    if has_vres:
        vl1, vl2, v0 = lora[8:]
        outs[4][...] = jax.nn.sigmoid(low_rank(xv, vl1, vl2, lambda t: t)
                                      + v0[...]).astype(outs[4].dtype)


def _pad_rank(l1, l2):
    pad = (-l1.shape[1]) % LANES
    return (jnp.pad(l1, ((0, 0), (0, pad))).astype(BF16),
            jnp.pad(l2, ((0, pad), (0, 0))).astype(BF16))


def _rwkv_front(x, g, mu, seq, w_lora, a_lora, g_lora, v_lora):
    m, d = x.shape
    bm = _tile(seq, 128, 8)
    row = pl.BlockSpec((bm, d), lambda i: (i, 0))
    vec = pl.BlockSpec((1, d), lambda i: (0, 0))
    in_specs = [row, vec, pl.BlockSpec(mu.shape, lambda i: (0, 0))]
    args = [x, g.reshape(1, d), mu]

    def add_pair(l1, l2):
        l1, l2 = _pad_rank(l1, l2)
        in_specs.extend([pl.BlockSpec(l1.shape, lambda i: (0, 0)),
                         pl.BlockSpec(l2.shape, lambda i: (0, 0))])
        args.extend([l1, l2])

    def add_bias(b):
        in_specs.append(vec)
        args.append(b.reshape(1, d))

    w0, w_l1, w_l2 = w_lora
    a0, a_l1, a_l2 = a_lora
    add_pair(w_l1, w_l2)
    add_bias(w0)
    add_pair(a_l1, a_l2)
    add_bias(a0)
    add_pair(*g_lora)
    n_tok_out = 3
    if v_lora is not None:
        v0, v_l1, v_l2 = v_lora
        add_pair(v_l1, v_l2)
        add_bias(v0)
        n_tok_out = 4
    return pl.pallas_call(
        functools.partial(_rwkv_front_body, tiles_per_seq=seq // bm, has_vres=v_lora is not None),
        grid=(m // bm,),
        in_specs=in_specs,
        out_specs=[pl.BlockSpec((3, bm, d), lambda i: (0, i, 0))] + [row] * n_tok_out,
        out_shape=[jax.ShapeDtypeStruct((3, m, d), BF16)]
        + [jax.ShapeDtypeStruct((m, d), F32)]
        + [jax.ShapeDtypeStruct((m, d), BF16)] * (n_tok_out - 1),
        scratch_shapes=[pltpu.VMEM((1, d), F32)],
        compiler_params=_params("arbitrary"),
        name="rwkv_front",
    )(*args)


def _w_spec(widx, k, bn, col):
    return pl.BlockSpec((None,) * len(widx) + (k, bn), lambda *gi: widx + (0, col(*gi)))


def _emit_next_norm_input(x_new, g_ref, xg_ref, ssq_ref):
    xg_ref[...] = (x_new * g_ref[...]).astype(xg_ref.dtype)
    ssq_ref[...] = jnp.sum(x_new * x_new, axis=-1, keepdims=True)


def _next_norm_specs(m, n, bm, bn):
    g_spec = pl.BlockSpec((1, bn), lambda i, j: (0, j))
    out_specs = [pl.BlockSpec((bm, bn), lambda i, j: (i, j)),
                 pl.BlockSpec((None, bm, 1), lambda i, j: (j, i, 0))]
    out_shapes = [jax.ShapeDtypeStruct((m, n), BF16),
                  jax.ShapeDtypeStruct((n // bn, m, 1), F32)]
    return g_spec, out_specs, out_shapes


def _dual_body(*refs, mode, normed, emit_next, k_dim):
    refs = list(refs)
    a_ref, w1_ref, w2_ref = refs[:3]
    del refs[:3]
    a = a_ref[...]
    p1 = jnp.dot(a, w1_ref[...], preferred_element_type=F32)
    p2 = jnp.dot(a, w2_ref[...], preferred_element_type=F32)
    if not normed:
        ssq_ref = refs.pop(0)
        rstd = lax.rsqrt(jnp.sum(ssq_ref[...], axis=0) * (1.0 / k_dim) + RMS_EPS)
        p1, p2 = p1 * rstd, p2 * rstd
    if mode == "swiglu":
        (o_ref,) = refs
        o_ref[...] = (jax.nn.silu(p1) * p2).astype(o_ref.dtype)
        return
    res_ref = refs.pop(0)
    x_new = res_ref[...] + p1 * jax.nn.sigmoid(p2)
    if emit_next:
        g_ref, o_ref, xg_ref, ssq_out_ref = refs
        _emit_next_norm_input(x_new, g_ref, xg_ref, ssq_out_ref)
    else:
        (o_ref,) = refs
    o_ref[...] = x_new


def _dual_matmul(a, w, widx, mode, ssq=None, res=None, next_g=None):
    m, k = a.shape
    n = w.shape[-1] // 2
    bm = _tile(m, 1024, 8)
    bn = _tile(n, 512, LANES)
    nj = n // bn
    in_specs = [pl.BlockSpec((bm, k), lambda i, j: (i, 0)),
                _w_spec(widx, k, bn, lambda i, j: j),
                _w_spec(widx, k, bn, lambda i, j: j + nj)]
    args = [a, w, w]
    if ssq is not None:
        in_specs.append(pl.BlockSpec((ssq.shape[0], bm, 1), lambda i, j: (0, i, 0)))
        args.append(ssq)
    tile = pl.BlockSpec((bm, bn), lambda i, j: (i, j))
    out_specs, out_shapes = [tile], [jax.ShapeDtypeStruct((m, n), BF16 if mode == "swiglu" else F32)]
    if mode == "glu_residual":
        in_specs.append(tile)
        args.append(res)
        if next_g is not None:
            g_spec, extra_specs, extra_shapes = _next_norm_specs(m, n, bm, bn)
            in_specs.append(g_spec)
            args.append(next_g.reshape(1, n))
            out_specs += extra_specs
            out_shapes += extra_shapes
    out = pl.pallas_call(
        functools.partial(_dual_body, mode=mode, normed=ssq is None,
                          emit_next=next_g is not None, k_dim=k),
        grid=(m // bm, nj),
        in_specs=in_specs,
        out_specs=out_specs,
        out_shape=out_shapes,
        compiler_params=_params("parallel", "parallel"),
        name="matmul_" + mode,
    )(*args)
    return out[0] if len(out) == 1 else tuple(out)


def _resid_body(a_ref, w_ref, res_ref, *rest, scale, emit_next):
    acc = jnp.dot(a_ref[...], w_ref[...], preferred_element_type=F32)
    x_new = res_ref[...] + scale * acc
    if emit_next:
        g_ref, o_ref, xg_ref, ssq_ref = rest
        _emit_next_norm_input(x_new, g_ref, xg_ref, ssq_ref)
    else:
        (o_ref,) = rest
    o_ref[...] = x_new


def _resid_matmul(a, w, widx, res, scale, next_g=None):
    m, k = a.shape
    n = w.shape[-1]
    bm = _tile(m, 1024, 8)
    bn = _tile(n, 512, LANES)
    tile = pl.BlockSpec((bm, bn), lambda i, j: (i, j))
    in_specs = [pl.BlockSpec((bm, k), lambda i, j: (i, 0)),
                _w_spec(widx, k, bn, lambda i, j: j), tile]
    args = [a, w, res]
    out_specs, out_shapes = [tile], [jax.ShapeDtypeStruct((m, n), F32)]
    if next_g is not None:
        g_spec, extra_specs, extra_shapes = _next_norm_specs(m, n, bm, bn)
        in_specs.append(g_spec)
        args.append(next_g.reshape(1, n))
        out_specs += extra_specs
        out_shapes += extra_shapes
    out = pl.pallas_call(
        functools.partial(_resid_body, scale=scale, emit_next=next_g is not None),
        grid=(m // bm, n // bn),
        in_specs=in_specs,
        out_specs=out_specs,
        out_shape=out_shapes,
        compiler_params=_params("parallel", "parallel"),
        name="matmul_residual",
    )(*args)
    return out[0] if len(out) == 1 else tuple(out)


def _plain_body(a_ref, w_ref, o_ref):
    o_ref[...] = jnp.dot(a_ref[...], w_ref[...], preferred_element_type=F32)


def _batched_matmul(a, w, widx, n_batch):
    _, m, k = a.shape
    n = w.shape[-1]
    bm = _tile(m, 1024, 8)
    bn = _tile(n, 1024, LANES)
    return pl.pallas_call(
        _plain_body,
        grid=(n_batch, m // bm, n // bn),
        in_specs=[pl.BlockSpec((None, bm, k), lambda c, i, j: (c, i, 0)),
                  pl.BlockSpec((None,) * (len(widx) + 1) + (k, bn),
                               lambda c, i, j: widx + (c, 0, j))],
        out_specs=pl.BlockSpec((None, bm, bn), lambda c, i, j: (c, i, j)),
        out_shape=jax.ShapeDtypeStruct((n_batch, m, n), F32),
        compiler_params=_params("parallel", "parallel", "parallel"),
        name="matmul_rkv",
    )(a, w)


RWKV_TIME_BLOCK = 512
RWKV_PAIRS_PER_STEP = 4


def _bdot(a, b, dims=_NN):
    return lax.dot_general(a.astype(BF16), b.astype(BF16), dims, preferred_element_type=F32)


def _split_dot(a, b, split, passes):
    acc = None
    rem = (a, b)[split]
    for _ in range(passes):
        part = rem.astype(BF16)
        term = (lax.dot_general(part, b, _NN, preferred_element_type=F32) if split == 0 else
                lax.dot_general(a, part, _NN, preferred_element_type=F32))
        acc = term if acc is None else acc + term
        rem = rem - part.astype(F32)
    return acc


def _rwkv_body(r_ref, k_ref, v_ref, lw_ref, a_ref, g_ref, kk_ref, ka_ref, rk_ref, lnw_ref, lnb_ref,
               *rest, n_chunks, n_pairs, has_vres):
    L, N, W = CHUNK, HEAD, LANES
    if has_vres:
        vfirst_ref, vgate_ref, o_ref, s_ref = rest
    else:
        o_ref, s_ref = rest

    @pl.when(pl.program_id(2) == 0)
    def _():
        s_ref[...] = jnp.zeros_like(s_ref)

    def iota(shape, dim):
        return lax.broadcasted_iota(jnp.int32, shape, dim)

    same_head = (iota((W, W), 0) // N) == (iota((W, W), 1) // N)
    g_row, g_col = iota((2 * L, 2 * W), 0), iota((2 * L, 2 * W), 1)
    mask_g = (g_col % N) < (g_row % L) + (g_row >= L).astype(jnp.int32)
    n_double = max(1, (L - 1).bit_length())

    def stack(x):
        head0 = (iota(x.shape, 1) % W) < N
        zero = jnp.zeros_like(x)
        return jnp.concatenate([jnp.where(head0, x, zero), jnp.where(head0, zero, x)], axis=0)

    head0_lanes = iota((L, W), 1) < N
    step_idx = iota((L, W), 0)

    def seg_sum(x):
        s0 = jnp.sum(jnp.where(head0_lanes, x, 0.0), axis=-1, keepdims=True)
        s1 = jnp.sum(jnp.where(head0_lanes, 0.0, x), axis=-1, keepdims=True)
        return jnp.where(head0_lanes, s0, s1)

    def running_sum(x):
        shift = 1
        while shift < L:
            x = x + jnp.where(step_idx >= shift, pltpu.roll(x, shift, axis=0), 0.0)
            shift *= 2
        return x

    eye2 = ((iota((L, W), 1) % N) == iota((L, W), 0)).astype(F32)
    state = {pr: s_ref[pr] for pr in range(n_pairs)}

    def chunk_stages(c, pr):
        rows, lanes = slice(c * L, (c + 1) * L), slice(pr * W, (pr + 1) * W)
        r, k, v = r_ref[rows, lanes], k_ref[rows, lanes], v_ref[rows, lanes]
        if has_vres:
            v = v + (vfirst_ref[rows, lanes] - v) * vgate_ref[rows, lanes].astype(F32)
        lw, a = lw_ref[rows, lanes], a_ref[rows, lanes].astype(F32)
        kk_p, ka_p, rk_p = kk_ref[:, lanes], ka_ref[:, lanes], rk_ref[:, lanes]
        kkr = k * kk_p
        kh = k * (1.0 + (a - 1.0) * ka_p)
        ssq = seg_sum(kkr * kkr)
        lp = running_sum(lw)
        bonus_w = seg_sum(r * kh * rk_p)
        yield
        kk = kkr / jnp.maximum(jnp.sqrt(ssq), 1e-12)
        ba = kk * a
        lp_end = lp[L - 1:L, :]
        e_neg = jnp.exp(-lp)
        e_end = jnp.exp(lp_end - lp)
        at = -kk * jnp.exp(lp - lw)
        rt = r * jnp.exp(lp)
        v16 = v.astype(BF16)
        ar = jnp.concatenate([at, rt], axis=0)
        bk = jnp.concatenate([stack((ba * e_neg).astype(BF16)),
                              stack((kh * e_neg).astype(BF16))], axis=0)
        gm = _bdot(ar, bk, _NT)
        bk_end_t = jnp.concatenate([ba * e_end, kh * e_end], axis=0).T.astype(BF16)
        p_col = jnp.broadcast_to(jnp.exp(lp_end), (W, W)).T
        yield
        gm = jnp.where(mask_g, gm, 0.0)
        nk, gak = gm[:L, :W], gm[:L, W:]
        grb, grk = gm[L:, :W], gm[L:, W:]
        sv = stack(v16)
        nk16 = nk.astype(BF16)
        aakv = _bdot(gak, sv)
        npow = _bdot(nk16, stack(nk16))
        yield
        t = eye2 + nk
        for it in range(1, n_double):
            np16, t16 = npow.astype(BF16), t.astype(BF16)
            if it + 1 < n_double:
                p = _bdot(np16, jnp.concatenate([stack(np16), stack(t16)], axis=1))
                yield
                npow, t = p[:, :W], t + p[:, W:]
            else:
                p = _bdot(np16, stack(t16))
                yield
                t = t + p
        w16 = _bdot(t, stack(jnp.concatenate([at, aakv], axis=1).astype(BF16))).astype(BF16)
        yield
        p2 = _bdot(grb, stack(w16))
        p3 = _bdot(grk, sv)
        below = jnp.concatenate([jnp.zeros((L, W), BF16), v16], axis=1)
        mt = _bdot(bk_end_t, jnp.concatenate([w16, below], axis=0))
        yield
        rp = rt + p2[:, :W]
        yv = p2[:, W:] + p3
        zero = jnp.zeros((W, W), F32)
        m_bd_t = jnp.where(same_head, mt[:, :W], zero)
        sv_bd_t = jnp.where(same_head, mt[:, W:], zero)
        sst = state[pr]
        sst16 = sst.astype(BF16)
        ys = _bdot(rp, sst16)
        state[pr] = p_col * sst + _bdot(m_bd_t, sst16) + sv_bd_t
        yield
        y = yv + ys
        mean = seg_sum(y) * (1.0 / N)
        yield
        yc = y - mean
        var = seg_sum(yc * yc) * (1.0 / N)
        yield
        yn = yc * lax.rsqrt(var + GN_EPS) * lnw_ref[:, lanes] + lnb_ref[:, lanes]
        gate = g_ref[rows, lanes].astype(F32)
        o_ref[rows, lanes] = ((yn + bonus_w * v) * gate).astype(o_ref.dtype)

    pending = {(c, pr): chunk_stages(c, pr) for c in range(n_chunks) for pr in range(n_pairs)}
    slot = 0
    while pending:
        for key in sorted(pending):
            if key[0] <= slot and next(pending[key], True):
                del pending[key]
        slot += 1
    for pr in range(n_pairs):
        s_ref[pr] = state[pr]


def _rwkv_recurrence(rkv, v_mix, lw, a, g, k_k, k_a, r_k, ln_w, ln_b):
    _, b, t, d = rkv.shape
    tb = _tile(t, RWKV_TIME_BLOCK, CHUNK)
    width = _tile(d, RWKV_PAIRS_PER_STEP * LANES, LANES)
    tok = pl.BlockSpec((None, tb, width), lambda bi, hi, ti: (bi, ti, hi))
    par = pl.BlockSpec((1, width), lambda bi, hi, ti: (0, hi))

    def stacked(c):
        return pl.BlockSpec((None, None, tb, width), lambda bi, hi, ti: (c, bi, ti, hi))

    in_specs = [stacked(0), stacked(1), stacked(2), tok, tok, tok] + [par] * 5
    args = [rkv, rkv, rkv, lw, a, g, *(p.reshape(1, d) for p in (k_k, k_a, r_k, ln_w, ln_b))]
    if v_mix is not None:
        in_specs += [stacked(2), tok]
        args += list(v_mix)
    return pl.pallas_call(
        functools.partial(_rwkv_body, n_chunks=tb // CHUNK, n_pairs=width // LANES,
                          has_vres=v_mix is not None),
        grid=(b, d // width, t // tb),
        in_specs=in_specs,
        out_specs=tok,
        out_shape=jax.ShapeDtypeStruct((b, t, d), BF16),
        scratch_shapes=[pltpu.VMEM((width // LANES, LANES, LANES), F32)],
        compiler_params=_params("parallel", "parallel", "arbitrary"),
        name="rwkv7_recurrence",
    )(*args)


def _rwkv_layer(x, seq, v_first, norm_g, mu, w_rkv, w_o, widx, w0, w_l1, w_l2, a0, a_l1, a_l2,
                g_l1, g_l2, k_k, k_a, r_k, ln_w, ln_b, v_res, next_g):
    m, d = x.shape
    bsz = m // seq
    xmix, lw, a, g, *v_gate = _rwkv_front(x, norm_g, mu, seq, (w0, w_l1, w_l2), (a0, a_l1, a_l2),
                                          (g_l1, g_l2), v_res)
    rkv = _batched_matmul(xmix, w_rkv, widx, 3)
    sh = (bsz, seq, d)
    rkv4 = rkv.reshape((3,) + sh)
    if v_res is None:
        v_first = rkv4
        v_mix = None
    else:
        v_mix = (v_first, v_gate[0].reshape(sh))
    yg = _rwkv_recurrence(rkv4, v_mix, lw.reshape(sh), a.reshape(sh), g.reshape(sh),
                          k_k, k_a, r_k, ln_w, ln_b)
    return _resid_matmul(yg.reshape(m, d), w_o, widx, x, 1.0, next_g=next_g), v_first


S5_GROUPS_PER_STEP = 8


def _s5_body(u_ref, pwlo_ref, pwhi_ref, cc_ref, bbt_ref, rr_ref, ri_ref, dsk_ref, c1_ref, c2_ref,
             o_ref, tt_ref, *, n_chunks, state):
    L, C, P = S5_CHUNK, S5_GROUP, state
    LC = L * C
    lags_per_tile = LANES // C

    def iota(shape, dim):
        return lax.broadcasted_iota(jnp.int32, shape, dim)

    rep_lag = (iota((L, LC), 1) // C == iota((L, LC), 0)).astype(BF16)
    rep_chan = (iota((C, LANES), 1) % C == iota((C, LANES), 0)).astype(BF16)

    def rows_lag(x):
        return jnp.broadcast_to(x[:, None, :], (L, C, x.shape[1])).reshape(LC, x.shape[1])

    def group_stages(gi):
        u = u_ref[gi]
        pw_lo = _split_dot(pwlo_ref[gi], rep_lag, 0, 2)
        pw_hi = _split_dot(pwhi_ref[gi], rep_lag, 0, 1)
        cc = jnp.tile(_split_dot(cc_ref[gi], rep_chan, 0, 2), (1, LC // LANES))
        rr, ri = rows_lag(rr_ref[gi]), rows_lag(ri_ref[gi])
        bb = jnp.broadcast_to(bbt_ref[gi][None], (L, C, 2 * P)).reshape(LC, 2 * P)
        yield

        def c_times_power(pw):
            return jnp.concatenate([cc[:P] * pw[:P] - cc[P:] * pw[P:],
                                    -(cc[:P] * pw[P:] + cc[P:] * pw[:P])], axis=0)

        vt = c_times_power(pw_hi).astype(BF16)
        wt = (rr * bb + ri * pltpu.roll(bb, P, axis=1)).astype(BF16)
        taps = lax.dot_general(bbt_ref[gi], c_times_power(pw_lo), _NN,
                               precision=lax.Precision.HIGHEST,
                               preferred_element_type=F32)
        s = jnp.dot(u, wt, preferred_element_type=F32)
        yield
        on_diag = iota(taps.shape, 1) == iota(taps.shape, 0)
        taps = taps + jnp.where(on_diag, dsk_ref[gi], 0.0)
        padded = jnp.concatenate([jnp.zeros_like(taps), taps], axis=1)
        for sub in range(lags_per_tile):
            shifted = padded if sub == 0 else pltpu.roll(padded, sub * C, axis=1)
            shifted = shifted.astype(BF16)
            for q in range(L // lags_per_tile):
                lag = q * lags_per_tile + sub
                tt_ref[gi, lag * C:(lag + 1) * C, :] = (
                    shifted[:, LC - q * LANES:2 * LC - q * LANES])
        yield
        chunk_idx = iota(s.shape, 0) % n_chunks
        step = 1
        it = 0
        while step < n_chunks:
            sp = jnp.where(chunk_idx >= step, pltpu.roll(s, step, axis=0), 0.0)
            s = (s + c1_ref[gi, it:it + 1, :] * sp
                 + c2_ref[gi, it:it + 1, :] * pltpu.roll(sp, P, axis=1))
            step *= 2
            it += 1
            yield
        s_start = jnp.where(chunk_idx >= 1, pltpu.roll(s, 1, axis=0), 0.0)
        y = (jnp.dot(u, tt_ref[gi], preferred_element_type=F32)
             + jnp.dot(s_start.astype(BF16), vt, preferred_element_type=F32))
        yield
        o_ref[gi] = jax.nn.gelu(y).astype(o_ref.dtype)

    pending = [group_stages(gi) for gi in range(u_ref.shape[0])]
    while pending:
        pending = [gen for gen in pending if not next(gen, True)]


def _s5_tables(lam_re, lam_im, log_step, b_re, b_im, c_re, c_im, d_skip, n_chunks):
    g, p = lam_re.shape
    c = S5_GROUP
    L = S5_CHUNK
    lr = jnp.minimum(lam_re.astype(F32), LAMBDA_RE_MAX)
    li = lam_im.astype(F32)
    dt = jnp.exp(log_step.astype(F32))[:, None]
    ldt_re, ldt_im = lr * dt, li * dt
    mag = jnp.exp(ldt_re)
    ab_re, ab_im = mag * jnp.cos(ldt_im), mag * jnp.sin(ldt_im)
    den = lr * lr + li * li
    q_re = ((ab_re - 1.0) * lr + ab_im * li) / den
    q_im = (ab_im * lr - (ab_re - 1.0) * li) / den
    br, bi = b_re.astype(F32), b_im.astype(F32)
    bb_re = q_re[..., None] * br - q_im[..., None] * bi
    bb_im = q_re[..., None] * bi + q_im[..., None] * br
    dsk = d_skip.astype(F32).reshape(g, c)

    def power(n):
        m_ = jnp.exp(n * ldt_re)
        return m_ * jnp.cos(n * ldt_im), m_ * jnp.sin(n * ldt_im)

    lag = jnp.arange(0, L + 1, dtype=F32)[:, None, None]
    pw_re, pw_im = power(lag)
    pwt = jnp.concatenate([pw_re, pw_im], axis=-1).transpose(1, 2, 0)
    cc = jnp.concatenate([c_re.astype(F32), c_im.astype(F32)], axis=-1).transpose(0, 2, 1)
    bbt = jnp.concatenate([bb_re, bb_im], axis=1).transpose(0, 2, 1)
    rev_re = pw_re[::-1][1:].transpose(1, 0, 2)
    rev_im = pw_im[::-1][1:].transpose(1, 0, 2)
    rr = jnp.concatenate([rev_re, rev_re], axis=-1)
    ri = jnp.concatenate([-rev_im, rev_im], axis=-1)
    n_steps = max(1, (n_chunks - 1).bit_length())
    hop = (L * 2.0 ** jnp.arange(n_steps, dtype=F32))[:, None, None]
    hop_re, hop_im = power(hop)
    c1 = jnp.concatenate([hop_re, hop_re], axis=-1).transpose(1, 0, 2)
    c2 = jnp.concatenate([-hop_im, hop_im], axis=-1).transpose(1, 0, 2)
    return pwt[:, :, :L], pwt[:, :, 1:], cc, bbt, rr, ri, dsk[:, :, None], c1, c2


def _group_rows_body(*refs, rows, per_tile):
    *lag_refs, perm_ref, o_ref = refs
    n_tiles = len(lag_refs) // per_tile
    a = jnp.concatenate(
        [jnp.concatenate([lag_refs[q * per_tile + s][...] for s in range(per_tile)], axis=1)
         for q in range(n_tiles)], axis=0)
    res = jnp.dot(a, perm_ref[...], preferred_element_type=F32)
    for gi in range(per_tile):
        for q in range(n_tiles):
            o_ref[gi, :, q * LANES:(q + 1) * LANES] = (
                res[q * rows:(q + 1) * rows, gi * LANES:(gi + 1) * LANES].astype(o_ref.dtype))


def _group_rows(h, n_groups):
    m, d = h.shape
    c, L = S5_GROUP, S5_CHUNK
    rows = m // L
    per_tile = LANES // c
    n_blocks = d // LANES
    perm = _granule_swap_matrix()
    h2 = h.reshape(rows, L * d)
    lag_specs = [pl.BlockSpec((rows, LANES), lambda b, l=l: (0, l * n_blocks + b))
                 for l in range(L)]
    return pl.pallas_call(
        functools.partial(_group_rows_body, rows=rows, per_tile=per_tile),
        grid=(n_blocks,),
        in_specs=lag_specs + [pl.BlockSpec(perm.shape, lambda b: (0, 0))],
        out_specs=pl.BlockSpec((per_tile, rows, L * c), lambda b: (b, 0, 0)),
        out_shape=jax.ShapeDtypeStruct((n_groups, rows, L * c), BF16),
        compiler_params=_params("parallel"),
        name="s5_group_rows",
    )(*([h2] * L), perm)


def _ungroup_rows_body(y_ref, perm_ref, o_hbm, stage_ref, sem, *, rows, per_tile, n_blocks):
    b = pl.program_id(0)
    n_tiles = stage_ref.shape[0] // per_tile
    a = jnp.concatenate(
        [jnp.concatenate([y_ref[gi, :, q * LANES:(q + 1) * LANES] for gi in range(per_tile)],
                         axis=1) for q in range(n_tiles)], axis=0)
    res = jnp.dot(a, perm_ref[...], preferred_element_type=F32)
    copies = []
    for q in range(n_tiles):
        for s in range(per_tile):
            lag = q * per_tile + s
            stage_ref[lag] = res[q * rows:(q + 1) * rows,
                                 s * LANES:(s + 1) * LANES].astype(stage_ref.dtype)
            col = pl.multiple_of((lag * n_blocks + b) * LANES, LANES)
            copy = pltpu.make_async_copy(stage_ref.at[lag], o_hbm.at[:, pl.ds(col, LANES)],
                                         sem.at[lag])
            copy.start()
            copies.append(copy)
    for copy in copies:
        copy.wait()


def _granule_swap_matrix():
    c = S5_GROUP
    per_tile = LANES // c
    idx = jnp.arange(per_tile * LANES)
    hi, mid, lo = idx // LANES, (idx % LANES) // c, idx % c
    target = mid * LANES + hi * c + lo
    return (target[:, None] == idx[None, :]).astype(BF16)


def _ungroup_rows(y, d):
    g, rows, lc = y.shape
    c, L = S5_GROUP, S5_CHUNK
    per_tile = LANES // c
    n_blocks = d // LANES
    perm = _granule_swap_matrix()
    out = pl.pallas_call(
        functools.partial(_ungroup_rows_body, rows=rows, per_tile=per_tile, n_blocks=n_blocks),
        grid=(n_blocks,),
        in_specs=[pl.BlockSpec((per_tile, rows, lc), lambda b: (b, 0, 0)),
                  pl.BlockSpec(perm.shape, lambda b: (0, 0))],
        out_specs=pl.BlockSpec(memory_space=pl.ANY),
        out_shape=jax.ShapeDtypeStruct((rows, L * d), BF16),
        scratch_shapes=[pltpu.VMEM((L, rows, LANES), BF16), pltpu.SemaphoreType.DMA((L,))],
        compiler_params=_params("arbitrary"),
        name="s5_ungroup_rows",
    )(y, perm)
    return out.reshape(rows * L, d)


def _s5_core(h, seq, lam_re, lam_im, log_step, b_re, b_im, c_re, c_im, d_skip):
    m, d = h.shape
    g, p = lam_re.shape
    c, L = S5_GROUP, S5_CHUNK
    n_chunks = seq // L
    rows = m // L
    tables = _s5_tables(lam_re, lam_im, log_step, b_re, b_im, c_re, c_im, d_skip, n_chunks)
    u = _group_rows(h, g)
    gb = _tile(g, S5_GROUPS_PER_STEP, 1)

    def grp(shape):
        return pl.BlockSpec((gb,) + tuple(shape), lambda gi: (gi, 0, 0))

    y = pl.pallas_call(
        functools.partial(_s5_body, n_chunks=n_chunks, state=p),
        grid=(g // gb,),
        in_specs=[grp((rows, L * c))] + [grp(t.shape[1:]) for t in tables],
        out_specs=grp((rows, L * c)),
        out_shape=jax.ShapeDtypeStruct((g, rows, L * c), BF16),
        scratch_shapes=[pltpu.VMEM((gb, L * c, L * c), BF16)],
        compiler_params=_params("parallel"),
        name="s5_chunk_scan",
    )(u, *tables)
    return _ungroup_rows(y, d)


def _ffn(x, pre, w_in, w_out, widx, norm_g, next_g):
    if pre is None:
        act = _dual_matmul(_rmsnorm(x, norm_g, BF16), w_in, widx, "swiglu")
    else:
        act = _dual_matmul(pre[0], w_in, widx, "swiglu", ssq=pre[1])
    return _resid_matmul(act, w_out, widx, x, 0.5, next_g=next_g)


def kernel(x, ffn_norm, ffn_w_in, ffn_w_out, mix_norm, rwkv_mu, rwkv_w_rkv, rwkv_w_o, rwkv_w0, rwkv_w_l1, rwkv_w_l2, rwkv_a0, rwkv_a_l1, rwkv_a_l2, rwkv_v0, rwkv_v_l1, rwkv_v_l2, rwkv_g_l1, rwkv_g_l2, rwkv_k_k, rwkv_k_a, rwkv_r_k, rwkv_ln_w, rwkv_ln_b, s5_lam_re, s5_lam_im, s5_log_step, s5_b_re, s5_b_im, s5_c_re, s5_c_im, s5_d, s5_w_glu, final_norm):
    bsz, seq, d = x.shape
    depth = ffn_norm.shape[0]
    n_mixers = 2
    w_in, w_out = ffn_w_in.astype(BF16), ffn_w_out.astype(BF16)
    w_rkv, w_o, w_glu = rwkv_w_rkv.astype(BF16), rwkv_w_o.astype(BF16), s5_w_glu.astype(BF16)
    x = x.reshape(bsz * seq, d)
    v_first = None
    pre = None
    for i in range(depth):
        x = _ffn(x, pre, w_in, w_out, (i, 0), ffn_norm[i, 0], None)
        j = i // n_mixers
        if i % n_mixers == 0:
            v_res = None if j == 0 else (rwkv_v0[j - 1], rwkv_v_l1[j - 1], rwkv_v_l2[j - 1])
            (x, xg, ssq), v_first = _rwkv_layer(
                x, seq, v_first, mix_norm[i], rwkv_mu[j], w_rkv, w_o, (j,),
                rwkv_w0[j], rwkv_w_l1[j], rwkv_w_l2[j], rwkv_a0[j], rwkv_a_l1[j], rwkv_a_l2[j],
                rwkv_g_l1[j], rwkv_g_l2[j], rwkv_k_k[j], rwkv_k_a[j], rwkv_r_k[j],
                rwkv_ln_w[j], rwkv_ln_b[j], v_res, ffn_norm[i, 1])
        else:
            h = _rmsnorm(x, mix_norm[i], BF16)
            y = _s5_core(h, seq, s5_lam_re[j], s5_lam_im[j], s5_log_step[j], s5_b_re[j],
                         s5_b_im[j], s5_c_re[j], s5_c_im[j], s5_d[j])
            x, xg, ssq = _dual_matmul(y, w_glu, (j,), "glu_residual", res=x,
                                      next_g=ffn_norm[i, 1])
        if i + 1 < depth:
            x, xg, ssq = _ffn(x, (xg, ssq), w_in, w_out, (i, 1), None, ffn_norm[i + 1, 0])
            pre = (xg, ssq)
        else:
            x = _ffn(x, (xg, ssq), w_in, w_out, (i, 1), None, None)
    return _rmsnorm(x, final_norm, F32).reshape(bsz, seq, d)
```

```python
import functools
import math

import jax
import jax.numpy as jnp
from jax import lax
from jax.experimental import pallas as pl
from jax.experimental.pallas import tpu as pltpu

F32 = jnp.float32
BF16 = jnp.bfloat16

RMS_EPS = 1e-6
GN_EPS = 64e-5
LAMBDA_RE_MAX = -1e-4
HEAD = 64
CHUNK = 64
S5_CHUNK = 64
S5_GROUP = 16
LANES = 128
VMEM_LIMIT = 56 * 1024 * 1024

_NN = (((1,), (0,)), ((), ()))
_NT = (((1,), (1,)), ((), ()))
_TN = (((0,), (0,)), ((), ()))


def _params(*sem):
    return pltpu.CompilerParams(dimension_semantics=sem, vmem_limit_bytes=VMEM_LIMIT)


def _tile(n, pref, quantum):
    best = None
    t = quantum
    while t <= min(n, pref):
        if n % t == 0:
            best = t
        t += quantum
    return best if best is not None else n


def _rms_body(x_ref, g_ref, o_ref):
    x = x_ref[...]
    ms = jnp.mean(x * x, axis=-1, keepdims=True)
    o_ref[...] = (x * lax.rsqrt(ms + RMS_EPS) * g_ref[...]).astype(o_ref.dtype)


def _rmsnorm(x, g, out_dtype):
    m, d = x.shape
    bm = _tile(m, 256, 8)
    return pl.pallas_call(
        _rms_body,
        grid=(m // bm,),
        in_specs=[pl.BlockSpec((bm, d), lambda i: (i, 0)),
                  pl.BlockSpec((1, d), lambda i: (0, 0))],
        out_specs=pl.BlockSpec((bm, d), lambda i: (i, 0)),
        out_shape=jax.ShapeDtypeStruct((m, d), out_dtype),
        compiler_params=_params("parallel"),
        name="rmsnorm",
    )(x, g.reshape(1, d))


_DECAY_SCALE = math.exp(-0.5)


def _rwkv_front_body(*refs, tiles_per_seq, has_vres):
    x_ref, g_ref, mu_ref = refs[:3]
    n_lora = 11 if has_vres else 8
    lora = refs[3:3 + n_lora]
    outs = refs[3 + n_lora:-1]
    carry_ref = refs[-1]
    wl1, wl2, w0, al1, al2, a0, gl1, gl2 = lora[:8]
    xmix_ref, lw_ref, a_ref, gate_ref = outs[:4]
    i = pl.program_id(0)

    @pl.when(i % tiles_per_seq == 0)
    def _():
        carry_ref[...] = jnp.zeros_like(carry_ref)

    x = x_ref[...]
    ms = jnp.mean(x * x, axis=-1, keepdims=True)
    h = x * lax.rsqrt(ms + RMS_EPS) * g_ref[...]
    bm = h.shape[0]
    row = lax.broadcasted_iota(jnp.int32, h.shape, 0)
    prev = jnp.where(row == 0, carry_ref[...], pltpu.roll(h, 1, axis=0))
    carry_ref[...] = h[bm - 1:bm, :]
    dx = prev - h

    def mixed(c):
        return (h + dx * mu_ref[c:c + 1, :]).astype(BF16)

    def low_rank(xc, l1_ref, l2_ref, inner):
        t = inner(jnp.dot(xc, l1_ref[...], preferred_element_type=F32))
        return jnp.dot(t.astype(BF16), l2_ref[...], preferred_element_type=F32)

    xmix_ref[0] = mixed(0)
    xmix_ref[1] = mixed(1)
    xv = mixed(2)
    xmix_ref[2] = xv
    lw_ref[...] = -_DECAY_SCALE * jax.nn.sigmoid(low_rank(mixed(3), wl1, wl2, jnp.tanh) + w0[...])
    a_ref[...] = jax.nn.sigmoid(low_rank(mixed(4), al1, al2, lambda t: t)
                                + a0[...]).astype(a_ref.dtype)
    gate_ref[...] = low_rank(mixed(5), gl1, gl2, jax.nn.sigmoid).astype(gate_ref.dtype)
    if has_vres:
        vl1, vl2, v0 = lora[8:]
        outs[4][...] = jax.nn.sigmoid(low_rank(xv, vl1, vl2, lambda t: t)
                                      + v0[...]).astype(outs[4].dtype)


def _pad_rank(l1, l2):
    pad = (-l1.shape[1]) % LANES
    return (jnp.pad(l1, ((0, 0), (0, pad))).astype(BF16),
            jnp.pad(l2, ((0, pad), (0, 0))).astype(BF16))


def _rwkv_front(x, g, mu, seq, w_lora, a_lora, g_lora, v_lora):
    m, d = x.shape
    bm = _tile(seq, 128, 8)
    row = pl.BlockSpec((bm, d), lambda i: (i, 0))
    vec = pl.BlockSpec((1, d), lambda i: (0, 0))
    in_specs = [row, vec, pl.BlockSpec(mu.shape, lambda i: (0, 0))]
    args = [x, g.reshape(1, d), mu]

    def add_pair(l1, l2):
        l1, l2 = _pad_rank(l1, l2)
        in_specs.extend([pl.BlockSpec(l1.shape, lambda i: (0, 0)),
                         pl.BlockSpec(l2.shape, lambda i: (0, 0))])
        args.extend([l1, l2])

    def add_bias(b):
        in_specs.append(vec)
        args.append(b.reshape(1, d))

    w0, w_l1, w_l2 = w_lora
    a0, a_l1, a_l2 = a_lora
    add_pair(w_l1, w_l2)
    add_bias(w0)
    add_pair(a_l1, a_l2)
    add_bias(a0)
    add_pair(*g_lora)
    n_tok_out = 3
    if v_lora is not None:
        v0, v_l1, v_l2 = v_lora
        add_pair(v_l1, v_l2)
        add_bias(v0)
        n_tok_out = 4
    return pl.pallas_call(
        functools.partial(_rwkv_front_body, tiles_per_seq=seq // bm, has_vres=v_lora is not None),
        grid=(m // bm,),
        in_specs=in_specs,
        out_specs=[pl.BlockSpec((3, bm, d), lambda i: (0, i, 0))] + [row] * n_tok_out,
        out_shape=[jax.ShapeDtypeStruct((3, m, d), BF16)]
        + [jax.ShapeDtypeStruct((m, d), F32)]
        + [jax.ShapeDtypeStruct((m, d), BF16)] * (n_tok_out - 1),
        scratch_shapes=[pltpu.VMEM((1, d), F32)],
        compiler_params=_params("arbitrary"),
        name="rwkv_front",
    )(*args)


def _w_spec(widx, k, bn, col):
    return pl.BlockSpec((None,) * len(widx) + (k, bn), lambda *gi: widx + (0, col(*gi)))


def _emit_next_norm_input(x_new, g_ref, xg_ref, ssq_ref):
    xg_ref[...] = (x_new * g_ref[...]).astype(xg_ref.dtype)
    ssq_ref[...] = jnp.sum(x_new * x_new, axis=-1, keepdims=True)


def _next_norm_specs(m, n, bm, bn):
    g_spec = pl.BlockSpec((1, bn), lambda i, j: (0, j))
    out_specs = [pl.BlockSpec((bm, bn), lambda i, j: (i, j)),
                 pl.BlockSpec((None, bm, 1), lambda i, j: (j, i, 0))]
    out_shapes = [jax.ShapeDtypeStruct((m, n), BF16),
                  jax.ShapeDtypeStruct((n // bn, m, 1), F32)]
    return g_spec, out_specs, out_shapes


def _dual_body(*refs, mode, normed, emit_next, k_dim):
    refs = list(refs)
    a_ref, w1_ref, w2_ref = refs[:3]
    del refs[:3]
    a = a_ref[...]
    p1 = jnp.dot(a, w1_ref[...], preferred_element_type=F32)
    p2 = jnp.dot(a, w2_ref[...], preferred_element_type=F32)
    if not normed:
        ssq_ref = refs.pop(0)
        rstd = lax.rsqrt(jnp.sum(ssq_ref[...], axis=0) * (1.0 / k_dim) + RMS_EPS)
        p1, p2 = p1 * rstd, p2 * rstd
    if mode == "swiglu":
        (o_ref,) = refs
        o_ref[...] = (jax.nn.silu(p1) * p2).astype(o_ref.dtype)
        return
    res_ref = refs.pop(0)
    x_new = res_ref[...] + p1 * jax.nn.sigmoid(p2)
    if emit_next:
        g_ref, o_ref, xg_ref, ssq_out_ref = refs
        _emit_next_norm_input(x_new, g_ref, xg_ref, ssq_out_ref)
    else:
        (o_ref,) = refs
    o_ref[...] = x_new


def _dual_matmul(a, w, widx, mode, ssq=None, res=None, next_g=None):
    m, k = a.shape
    n = w.shape[-1] // 2
    bm = _tile(m, 1024, 8)
    bn = _tile(n, 512, LANES)
    nj = n // bn
    in_specs = [pl.BlockSpec((bm, k), lambda i, j: (i, 0)),
                _w_spec(widx, k, bn, lambda i, j: j),
                _w_spec(widx, k, bn, lambda i, j: j + nj)]
    args = [a, w, w]
    if ssq is not None:
        in_specs.append(pl.BlockSpec((ssq.shape[0], bm, 1), lambda i, j: (0, i, 0)))
        args.append(ssq)
    tile = pl.BlockSpec((bm, bn), lambda i, j: (i, j))
    out_specs, out_shapes = [tile], [jax.ShapeDtypeStruct((m, n), BF16 if mode == "swiglu" else F32)]
    if mode == "glu_residual":
        in_specs.append(tile)
        args.append(res)
        if next_g is not None:
            g_spec, extra_specs, extra_shapes = _next_norm_specs(m, n, bm, bn)
            in_specs.append(g_spec)
            args.append(next_g.reshape(1, n))
            out_specs += extra_specs
            out_shapes += extra_shapes
    out = pl.pallas_call(
        functools.partial(_dual_body, mode=mode, normed=ssq is None,
                          emit_next=next_g is not None, k_dim=k),
        grid=(m // bm, nj),
        in_specs=in_specs,
        out_specs=out_specs,
        out_shape=out_shapes,
        compiler_params=_params("parallel", "parallel"),
        name="matmul_" + mode,
    )(*args)
    return out[0] if len(out) == 1 else tuple(out)


def _resid_body(a_ref, w_ref, res_ref, *rest, scale, emit_next):
    acc = jnp.dot(a_ref[...], w_ref[...], preferred_element_type=F32)
    x_new = res_ref[...] + scale * acc
    if emit_next:
        g_ref, o_ref, xg_ref, ssq_ref = rest
        _emit_next_norm_input(x_new, g_ref, xg_ref, ssq_ref)
    else:
        (o_ref,) = rest
    o_ref[...] = x_new


def _resid_matmul(a, w, widx, res, scale, next_g=None):
    m, k = a.shape
    n = w.shape[-1]
    bm = _tile(m, 1024, 8)
    bn = _tile(n, 512, LANES)
    tile = pl.BlockSpec((bm, bn), lambda i, j: (i, j))
    in_specs = [pl.BlockSpec((bm, k), lambda i, j: (i, 0)),
                _w_spec(widx, k, bn, lambda i, j: j), tile]
    args = [a, w, res]
    out_specs, out_shapes = [tile], [jax.ShapeDtypeStruct((m, n), F32)]
    if next_g is not None:
        g_spec, extra_specs, extra_shapes = _next_norm_specs(m, n, bm, bn)
        in_specs.append(g_spec)
        args.append(next_g.reshape(1, n))
        out_specs += extra_specs
        out_shapes += extra_shapes
    out = pl.pallas_call(
        functools.partial(_resid_body, scale=scale, emit_next=next_g is not None),
        grid=(m // bm, n // bn),
        in_specs=in_specs,
        out_specs=out_specs,
        out_shape=out_shapes,
        compiler_params=_params("parallel", "parallel"),
        name="matmul_residual",
    )(*args)
    return out[0] if len(out) == 1 else tuple(out)


def _plain_body(a_ref, w_ref, o_ref):
    o_ref[...] = jnp.dot(a_ref[...], w_ref[...], preferred_element_type=F32)


def _batched_matmul(a, w, widx, n_batch):
    _, m, k = a.shape
    n = w.shape[-1]
    bm = _tile(m, 1024, 8)
    bn = _tile(n, 1024, LANES)
    return pl.pallas_call(
        _plain_body,
        grid=(n_batch, m // bm, n // bn),
        in_specs=[pl.BlockSpec((None, bm, k), lambda c, i, j: (c, i, 0)),
                  pl.BlockSpec((None,) * (len(widx) + 1) + (k, bn),
                               lambda c, i, j: widx + (c, 0, j))],
        out_specs=pl.BlockSpec((None, bm, bn), lambda c, i, j: (c, i, j)),
        out_shape=jax.ShapeDtypeStruct((n_batch, m, n), F32),
        compiler_params=_params("parallel", "parallel", "parallel"),
        name="matmul_rkv",
    )(a, w)


RWKV_TIME_BLOCK = 512
RWKV_PAIRS_PER_STEP = 4


def _bdot(a, b, dims=_NN):
    return lax.dot_general(a.astype(BF16), b.astype(BF16), dims, preferred_element_type=F32)


def _split_dot(a, b, split, passes):
    acc = None
    rem = (a, b)[split]
    for _ in range(passes):
        part = rem.astype(BF16)
        term = (lax.dot_general(part, b, _NN, preferred_element_type=F32) if split == 0 else
                lax.dot_general(a, part, _NN, preferred_element_type=F32))
        acc = term if acc is None else acc + term
        rem = rem - part.astype(F32)
    return acc


def _rwkv_body(r_ref, k_ref, v_ref, lw_ref, a_ref, g_ref, kk_ref, ka_ref, rk_ref, lnw_ref, lnb_ref,
               *rest, n_chunks, n_pairs, has_vres):
    L, N, W = CHUNK, HEAD, LANES
    if has_vres:
        vfirst_ref, vgate_ref, o_ref, s_ref = rest
    else:
        o_ref, s_ref = rest

    @pl.when(pl.program_id(2) == 0)
    def _():
        s_ref[...] = jnp.zeros_like(s_ref)

    def iota(shape, dim):
        return lax.broadcasted_iota(jnp.int32, shape, dim)

    same_head = (iota((W, W), 0) // N) == (iota((W, W), 1) // N)
    g_row, g_col = iota((2 * L, 2 * W), 0), iota((2 * L, 2 * W), 1)
    mask_g = (g_col % N) < (g_row % L) + (g_row >= L).astype(jnp.int32)
    n_double = max(1, (L - 1).bit_length())

    def stack(x):
        head0 = (iota(x.shape, 1) % W) < N
        zero = jnp.zeros_like(x)
        return jnp.concatenate([jnp.where(head0, x, zero), jnp.where(head0, zero, x)], axis=0)

    head0_lanes = iota((L, W), 1) < N
    step_idx = iota((L, W), 0)

    def seg_sum(x):
        s0 = jnp.sum(jnp.where(head0_lanes, x, 0.0), axis=-1, keepdims=True)
        s1 = jnp.sum(jnp.where(head0_lanes, 0.0, x), axis=-1, keepdims=True)
        return jnp.where(head0_lanes, s0, s1)

    def running_sum(x):
        shift = 1
        while shift < L:
            x = x + jnp.where(step_idx >= shift, pltpu.roll(x, shift, axis=0), 0.0)
            shift *= 2
        return x

    eye2 = ((iota((L, W), 1) % N) == iota((L, W), 0)).astype(F32)
    state = {pr: s_ref[pr] for pr in range(n_pairs)}

    def chunk_stages(c, pr):
        rows, lanes = slice(c * L, (c + 1) * L), slice(pr * W, (pr + 1) * W)
        r, k, v = r_ref[rows, lanes], k_ref[rows, lanes], v_ref[rows, lanes]
        if has_vres:
            v = v + (vfirst_ref[rows, lanes] - v) * vgate_ref[rows, lanes].astype(F32)
        lw, a = lw_ref[rows, lanes], a_ref[rows, lanes].astype(F32)
        kk_p, ka_p, rk_p = kk_ref[:, lanes], ka_ref[:, lanes], rk_ref[:, lanes]
        kkr = k * kk_p
        kh = k * (1.0 + (a - 1.0) * ka_p)
        ssq = seg_sum(kkr * kkr)
        lp = running_sum(lw)
        bonus_w = seg_sum(r * kh * rk_p)
        yield
        kk = kkr / jnp.maximum(jnp.sqrt(ssq), 1e-12)
        ba = kk * a
        lp_end = lp[L - 1:L, :]
        e_neg = jnp.exp(-lp)
        e_end = jnp.exp(lp_end - lp)
        at = -kk * jnp.exp(lp - lw)
        rt = r * jnp.exp(lp)
        v16 = v.astype(BF16)
        ar = jnp.concatenate([at, rt], axis=0)
        bk = jnp.concatenate([stack((ba * e_neg).astype(BF16)),
                              stack((kh * e_neg).astype(BF16))], axis=0)
        gm = _bdot(ar, bk, _NT)
        bk_end_t = jnp.concatenate([ba * e_end, kh * e_end], axis=0).T.astype(BF16)
        p_col = jnp.broadcast_to(jnp.exp(lp_end), (W, W)).T
        yield
        gm = jnp.where(mask_g, gm, 0.0)
        nk, gak = gm[:L, :W], gm[:L, W:]
        grb, grk = gm[L:, :W], gm[L:, W:]
        sv = stack(v16)
        nk16 = nk.astype(BF16)
        aakv = _bdot(gak, sv)
        npow = _bdot(nk16, stack(nk16))
        yield
        t = eye2 + nk
        for it in range(1, n_double):
            np16, t16 = npow.astype(BF16), t.astype(BF16)
            if it + 1 < n_double:
                p = _bdot(np16, jnp.concatenate([stack(np16), stack(t16)], axis=1))
                yield
                npow, t = p[:, :W], t + p[:, W:]
            else:
                p = _bdot(np16, stack(t16))
                yield
                t = t + p
        w16 = _bdot(t, stack(jnp.concatenate([at, aakv], axis=1).astype(BF16))).astype(BF16)
        yield
        p2 = _bdot(grb, stack(w16))
        p3 = _bdot(grk, sv)
        below = jnp.concatenate([jnp.zeros((L, W), BF16), v16], axis=1)
        mt = _bdot(bk_end_t, jnp.concatenate([w16, below], axis=0))
        yield
        rp = rt + p2[:, :W]
        yv = p2[:, W:] + p3
        zero = jnp.zeros((W, W), F32)
        m_bd_t = jnp.where(same_head, mt[:, :W], zero)
        sv_bd_t = jnp.where(same_head, mt[:, W:], zero)
        sst = state[pr]
        sst16 = sst.astype(BF16)
        ys = _bdot(rp, sst16)
        state[pr] = p_col * sst + _bdot(m_bd_t, sst16) + sv_bd_t
        yield
        y = yv + ys
        mean = seg_sum(y) * (1.0 / N)
        yield
        yc = y - mean
        var = seg_sum(yc * yc) * (1.0 / N)
        yield
        yn = yc * lax.rsqrt(var + GN_EPS) * lnw_ref[:, lanes] + lnb_ref[:, lanes]
        gate = g_ref[rows, lanes].astype(F32)
        o_ref[rows, lanes] = ((yn + bonus_w * v) * gate).astype(o_ref.dtype)

    pending = {(c, pr): chunk_stages(c, pr) for c in range(n_chunks) for pr in range(n_pairs)}
    slot = 0
    while pending:
        for key in sorted(pending):
            if key[0] <= slot and next(pending[key], True):
                del pending[key]
        slot += 1
    for pr in range(n_pairs):
        s_ref[pr] = state[pr]


def _rwkv_recurrence(rkv, v_mix, lw, a, g, k_k, k_a, r_k, ln_w, ln_b):
    _, b, t, d = rkv.shape
    tb = _tile(t, RWKV_TIME_BLOCK, CHUNK)
    width = _tile(d, RWKV_PAIRS_PER_STEP * LANES, LANES)
    tok = pl.BlockSpec((None, tb, width), lambda bi, hi, ti: (bi, ti, hi))
    par = pl.BlockSpec((1, width), lambda bi, hi, ti: (0, hi))

    def stacked(c):
        return pl.BlockSpec((None, None, tb, width), lambda bi, hi, ti: (c, bi, ti, hi))

    in_specs = [stacked(0), stacked(1), stacked(2), tok, tok, tok] + [par] * 5
    args = [rkv, rkv, rkv, lw, a, g, *(p.reshape(1, d) for p in (k_k, k_a, r_k, ln_w, ln_b))]
    if v_mix is not None:
        in_specs += [stacked(2), tok]
        args += list(v_mix)
    return pl.pallas_call(
        functools.partial(_rwkv_body, n_chunks=tb // CHUNK, n_pairs=width // LANES,
                          has_vres=v_mix is not None),
        grid=(b, d // width, t // tb),
        in_specs=in_specs,
        out_specs=tok,
        out_shape=jax.ShapeDtypeStruct((b, t, d), BF16),
        scratch_shapes=[pltpu.VMEM((width // LANES, LANES, LANES), F32)],
        compiler_params=_params("parallel", "parallel", "arbitrary"),
        name="rwkv7_recurrence",
    )(*args)


def _rwkv_layer(x, seq, v_first, norm_g, mu, w_rkv, w_o, widx, w0, w_l1, w_l2, a0, a_l1, a_l2,
                g_l1, g_l2, k_k, k_a, r_k, ln_w, ln_b, v_res, next_g):
    m, d = x.shape
    bsz = m // seq
    xmix, lw, a, g, *v_gate = _rwkv_front(x, norm_g, mu, seq, (w0, w_l1, w_l2), (a0, a_l1, a_l2),
                                          (g_l1, g_l2), v_res)
    rkv = _batched_matmul(xmix, w_rkv, widx, 3)
    sh = (bsz, seq, d)
    rkv4 = rkv.reshape((3,) + sh)
    if v_res is None:
        v_first = rkv4
        v_mix = None
    else:
        v_mix = (v_first, v_gate[0].reshape(sh))
    yg = _rwkv_recurrence(rkv4, v_mix, lw.reshape(sh), a.reshape(sh), g.reshape(sh),
                          k_k, k_a, r_k, ln_w, ln_b)
    return _resid_matmul(yg.reshape(m, d), w_o, widx, x, 1.0, next_g=next_g), v_first


S5_GROUPS_PER_STEP = 8


def _s5_body(u_ref, pwlo_ref, pwhi_ref, cc_ref, bbt_ref, rr_ref, ri_ref, dsk_ref, c1_ref, c2_ref,
             o_ref, tt_ref, *, n_chunks, state):
    L, C, P = S5_CHUNK, S5_GROUP, state
    LC = L * C
    lags_per_tile = LANES // C

    def iota(shape, dim):
        return lax.broadcasted_iota(jnp.int32, shape, dim)

    rep_lag = (iota((L, LC), 1) // C == iota((L, LC), 0)).astype(BF16)
    rep_chan = (iota((C, LANES), 1) % C == iota((C, LANES), 0)).astype(BF16)

    def rows_lag(x):
        return jnp.broadcast_to(x[:, None, :], (L, C, x.shape[1])).reshape(LC, x.shape[1])

    def group_stages(gi):
        u = u_ref[gi]
        pw_lo = _split_dot(pwlo_ref[gi], rep_lag, 0, 2)
        pw_hi = _split_dot(pwhi_ref[gi], rep_lag, 0, 1)
        cc = jnp.tile(_split_dot(cc_ref[gi], rep_chan, 0, 2), (1, LC // LANES))
        rr, ri = rows_lag(rr_ref[gi]), rows_lag(ri_ref[gi])
        bb = jnp.broadcast_to(bbt_ref[gi][None], (L, C, 2 * P)).reshape(LC, 2 * P)
        yield

        def c_times_power(pw):
            return jnp.concatenate([cc[:P] * pw[:P] - cc[P:] * pw[P:],
                                    -(cc[:P] * pw[P:] + cc[P:] * pw[:P])], axis=0)

        vt = c_times_power(pw_hi).astype(BF16)
        wt = (rr * bb + ri * pltpu.roll(bb, P, axis=1)).astype(BF16)
        taps = lax.dot_general(bbt_ref[gi], c_times_power(pw_lo), _NN,
                               precision=lax.Precision.HIGHEST,
                               preferred_element_type=F32)
        s = jnp.dot(u, wt, preferred_element_type=F32)
        yield
        on_diag = iota(taps.shape, 1) == iota(taps.shape, 0)
        taps = taps + jnp.where(on_diag, dsk_ref[gi], 0.0)
        padded = jnp.concatenate([jnp.zeros_like(taps), taps], axis=1)
        for sub in range(lags_per_tile):
            shifted = padded if sub == 0 else pltpu.roll(padded, sub * C, axis=1)
            shifted = shifted.astype(BF16)
            for q in range(L // lags_per_tile):
                lag = q * lags_per_tile + sub
                tt_ref[gi, lag * C:(lag + 1) * C, :] = (
                    shifted[:, LC - q * LANES:2 * LC - q * LANES])
        yield
        chunk_idx = iota(s.shape, 0) % n_chunks
        step = 1
        it = 0
        while step < n_chunks:
            sp = jnp.where(chunk_idx >= step, pltpu.roll(s, step, axis=0), 0.0)
            s = (s + c1_ref[gi, it:it + 1, :] * sp
                 + c2_ref[gi, it:it + 1, :] * pltpu.roll(sp, P, axis=1))
            step *= 2
            it += 1
            yield
        s_start = jnp.where(chunk_idx >= 1, pltpu.roll(s, 1, axis=0), 0.0)
        y = (jnp.dot(u, tt_ref[gi], preferred_element_type=F32)
             + jnp.dot(s_start.astype(BF16), vt, preferred_element_type=F32))
        yield
        o_ref[gi] = jax.nn.gelu(y).astype(o_ref.dtype)

    pending = [group_stages(gi) for gi in range(u_ref.shape[0])]
    while pending:
        pending = [gen for gen in pending if not next(gen, True)]


def _s5_tables(lam_re, lam_im, log_step, b_re, b_im, c_re, c_im, d_skip, n_chunks):
    g, p = lam_re.shape
    c = S5_GROUP
    L = S5_CHUNK
    lr = jnp.minimum(lam_re.astype(F32), LAMBDA_RE_MAX)
    li = lam_im.astype(F32)
    dt = jnp.exp(log_step.astype(F32))[:, None]
    ldt_re, ldt_im = lr * dt, li * dt
    mag = jnp.exp(ldt_re)
    ab_re, ab_im = mag * jnp.cos(ldt_im), mag * jnp.sin(ldt_im)
    den = lr * lr + li * li
    q_re = ((ab_re - 1.0) * lr + ab_im * li) / den
    q_im = (ab_im * lr - (ab_re - 1.0) * li) / den
    br, bi = b_re.astype(F32), b_im.astype(F32)
    bb_re = q_re[..., None] * br - q_im[..., None] * bi
    bb_im = q_re[..., None] * bi + q_im[..., None] * br
    dsk = d_skip.astype(F32).reshape(g, c)

    def power(n):
        m_ = jnp.exp(n * ldt_re)
        return m_ * jnp.cos(n * ldt_im), m_ * jnp.sin(n * ldt_im)

    lag = jnp.arange(0, L + 1, dtype=F32)[:, None, None]
    pw_re, pw_im = power(lag)
    pwt = jnp.concatenate([pw_re, pw_im], axis=-1).transpose(1, 2, 0)
    cc = jnp.concatenate([c_re.astype(F32), c_im.astype(F32)], axis=-1).transpose(0, 2, 1)
    bbt = jnp.concatenate([bb_re, bb_im], axis=1).transpose(0, 2, 1)
    rev_re = pw_re[::-1][1:].transpose(1, 0, 2)
    rev_im = pw_im[::-1][1:].transpose(1, 0, 2)
    rr = jnp.concatenate([rev_re, rev_re], axis=-1)
    ri = jnp.concatenate([-rev_im, rev_im], axis=-1)
    n_steps = max(1, (n_chunks - 1).bit_length())
    hop = (L * 2.0 ** jnp.arange(n_steps, dtype=F32))[:, None, None]
    hop_re, hop_im = power(hop)
    c1 = jnp.concatenate([hop_re, hop_re], axis=-1).transpose(1, 0, 2)
    c2 = jnp.concatenate([-hop_im, hop_im], axis=-1).transpose(1, 0, 2)
    return pwt[:, :, :L], pwt[:, :, 1:], cc, bbt, rr, ri, dsk[:, :, None], c1, c2


def _group_rows_body(h_ref, perm_ref, o_ref, *, rows, per_tile):
    lags = [h_ref[:, lag, :] for lag in range(h_ref.shape[1])]
    n_tiles = len(lags) // per_tile
    a = jnp.concatenate(
        [jnp.concatenate(lags[q * per_tile:(q + 1) * per_tile], axis=1) for q in range(n_tiles)],
        axis=0)
    res = jnp.dot(a, perm_ref[...], preferred_element_type=F32)
    for gi in range(per_tile):
        for q in range(n_tiles):
            o_ref[gi, :, q * LANES:(q + 1) * LANES] = (
                res[q * rows:(q + 1) * rows, gi * LANES:(gi + 1) * LANES].astype(o_ref.dtype))


def _group_rows(h, n_groups):
    m, d = h.shape
    c, L = S5_GROUP, S5_CHUNK
    rows = m // L
    per_tile = LANES // c
    perm = _granule_swap_matrix()
    return pl.pallas_call(
        functools.partial(_group_rows_body, rows=rows, per_tile=per_tile),
        grid=(d // LANES,),
        in_specs=[pl.BlockSpec((rows, L, LANES), lambda b: (0, 0, b)),
                  pl.BlockSpec(perm.shape, lambda b: (0, 0))],
        out_specs=pl.BlockSpec((per_tile, rows, L * c), lambda b: (b, 0, 0)),
        out_shape=jax.ShapeDtypeStruct((n_groups, rows, L * c), BF16),
        compiler_params=_params("parallel"),
        name="s5_group_rows",
    )(h.reshape(rows, L, d), perm)


def _ungroup_rows_body(y_ref, perm_ref, o_ref, *, rows, per_tile):
    n_tiles = o_ref.shape[1] // per_tile
    a = jnp.concatenate(
        [jnp.concatenate([y_ref[gi, :, q * LANES:(q + 1) * LANES] for gi in range(per_tile)],
                         axis=1) for q in range(n_tiles)], axis=0)
    res = jnp.dot(a, perm_ref[...], preferred_element_type=F32)
    for q in range(n_tiles):
        for s in range(per_tile):
            o_ref[:, q * per_tile + s, :] = (
                res[q * rows:(q + 1) * rows, s * LANES:(s + 1) * LANES].astype(o_ref.dtype))


def _granule_swap_matrix():
    c = S5_GROUP
    per_tile = LANES // c
    idx = jnp.arange(per_tile * LANES)
    hi, mid, lo = idx // LANES, (idx % LANES) // c, idx % c
    target = mid * LANES + hi * c + lo
    return (target[:, None] == idx[None, :]).astype(BF16)


def _ungroup_rows(y, d):
    g, rows, lc = y.shape
    c, L = S5_GROUP, S5_CHUNK
    per_tile = LANES // c
    perm = _granule_swap_matrix()
    out = pl.pallas_call(
        functools.partial(_ungroup_rows_body, rows=rows, per_tile=per_tile),
        grid=(d // LANES,),
        in_specs=[pl.BlockSpec((per_tile, rows, lc), lambda b: (b, 0, 0)),
                  pl.BlockSpec(perm.shape, lambda b: (0, 0))],
        out_specs=pl.BlockSpec((rows, L, LANES), lambda b: (0, 0, b)),
        out_shape=jax.ShapeDtypeStruct((rows, L, d), BF16),
        compiler_params=_params("parallel"),
        name="s5_ungroup_rows",
    )(y, perm)
    return out.reshape(rows * L, d)


def _s5_core(h, seq, lam_re, lam_im, log_step, b_re, b_im, c_re, c_im, d_skip):
    m, d = h.shape
    g, p = lam_re.shape
    c, L = S5_GROUP, S5_CHUNK
    n_chunks = seq // L
    rows = m // L
    tables = _s5_tables(lam_re, lam_im, log_step, b_re, b_im, c_re, c_im, d_skip, n_chunks)
    u = _group_rows(h, g)
    gb = _tile(g, S5_GROUPS_PER_STEP, 1)

    def grp(shape):
        return pl.BlockSpec((gb,) + tuple(shape), lambda gi: (gi, 0, 0))

    y = pl.pallas_call(
        functools.partial(_s5_body, n_chunks=n_chunks, state=p),
        grid=(g // gb,),
        in_specs=[grp((rows, L * c))] + [grp(t.shape[1:]) for t in tables],
        out_specs=grp((rows, L * c)),
        out_shape=jax.ShapeDtypeStruct((g, rows, L * c), BF16),
        scratch_shapes=[pltpu.VMEM((gb, L * c, L * c), BF16)],
        compiler_params=_params("parallel"),
        name="s5_chunk_scan",
    )(u, *tables)
    return _ungroup_rows(y, d)


def _ffn(x, pre, w_in, w_out, widx, norm_g, next_g):
    if pre is None:
        act = _dual_matmul(_rmsnorm(x, norm_g, BF16), w_in, widx, "swiglu")
    else:
        act = _dual_matmul(pre[0], w_in, widx, "swiglu", ssq=pre[1])
    return _resid_matmul(act, w_out, widx, x, 0.5, next_g=next_g)


def kernel(x, ffn_norm, ffn_w_in, ffn_w_out, mix_norm, rwkv_mu, rwkv_w_rkv, rwkv_w_o, rwkv_w0, rwkv_w_l1, rwkv_w_l2, rwkv_a0, rwkv_a_l1, rwkv_a_l2, rwkv_v0, rwkv_v_l1, rwkv_v_l2, rwkv_g_l1, rwkv_g_l2, rwkv_k_k, rwkv_k_a, rwkv_r_k, rwkv_ln_w, rwkv_ln_b, s5_lam_re, s5_lam_im, s5_log_step, s5_b_re, s5_b_im, s5_c_re, s5_c_im, s5_d, s5_w_glu, final_norm):
    bsz, seq, d = x.shape
    depth = ffn_norm.shape[0]
    n_mixers = 2
    w_in, w_out = ffn_w_in.astype(BF16), ffn_w_out.astype(BF16)
    w_rkv, w_o, w_glu = rwkv_w_rkv.astype(BF16), rwkv_w_o.astype(BF16), s5_w_glu.astype(BF16)
    x = x.reshape(bsz * seq, d)
    v_first = None
    pre = None
    for i in range(depth):
        x = _ffn(x, pre, w_in, w_out, (i, 0), ffn_norm[i, 0], None)
        j = i // n_mixers
        if i % n_mixers == 0:
            v_res = None if j == 0 else (rwkv_v0[j - 1], rwkv_v_l1[j - 1], rwkv_v_l2[j - 1])
            (x, xg, ssq), v_first = _rwkv_layer(
                x, seq, v_first, mix_norm[i], rwkv_mu[j], w_rkv, w_o, (j,),
                rwkv_w0[j], rwkv_w_l1[j], rwkv_w_l2[j], rwkv_a0[j], rwkv_a_l1[j], rwkv_a_l2[j],
                rwkv_g_l1[j], rwkv_g_l2[j], rwkv_k_k[j], rwkv_k_a[j], rwkv_r_k[j],
                rwkv_ln_w[j], rwkv_ln_b[j], v_res, ffn_norm[i, 1])
        else:
            h = _rmsnorm(x, mix_norm[i], BF16)
            y = _s5_core(h, seq, s5_lam_re[j], s5_lam_im[j], s5_log_step[j], s5_b_re[j],
                         s5_b_im[j], s5_c_re[j], s5_c_im[j], s5_d[j])
            x, xg, ssq = _dual_matmul(y, w_glu, (j,), "glu_residual", res=x,
                                      next_g=ffn_norm[i, 1])
        if i + 1 < depth:
            x, xg, ssq = _ffn(x, (xg, ssq), w_in, w_out, (i, 1), None, ffn_norm[i + 1, 0])
            pre = (xg, ssq)
        else:
            x = _ffn(x, (xg, ssq), w_in, w_out, (i, 1), None, None)
    return _rmsnorm(x, final_norm, F32).reshape(bsz, seq, d)
```

```python
import functools
import math

import jax
import jax.numpy as jnp
from jax import lax
from jax.experimental import pallas as pl
from jax.experimental.pallas import tpu as pltpu

F32 = jnp.float32
BF16 = jnp.bfloat16

RMS_EPS = 1e-6
GN_EPS = 64e-5
LAMBDA_RE_MAX = -1e-4
HEAD = 64
CHUNK = 64
S5_CHUNK = 64
S5_GROUP = 16
LANES = 128
VMEM_LIMIT = 56 * 1024 * 1024

_NN = (((1,), (0,)), ((), ()))
_NT = (((1,), (1,)), ((), ()))
_TN = (((0,), (0,)), ((), ()))


def _params(*sem):
    return pltpu.CompilerParams(dimension_semantics=sem, vmem_limit_bytes=VMEM_LIMIT)


def _tile(n, pref, quantum):
    best = None
    t = quantum
    while t <= min(n, pref):
        if n % t == 0:
            best = t
        t += quantum
    return best if best is not None else n


def _rms_body(x_ref, g_ref, o_ref):
    x = x_ref[...]
    ms = jnp.mean(x * x, axis=-1, keepdims=True)
    o_ref[...] = (x * lax.rsqrt(ms + RMS_EPS) * g_ref[...]).astype(o_ref.dtype)


def _rmsnorm(x, g, out_dtype):
    m, d = x.shape
    bm = _tile(m, 256, 8)
    return pl.pallas_call(
        _rms_body,
        grid=(m // bm,),
        in_specs=[pl.BlockSpec((bm, d), lambda i: (i, 0)),
                  pl.BlockSpec((1, d), lambda i: (0, 0))],
        out_specs=pl.BlockSpec((bm, d), lambda i: (i, 0)),
        out_shape=jax.ShapeDtypeStruct((m, d), out_dtype),
        compiler_params=_params("parallel"),
        name="rmsnorm",
    )(x, g.reshape(1, d))


_DECAY_SCALE = math.exp(-0.5)


def _rwkv_front_body(*refs, tiles_per_seq, has_vres):
    x_ref, g_ref, mu_ref = refs[:3]
    n_lora = 11 if has_vres else 8
    lora = refs[3:3 + n_lora]
    outs = refs[3 + n_lora:-1]
    carry_ref = refs[-1]
    wl1, wl2, w0, al1, al2, a0, gl1, gl2 = lora[:8]
    xmix_ref, lw_ref, a_ref, gate_ref = outs[:4]
    i = pl.program_id(0)

    @pl.when(i % tiles_per_seq == 0)
    def _():
        carry_ref[...] = jnp.zeros_like(carry_ref)

    x = x_ref[...]
    ms = jnp.mean(x * x, axis=-1, keepdims=True)
    h = x * lax.rsqrt(ms + RMS_EPS) * g_ref[...]
    bm = h.shape[0]
    row = lax.broadcasted_iota(jnp.int32, h.shape, 0)
    prev = jnp.where(row == 0, carry_ref[...], pltpu.roll(h, 1, axis=0))
    carry_ref[...] = h[bm - 1:bm, :]
    dx = prev - h

    def mixed(c):
        return (h + dx * mu_ref[c:c + 1, :]).astype(BF16)

    def low_rank(xc, l1_ref, l2_ref, inner):
        t = inner(jnp.dot(xc, l1_ref[...], preferred_element_type=F32))
        return jnp.dot(t.astype(BF16), l2_ref[...], preferred_element_type=F32)

    xmix_ref[0] = mixed(0)
    xmix_ref[1] = mixed(1)
    xv = mixed(2)
    xmix_ref[2] = xv
    lw_ref[...] = -_DECAY_SCALE * jax.nn.sigmoid(low_rank(mixed(3), wl1, wl2, jnp.tanh) + w0[...])
    a_ref[...] = jax.nn.sigmoid(low_rank(mixed(4), al1, al2, lambda t: t)
                                + a0[...]).astype(a_ref.dtype)
    gate_ref[...] = low_rank(mixed(5), gl1, gl2, jax.nn.sigmoid).astype(gate_ref.dtype)
    if has_vres:
        vl1, vl2, v0 = lora[8:]
        outs[4][...] = jax.nn.sigmoid(low_rank(xv, vl1, vl2, lambda t: t)
                                      + v0[...]).astype(outs[4].dtype)


def _pad_rank(l1, l2):
    pad = (-l1.shape[1]) % LANES
    return (jnp.pad(l1, ((0, 0), (0, pad))).astype(BF16),
            jnp.pad(l2, ((0, pad), (0, 0))).astype(BF16))


def _rwkv_front(x, g, mu, seq, w_lora, a_lora, g_lora, v_lora):
    m, d = x.shape
    bm = _tile(seq, 128, 8)
    row = pl.BlockSpec((bm, d), lambda i: (i, 0))
    vec = pl.BlockSpec((1, d), lambda i: (0, 0))
    in_specs = [row, vec, pl.BlockSpec(mu.shape, lambda i: (0, 0))]
    args = [x, g.reshape(1, d), mu]

    def add_pair(l1, l2):
        l1, l2 = _pad_rank(l1, l2)
        in_specs.extend([pl.BlockSpec(l1.shape, lambda i: (0, 0)),
                         pl.BlockSpec(l2.shape, lambda i: (0, 0))])
        args.extend([l1, l2])

    def add_bias(b):
        in_specs.append(vec)
        args.append(b.reshape(1, d))

    w0, w_l1, w_l2 = w_lora
    a0, a_l1, a_l2 = a_lora
    add_pair(w_l1, w_l2)
    add_bias(w0)
    add_pair(a_l1, a_l2)
    add_bias(a0)
    add_pair(*g_lora)
    n_tok_out = 3
    if v_lora is not None:
        v0, v_l1, v_l2 = v_lora
        add_pair(v_l1, v_l2)
        add_bias(v0)
        n_tok_out = 4
    return pl.pallas_call(
        functools.partial(_rwkv_front_body, tiles_per_seq=seq // bm, has_vres=v_lora is not None),
        grid=(m // bm,),
        in_specs=in_specs,
        out_specs=[pl.BlockSpec((3, bm, d), lambda i: (0, i, 0))] + [row] * n_tok_out,
        out_shape=[jax.ShapeDtypeStruct((3, m, d), BF16)]
        + [jax.ShapeDtypeStruct((m, d), F32)]
        + [jax.ShapeDtypeStruct((m, d), BF16)] * (n_tok_out - 1),
        scratch_shapes=[pltpu.VMEM((1, d), F32)],
        compiler_params=_params("arbitrary"),
        name="rwkv_front",
    )(*args)


def _w_spec(widx, k, bn, col):
    return pl.BlockSpec((None,) * len(widx) + (k, bn), lambda *gi: widx + (0, col(*gi)))


def _emit_next_norm_input(x_new, g_ref, xg_ref, ssq_ref):
    xg_ref[...] = (x_new * g_ref[...]).astype(xg_ref.dtype)
    ssq_ref[...] = jnp.sum(x_new * x_new, axis=-1, keepdims=True)


def _next_norm_specs(m, n, bm, bn):
    g_spec = pl.BlockSpec((1, bn), lambda i, j: (0, j))
    out_specs = [pl.BlockSpec((bm, bn), lambda i, j: (i, j)),
                 pl.BlockSpec((None, bm, 1), lambda i, j: (j, i, 0))]
    out_shapes = [jax.ShapeDtypeStruct((m, n), BF16),
                  jax.ShapeDtypeStruct((n // bn, m, 1), F32)]
    return g_spec, out_specs, out_shapes


def _dual_body(*refs, mode, normed, emit_next, k_dim):
    refs = list(refs)
    a_ref, w1_ref, w2_ref = refs[:3]
    del refs[:3]
    a = a_ref[...]
    p1 = jnp.dot(a, w1_ref[...], preferred_element_type=F32)
    p2 = jnp.dot(a, w2_ref[...], preferred_element_type=F32)
    if not normed:
        ssq_ref = refs.pop(0)
        rstd = lax.rsqrt(jnp.sum(ssq_ref[...], axis=0) * (1.0 / k_dim) + RMS_EPS)
        p1, p2 = p1 * rstd, p2 * rstd
    if mode == "swiglu":
        (o_ref,) = refs
        o_ref[...] = (jax.nn.silu(p1) * p2).astype(o_ref.dtype)
        return
    res_ref = refs.pop(0)
    x_new = res_ref[...] + p1 * jax.nn.sigmoid(p2)
    if emit_next:
        g_ref, o_ref, xg_ref, ssq_out_ref = refs
        _emit_next_norm_input(x_new, g_ref, xg_ref, ssq_out_ref)
    else:
        (o_ref,) = refs
    o_ref[...] = x_new


def _dual_matmul(a, w, widx, mode, ssq=None, res=None, next_g=None):
    m, k = a.shape
    n = w.shape[-1] // 2
    bm = _tile(m, 1024, 8)
    bn = _tile(n, 512, LANES)
    nj = n // bn
    in_specs = [pl.BlockSpec((bm, k), lambda i, j: (i, 0)),
                _w_spec(widx, k, bn, lambda i, j: j),
                _w_spec(widx, k, bn, lambda i, j: j + nj)]
    args = [a, w, w]
    if ssq is not None:
        in_specs.append(pl.BlockSpec((ssq.shape[0], bm, 1), lambda i, j: (0, i, 0)))
        args.append(ssq)
    tile = pl.BlockSpec((bm, bn), lambda i, j: (i, j))
    out_specs, out_shapes = [tile], [jax.ShapeDtypeStruct((m, n), BF16 if mode == "swiglu" else F32)]
    if mode == "glu_residual":
        in_specs.append(tile)
        args.append(res)
        if next_g is not None:
            g_spec, extra_specs, extra_shapes = _next_norm_specs(m, n, bm, bn)
            in_specs.append(g_spec)
            args.append(next_g.reshape(1, n))
            out_specs += extra_specs
            out_shapes += extra_shapes
    out = pl.pallas_call(
        functools.partial(_dual_body, mode=mode, normed=ssq is None,
                          emit_next=next_g is not None, k_dim=k),
        grid=(m // bm, nj),
        in_specs=in_specs,
        out_specs=out_specs,
        out_shape=out_shapes,
        compiler_params=_params("parallel", "parallel"),
        name="matmul_" + mode,
    )(*args)
    return out[0] if len(out) == 1 else tuple(out)


def _resid_body(a_ref, w_ref, res_ref, *rest, scale, emit_next):
    acc = jnp.dot(a_ref[...], w_ref[...], preferred_element_type=F32)
    x_new = res_ref[...] + scale * acc
    if emit_next:
        g_ref, o_ref, xg_ref, ssq_ref = rest
        _emit_next_norm_input(x_new, g_ref, xg_ref, ssq_ref)
    else:
        (o_ref,) = rest
    o_ref[...] = x_new


def _resid_matmul(a, w, widx, res, scale, next_g=None):
    m, k = a.shape
    n = w.shape[-1]
    bm = _tile(m, 1024, 8)
    bn = _tile(n, 512, LANES)
    tile = pl.BlockSpec((bm, bn), lambda i, j: (i, j))
    in_specs = [pl.BlockSpec((bm, k), lambda i, j: (i, 0)),
                _w_spec(widx, k, bn, lambda i, j: j), tile]
    args = [a, w, res]
    out_specs, out_shapes = [tile], [jax.ShapeDtypeStruct((m, n), F32)]
    if next_g is not None:
        g_spec, extra_specs, extra_shapes = _next_norm_specs(m, n, bm, bn)
        in_specs.append(g_spec)
        args.append(next_g.reshape(1, n))
        out_specs += extra_specs
        out_shapes += extra_shapes
    out = pl.pallas_call(
        functools.partial(_resid_body, scale=scale, emit_next=next_g is not None),
        grid=(m // bm, n // bn),
        in_specs=in_specs,
        out_specs=out_specs,
        out_shape=out_shapes,
        compiler_params=_params("parallel", "parallel"),
        name="matmul_residual",
    )(*args)
    return out[0] if len(out) == 1 else tuple(out)


def _plain_body(a_ref, w_ref, o_ref):
    o_ref[...] = jnp.dot(a_ref[...], w_ref[...], preferred_element_type=F32)


def _batched_matmul(a, w, widx, n_batch):
    _, m, k = a.shape
    n = w.shape[-1]
    bm = _tile(m, 1024, 8)
    bn = _tile(n, 1024, LANES)
    return pl.pallas_call(
        _plain_body,
        grid=(n_batch, m // bm, n // bn),
        in_specs=[pl.BlockSpec((None, bm, k), lambda c, i, j: (c, i, 0)),
                  pl.BlockSpec((None,) * (len(widx) + 1) + (k, bn),
                               lambda c, i, j: widx + (c, 0, j))],
        out_specs=pl.BlockSpec((None, bm, bn), lambda c, i, j: (c, i, j)),
        out_shape=jax.ShapeDtypeStruct((n_batch, m, n), F32),
        compiler_params=_params("parallel", "parallel", "parallel"),
        name="matmul_rkv",
    )(a, w)


RWKV_TIME_BLOCK = 512
RWKV_PAIRS_PER_STEP = 4


def _bdot(a, b, dims=_NN):
    return lax.dot_general(a.astype(BF16), b.astype(BF16), dims, preferred_element_type=F32)


def _split_dot(a, b, split, passes):
    acc = None
    rem = (a, b)[split]
    for _ in range(passes):
        part = rem.astype(BF16)
        term = (lax.dot_general(part, b, _NN, preferred_element_type=F32) if split == 0 else
                lax.dot_general(a, part, _NN, preferred_element_type=F32))
        acc = term if acc is None else acc + term
        rem = rem - part.astype(F32)
    return acc


def _rwkv_body(r_ref, k_ref, v_ref, lw_ref, a_ref, g_ref, kk_ref, ka_ref, rk_ref, lnw_ref, lnb_ref,
               *rest, n_chunks, n_pairs, has_vres):
    L, N, W = CHUNK, HEAD, LANES
    if has_vres:
        vfirst_ref, vgate_ref, o_ref, s_ref = rest
    else:
        o_ref, s_ref = rest

    @pl.when(pl.program_id(2) == 0)
    def _():
        s_ref[...] = jnp.zeros_like(s_ref)

    def iota(shape, dim):
        return lax.broadcasted_iota(jnp.int32, shape, dim)

    same_head = (iota((W, W), 0) // N) == (iota((W, W), 1) // N)
    g_row, g_col = iota((2 * L, 2 * W), 0), iota((2 * L, 2 * W), 1)
    mask_g = (g_col % N) < (g_row % L) + (g_row >= L).astype(jnp.int32)
    n_double = max(1, (L - 1).bit_length())

    def stack(x):
        head0 = (iota(x.shape, 1) % W) < N
        zero = jnp.zeros_like(x)
        return jnp.concatenate([jnp.where(head0, x, zero), jnp.where(head0, zero, x)], axis=0)

    head0_lanes = iota((L, W), 1) < N
    step_idx = iota((L, W), 0)

    def seg_sum(x):
        s0 = jnp.sum(jnp.where(head0_lanes, x, 0.0), axis=-1, keepdims=True)
        s1 = jnp.sum(jnp.where(head0_lanes, 0.0, x), axis=-1, keepdims=True)
        return jnp.where(head0_lanes, s0, s1)

    def running_sum(x):
        shift = 1
        while shift < L:
            x = x + jnp.where(step_idx >= shift, pltpu.roll(x, shift, axis=0), 0.0)
            shift *= 2
        return x

    eye2 = ((iota((L, W), 1) % N) == iota((L, W), 0)).astype(F32)
    state = {pr: s_ref[pr] for pr in range(n_pairs)}

    def chunk_stages(c, pr):
        rows, lanes = slice(c * L, (c + 1) * L), slice(pr * W, (pr + 1) * W)
        r, k, v = r_ref[rows, lanes], k_ref[rows, lanes], v_ref[rows, lanes]
        if has_vres:
            v = v + (vfirst_ref[rows, lanes] - v) * vgate_ref[rows, lanes].astype(F32)
        lw, a = lw_ref[rows, lanes], a_ref[rows, lanes].astype(F32)
        kk_p, ka_p, rk_p = kk_ref[:, lanes], ka_ref[:, lanes], rk_ref[:, lanes]
        kkr = k * kk_p
        kh = k * (1.0 + (a - 1.0) * ka_p)
        ssq = seg_sum(kkr * kkr)
        lp = running_sum(lw)
        bonus_w = seg_sum(r * kh * rk_p)
        yield
        kk = kkr / jnp.maximum(jnp.sqrt(ssq), 1e-12)
        ba = kk * a
        lp_end = lp[L - 1:L, :]
        e_neg = jnp.exp(-lp)
        e_end = jnp.exp(lp_end - lp)
        at = -kk * jnp.exp(lp - lw)
        rt = r * jnp.exp(lp)
        v16 = v.astype(BF16)
        ar = jnp.concatenate([at, rt], axis=0)
        bk = jnp.concatenate([stack((ba * e_neg).astype(BF16)),
                              stack((kh * e_neg).astype(BF16))], axis=0)
        gm = _bdot(ar, bk, _NT)
        bk_end_t = jnp.concatenate([ba * e_end, kh * e_end], axis=0).T.astype(BF16)
        p_col = jnp.broadcast_to(jnp.exp(lp_end), (W, W)).T
        yield
        gm = jnp.where(mask_g, gm, 0.0)
        nk, gak = gm[:L, :W], gm[:L, W:]
        grb, grk = gm[L:, :W], gm[L:, W:]
        sv = stack(v16)
        nk16 = nk.astype(BF16)
        aakv = _bdot(gak, sv)
        npow = _bdot(nk16, stack(nk16))
        yield
        t = eye2 + nk
        for it in range(1, n_double):
            np16, t16 = npow.astype(BF16), t.astype(BF16)
            if it + 1 < n_double:
                p = _bdot(np16, jnp.concatenate([stack(np16), stack(t16)], axis=1))
                yield
                npow, t = p[:, :W], t + p[:, W:]
            else:
                p = _bdot(np16, stack(t16))
                yield
                t = t + p
        w16 = _bdot(t, stack(jnp.concatenate([at, aakv], axis=1).astype(BF16))).astype(BF16)
        yield
        p2 = _bdot(grb, stack(w16))
        p3 = _bdot(grk, sv)
        below = jnp.concatenate([jnp.zeros((L, W), BF16), v16], axis=1)
        mt = _bdot(bk_end_t, jnp.concatenate([w16, below], axis=0))
        yield
        rp = rt + p2[:, :W]
        yv = p2[:, W:] + p3
        zero = jnp.zeros((W, W), F32)
        m_bd_t = jnp.where(same_head, mt[:, :W], zero)
        sv_bd_t = jnp.where(same_head, mt[:, W:], zero)
        sst = state[pr]
        sst16 = sst.astype(BF16)
        ys = _bdot(rp, sst16)
        state[pr] = p_col * sst + _bdot(m_bd_t, sst16) + sv_bd_t
        yield
        y = yv + ys
        mean = seg_sum(y) * (1.0 / N)
        yield
        yc = y - mean
        var = seg_sum(yc * yc) * (1.0 / N)
        yield
        yn = yc * lax.rsqrt(var + GN_EPS) * lnw_ref[:, lanes] + lnb_ref[:, lanes]
        gate = g_ref[rows, lanes].astype(F32)
        o_ref[rows, lanes] = ((yn + bonus_w * v) * gate).astype(o_ref.dtype)

    pending = {(c, pr): chunk_stages(c, pr) for c in range(n_chunks) for pr in range(n_pairs)}
    slot = 0
    while pending:
        for key in sorted(pending):
            if key[0] <= slot and next(pending[key], True):
                del pending[key]
        slot += 1
    for pr in range(n_pairs):
        s_ref[pr] = state[pr]


def _rwkv_recurrence(rkv, v_mix, lw, a, g, k_k, k_a, r_k, ln_w, ln_b):
    _, b, t, d = rkv.shape
    tb = _tile(t, RWKV_TIME_BLOCK, CHUNK)
    width = _tile(d, RWKV_PAIRS_PER_STEP * LANES, LANES)
    tok = pl.BlockSpec((None, tb, width), lambda bi, hi, ti: (bi, ti, hi))
    par = pl.BlockSpec((1, width), lambda bi, hi, ti: (0, hi))

    def stacked(c):
        return pl.BlockSpec((None, None, tb, width), lambda bi, hi, ti: (c, bi, ti, hi))

    in_specs = [stacked(0), stacked(1), stacked(2), tok, tok, tok] + [par] * 5
    args = [rkv, rkv, rkv, lw, a, g, *(p.reshape(1, d) for p in (k_k, k_a, r_k, ln_w, ln_b))]
    if v_mix is not None:
        in_specs += [stacked(2), tok]
        args += list(v_mix)
    return pl.pallas_call(
        functools.partial(_rwkv_body, n_chunks=tb // CHUNK, n_pairs=width // LANES,
                          has_vres=v_mix is not None),
        grid=(b, d // width, t // tb),
        in_specs=in_specs,
        out_specs=tok,
        out_shape=jax.ShapeDtypeStruct((b, t, d), BF16),
        scratch_shapes=[pltpu.VMEM((width // LANES, LANES, LANES), F32)],
        compiler_params=_params("parallel", "parallel", "arbitrary"),
        name="rwkv7_recurrence",
    )(*args)


def _rwkv_layer(x, seq, v_first, norm_g, mu, w_rkv, w_o, widx, w0, w_l1, w_l2, a0, a_l1, a_l2,
                g_l1, g_l2, k_k, k_a, r_k, ln_w, ln_b, v_res, next_g):
    m, d = x.shape
    bsz = m // seq
    xmix, lw, a, g, *v_gate = _rwkv_front(x, norm_g, mu, seq, (w0, w_l1, w_l2), (a0, a_l1, a_l2),
                                          (g_l1, g_l2), v_res)
    rkv = _batched_matmul(xmix, w_rkv, widx, 3)
    sh = (bsz, seq, d)
    rkv4 = rkv.reshape((3,) + sh)
    if v_res is None:
        v_first = rkv4
        v_mix = None
    else:
        v_mix = (v_first, v_gate[0].reshape(sh))
    yg = _rwkv_recurrence(rkv4, v_mix, lw.reshape(sh), a.reshape(sh), g.reshape(sh),
                          k_k, k_a, r_k, ln_w, ln_b)
    return _resid_matmul(yg.reshape(m, d), w_o, widx, x, 1.0, next_g=next_g), v_first


S5_GROUPS_PER_STEP = 8


def _s5_body(u_ref, pwlo_ref, pwhi_ref, cc_ref, bbt_ref, rr_ref, ri_ref, dsk_ref, c1_ref, c2_ref,
             o_ref, tt_ref, *, n_chunks, state):
    L, C, P = S5_CHUNK, S5_GROUP, state
    LC = L * C
    lags_per_tile = LANES // C

    def iota(shape, dim):
        return lax.broadcasted_iota(jnp.int32, shape, dim)

    rep_lag = (iota((L, LC), 1) // C == iota((L, LC), 0)).astype(BF16)
    rep_chan = (iota((C, LANES), 1) % C == iota((C, LANES), 0)).astype(BF16)

    def rows_lag(x):
        return jnp.broadcast_to(x[:, None, :], (L, C, x.shape[1])).reshape(LC, x.shape[1])

    def group_stages(gi):
        u = u_ref[gi]
        pw_lo = _split_dot(pwlo_ref[gi], rep_lag, 0, 2)
        pw_hi = _split_dot(pwhi_ref[gi], rep_lag, 0, 1)
        cc = jnp.tile(_split_dot(cc_ref[gi], rep_chan, 0, 2), (1, LC // LANES))
        rr, ri = rows_lag(rr_ref[gi]), rows_lag(ri_ref[gi])
        bb = jnp.broadcast_to(bbt_ref[gi][None], (L, C, 2 * P)).reshape(LC, 2 * P)
        yield

        def c_times_power(pw):
            return jnp.concatenate([cc[:P] * pw[:P] - cc[P:] * pw[P:],
                                    -(cc[:P] * pw[P:] + cc[P:] * pw[:P])], axis=0)

        vt = c_times_power(pw_hi).astype(BF16)
        wt = (rr * bb + ri * pltpu.roll(bb, P, axis=1)).astype(BF16)
        taps = lax.dot_general(bbt_ref[gi], c_times_power(pw_lo), _NN,
                               precision=lax.Precision.HIGHEST,
                               preferred_element_type=F32)
        s = jnp.dot(u, wt, preferred_element_type=F32)
        yield
        on_diag = iota(taps.shape, 1) == iota(taps.shape, 0)
        taps = taps + jnp.where(on_diag, dsk_ref[gi], 0.0)
        padded = jnp.concatenate([jnp.zeros_like(taps), taps], axis=1)
        for sub in range(lags_per_tile):
            shifted = padded if sub == 0 else pltpu.roll(padded, sub * C, axis=1)
            shifted = shifted.astype(BF16)
            for q in range(L // lags_per_tile):
                lag = q * lags_per_tile + sub
                tt_ref[gi, lag * C:(lag + 1) * C, :] = (
                    shifted[:, LC - q * LANES:2 * LC - q * LANES])
        yield
        chunk_idx = iota(s.shape, 0) % n_chunks
        step = 1
        it = 0
        while step < n_chunks:
            sp = jnp.where(chunk_idx >= step, pltpu.roll(s, step, axis=0), 0.0)
            s = (s + c1_ref[gi, it:it + 1, :] * sp
                 + c2_ref[gi, it:it + 1, :] * pltpu.roll(sp, P, axis=1))
            step *= 2
            it += 1
            yield
        s_start = jnp.where(chunk_idx >= 1, pltpu.roll(s, 1, axis=0), 0.0)
        y = (jnp.dot(u, tt_ref[gi], preferred_element_type=F32)
             + jnp.dot(s_start.astype(BF16), vt, preferred_element_type=F32))
        yield
        o_ref[gi] = jax.nn.gelu(y).astype(o_ref.dtype)

    pending = [group_stages(gi) for gi in range(u_ref.shape[0])]
    while pending:
        pending = [gen for gen in pending if not next(gen, True)]


def _s5_tables(lam_re, lam_im, log_step, b_re, b_im, c_re, c_im, d_skip, n_chunks):
    g, p = lam_re.shape
    c = S5_GROUP
    L = S5_CHUNK
    lr = jnp.minimum(lam_re.astype(F32), LAMBDA_RE_MAX)
    li = lam_im.astype(F32)
    dt = jnp.exp(log_step.astype(F32))[:, None]
    ldt_re, ldt_im = lr * dt, li * dt
    mag = jnp.exp(ldt_re)
    ab_re, ab_im = mag * jnp.cos(ldt_im), mag * jnp.sin(ldt_im)
    den = lr * lr + li * li
    q_re = ((ab_re - 1.0) * lr + ab_im * li) / den
    q_im = (ab_im * lr - (ab_re - 1.0) * li) / den
    br, bi = b_re.astype(F32), b_im.astype(F32)
    bb_re = q_re[..., None] * br - q_im[..., None] * bi
    bb_im = q_re[..., None] * bi + q_im[..., None] * br
    dsk = d_skip.astype(F32).reshape(g, c)

    def power(n):
        m_ = jnp.exp(n * ldt_re)
        return m_ * jnp.cos(n * ldt_im), m_ * jnp.sin(n * ldt_im)

    lag = jnp.arange(0, L + 1, dtype=F32)[:, None, None]
    pw_re, pw_im = power(lag)
    pwt = jnp.concatenate([pw_re, pw_im], axis=-1).transpose(1, 2, 0)
    cc = jnp.concatenate([c_re.astype(F32), c_im.astype(F32)], axis=-1).transpose(0, 2, 1)
    bbt = jnp.concatenate([bb_re, bb_im], axis=1).transpose(0, 2, 1)
    rev_re = pw_re[::-1][1:].transpose(1, 0, 2)
    rev_im = pw_im[::-1][1:].transpose(1, 0, 2)
    rr = jnp.concatenate([rev_re, rev_re], axis=-1)
    ri = jnp.concatenate([-rev_im, rev_im], axis=-1)
    n_steps = max(1, (n_chunks - 1).bit_length())
    hop = (L * 2.0 ** jnp.arange(n_steps, dtype=F32))[:, None, None]
    hop_re, hop_im = power(hop)
    c1 = jnp.concatenate([hop_re, hop_re], axis=-1).transpose(1, 0, 2)
    c2 = jnp.concatenate([-hop_im, hop_im], axis=-1).transpose(1, 0, 2)
    return pwt[:, :, :L], pwt[:, :, 1:], cc, bbt, rr, ri, dsk[:, :, None], c1, c2


def _group_rows_body(h_ref, perm_ref, o_ref, *, rows, per_tile):
    lags = [h_ref[:, lag, :] for lag in range(h_ref.shape[1])]
    n_tiles = len(lags) // per_tile
    a = jnp.concatenate(
        [jnp.concatenate(lags[q * per_tile:(q + 1) * per_tile], axis=1) for q in range(n_tiles)],
        axis=0)
    res = jnp.dot(a, perm_ref[...], preferred_element_type=F32)
    for gi in range(per_tile):
        for q in range(n_tiles):
            o_ref[gi, :, q * LANES:(q + 1) * LANES] = (
                res[q * rows:(q + 1) * rows, gi * LANES:(gi + 1) * LANES].astype(o_ref.dtype))


def _group_rows(h, n_groups):
    m, d = h.shape
    c, L = S5_GROUP, S5_CHUNK
    rows = m // L
    per_tile = LANES // c
    perm = _granule_swap_matrix()
    return pl.pallas_call(
        functools.partial(_group_rows_body, rows=rows, per_tile=per_tile),
        grid=(d // LANES,),
        in_specs=[pl.BlockSpec((rows, L, LANES), lambda b: (0, 0, b)),
                  pl.BlockSpec(perm.shape, lambda b: (0, 0))],
        out_specs=pl.BlockSpec((per_tile, rows, L * c), lambda b: (b, 0, 0)),
        out_shape=jax.ShapeDtypeStruct((n_groups, rows, L * c), BF16),
        compiler_params=_params("parallel"),
        name="s5_group_rows",
    )(h.reshape(rows, L, d), perm)


def _ungroup_rows_body(y_ref, perm_ref, o_ref, *, rows, per_tile):
    n_tiles = o_ref.shape[1] // per_tile
    a = jnp.concatenate(
        [jnp.concatenate([y_ref[gi, :, q * LANES:(q + 1) * LANES] for gi in range(per_tile)],
                         axis=1) for q in range(n_tiles)], axis=0)
    res = jnp.dot(a, perm_ref[...], preferred_element_type=F32)
    for q in range(n_tiles):
        for s in range(0, per_tile, 2):
            lag = q * per_tile + s
            pair = jnp.stack([res[q * rows:(q + 1) * rows, s * LANES:(s + 1) * LANES],
                              res[q * rows:(q + 1) * rows, (s + 1) * LANES:(s + 2) * LANES]], axis=1)
            o_ref[:, lag:lag + 2, :] = pair.astype(o_ref.dtype)


def _granule_swap_matrix():
    c = S5_GROUP
    per_tile = LANES // c
    idx = jnp.arange(per_tile * LANES)
    hi, mid, lo = idx // LANES, (idx % LANES) // c, idx % c
    target = mid * LANES + hi * c + lo
    return (target[:, None] == idx[None, :]).astype(BF16)


def _ungroup_rows(y, d):
    g, rows, lc = y.shape
    c, L = S5_GROUP, S5_CHUNK
    per_tile = LANES // c
    perm = _granule_swap_matrix()
    out = pl.pallas_call(
        functools.partial(_ungroup_rows_body, rows=rows, per_tile=per_tile),
        grid=(d // LANES,),
        in_specs=[pl.BlockSpec((per_tile, rows, lc), lambda b: (b, 0, 0)),
                  pl.BlockSpec(perm.shape, lambda b: (0, 0))],
        out_specs=pl.BlockSpec((rows, L, LANES), lambda b: (0, 0, b)),
        out_shape=jax.ShapeDtypeStruct((rows, L, d), BF16),
        compiler_params=_params("parallel"),
        name="s5_ungroup_rows",
    )(y, perm)
    return out.reshape(rows * L, d)


def _s5_core(h, seq, lam_re, lam_im, log_step, b_re, b_im, c_re, c_im, d_skip):
    m, d = h.shape
    g, p = lam_re.shape
    c, L = S5_GROUP, S5_CHUNK
    n_chunks = seq // L
    rows = m // L
    tables = _s5_tables(lam_re, lam_im, log_step, b_re, b_im, c_re, c_im, d_skip, n_chunks)
    u = _group_rows(h, g)
    gb = _tile(g, S5_GROUPS_PER_STEP, 1)

    def grp(shape):
        return pl.BlockSpec((gb,) + tuple(shape), lambda gi: (gi, 0, 0))

    y = pl.pallas_call(
        functools.partial(_s5_body, n_chunks=n_chunks, state=p),
        grid=(g // gb,),
        in_specs=[grp((rows, L * c))] + [grp(t.shape[1:]) for t in tables],
        out_specs=grp((rows, L * c)),
        out_shape=jax.ShapeDtypeStruct((g, rows, L * c), BF16),
        scratch_shapes=[pltpu.VMEM((gb, L * c, L * c), BF16)],
        compiler_params=_params("parallel"),
        name="s5_chunk_scan",
    )(u, *tables)
    return _ungroup_rows(y, d)


def _ffn(x, pre, w_in, w_out, widx, norm_g, next_g):
    if pre is None:
        act = _dual_matmul(_rmsnorm(x, norm_g, BF16), w_in, widx, "swiglu")
    else:
        act = _dual_matmul(pre[0], w_in, widx, "swiglu", ssq=pre[1])
    return _resid_matmul(act, w_out, widx, x, 0.5, next_g=next_g)


def kernel(x, ffn_norm, ffn_w_in, ffn_w_out, mix_norm, rwkv_mu, rwkv_w_rkv, rwkv_w_o, rwkv_w0, rwkv_w_l1, rwkv_w_l2, rwkv_a0, rwkv_a_l1, rwkv_a_l2, rwkv_v0, rwkv_v_l1, rwkv_v_l2, rwkv_g_l1, rwkv_g_l2, rwkv_k_k, rwkv_k_a, rwkv_r_k, rwkv_ln_w, rwkv_ln_b, s5_lam_re, s5_lam_im, s5_log_step, s5_b_re, s5_b_im, s5_c_re, s5_c_im, s5_d, s5_w_glu, final_norm):
    bsz, seq, d = x.shape
    depth = ffn_norm.shape[0]
    n_mixers = 2
    w_in, w_out = ffn_w_in.astype(BF16), ffn_w_out.astype(BF16)
    w_rkv, w_o, w_glu = rwkv_w_rkv.astype(BF16), rwkv_w_o.astype(BF16), s5_w_glu.astype(BF16)
    x = x.reshape(bsz * seq, d)
    v_first = None
    pre = None
    for i in range(depth):
        x = _ffn(x, pre, w_in, w_out, (i, 0), ffn_norm[i, 0], None)
        j = i // n_mixers
        if i % n_mixers == 0:
            v_res = None if j == 0 else (rwkv_v0[j - 1], rwkv_v_l1[j - 1], rwkv_v_l2[j - 1])
            (x, xg, ssq), v_first = _rwkv_layer(
                x, seq, v_first, mix_norm[i], rwkv_mu[j], w_rkv, w_o, (j,),
                rwkv_w0[j], rwkv_w_l1[j], rwkv_w_l2[j], rwkv_a0[j], rwkv_a_l1[j], rwkv_a_l2[j],
                rwkv_g_l1[j], rwkv_g_l2[j], rwkv_k_k[j], rwkv_k_a[j], rwkv_r_k[j],
                rwkv_ln_w[j], rwkv_ln_b[j], v_res, ffn_norm[i, 1])
        else:
            h = _rmsnorm(x, mix_norm[i], BF16)
            y = _s5_core(h, seq, s5_lam_re[j], s5_lam_im[j], s5_log_step[j], s5_b_re[j],
                         s5_b_im[j], s5_c_re[j], s5_c_im[j], s5_d[j])
            x, xg, ssq = _dual_matmul(y, w_glu, (j,), "glu_residual", res=x,
                                      next_g=ffn_norm[i, 1])
        if i + 1 < depth:
            x, xg, ssq = _ffn(x, (xg, ssq), w_in, w_out, (i, 1), None, ffn_norm[i + 1, 0])
            pre = (xg, ssq)
        else:
            x = _ffn(x, (xg, ssq), w_in, w_out, (i, 1), None, None)
    return _rmsnorm(x, final_norm, F32).reshape(bsz, seq, d)
```
